```python
import math
import jax
import jax.numpy as jnp
from jax import lax
import numpy as np

D_MODEL = 1024
BATCH = 8
SEQ = 2048
DEPTH = 2
DEC_BATCH = 32
DEC_SEQ = 4
PAST_LEN = 16384
PAGE_SIZE = 128

N_MIXERS = 2
N_MLA = (DEPTH + 1) // 2
N_DN = DEPTH // 2

MLA_HEADS = 8
QK_NOPE = 128
QK_ROPE = 64
V_HEAD = 128
KV_LORA = 256
Q_LORA = 384
MLA_ROW = KV_LORA + QK_ROPE
MLA_SCALE = (QK_NOPE + QK_ROPE) ** -0.5
ROPE_THETA = 10000.0
Q_BLOCK = 128

DN_HEADS = 8
DN_DK = 128
DN_DV = 128
CONV_W = 4
DN_CHUNK = 64
DN_HK = DN_HEADS * DN_DK
DN_QKV = DN_HEADS * (2 * DN_DK + DN_DV)
DN_Z = DN_HEADS * DN_DV

D_FF = ((-(-8 * D_MODEL // 3) + 255) // 256) * 256

RMS_EPS = 1e-6
L2_EPS = 1e-6

kernel_name = "hybrid_mla_gated_deltanet_step"


def rmsnorm(x, w):
    x32 = x.astype(jnp.float32)
    y = x32 * lax.rsqrt(jnp.mean(x32 * x32, axis=-1, keepdims=True) + RMS_EPS)
    return (y * w.astype(jnp.float32)).astype(x.dtype)


def l2norm(x):
    return x * lax.rsqrt(jnp.sum(x * x, axis=-1, keepdims=True) + L2_EPS)


def rope(x, pos):
    half = x.shape[-1] // 2
    freq = ROPE_THETA ** (-jnp.arange(half, dtype=jnp.float32) / half)
    ang = pos.astype(jnp.float32)[:, None] * freq[None, :]
    bshape = (ang.shape[0],) + (1,) * (x.ndim - 3) + (half,)
    cos = jnp.cos(ang).reshape(bshape)
    sin = jnp.sin(ang).reshape(bshape)
    x32 = x.astype(jnp.float32)
    x1, x2 = x32[..., :half], x32[..., half:]
    return jnp.concatenate([x1 * cos - x2 * sin, x2 * cos + x1 * sin], axis=-1).astype(x.dtype)


def mla_project(u, pos, w_in, g_q, g_kv, w_uq, w_uk):
    b, t, _ = u.shape
    a = u @ w_in
    c_q = rmsnorm(a[..., :Q_LORA], g_q)
    c_kv = rmsnorm(a[..., Q_LORA:Q_LORA + KV_LORA], g_kv)
    k_r = rope(a[..., Q_LORA + KV_LORA:], pos)
    q = (c_q @ w_uq).reshape(b, t, MLA_HEADS, QK_NOPE + QK_ROPE)
    q_lat = jnp.einsum('bthn,hrn->bthr', q[..., :QK_NOPE], w_uk)
    q_rope = rope(q[..., QK_NOPE:], pos)
    rows = jnp.concatenate([c_kv, k_r], axis=-1)
    return q_lat, q_rope, rows


def mla_core(q_lat, q_rope, rows, q_pos, k_pos):
    c = rows[..., :KV_LORA]
    kr = rows[..., KV_LORA:]
    s = (jnp.einsum('bqhr,bkr->bhqk', q_lat, c)
         + jnp.einsum('bqhp,bkp->bhqk', q_rope, kr)).astype(jnp.float32) * MLA_SCALE
    s = jnp.where(k_pos[None, :] <= q_pos[:, None], s, -jnp.inf)
    p = jax.nn.softmax(s, axis=-1).astype(c.dtype)
    return jnp.einsum('bhqk,bkr->bqhr', p, c)


def mla_prompt_attention(q_lat, q_rope, rows, pos):
    b, t, h, r = q_lat.shape
    nb = t // Q_BLOCK

    def blk(x):
        return jnp.moveaxis(x.reshape(b, nb, Q_BLOCK, *x.shape[2:]), 1, 0)

    def one(args):
        ql, qr, qp = args
        return mla_core(ql, qr, rows, qp, pos)

    o = lax.map(one, (blk(q_lat), blk(q_rope), pos.reshape(nb, Q_BLOCK)))
    return jnp.moveaxis(o, 0, 1).reshape(b, t, h, r)


def mla_out(o_lat, w_uv, w_o):
    b, t, _, _ = o_lat.shape
    o = jnp.einsum('bthr,hrv->bthv', o_lat, w_uv).reshape(b, t, MLA_HEADS * V_HEAD)
    return o @ w_o


def gdn_chunked(q, k, v, g, beta, s0):
    b, t, h, _ = q.shape
    dv = v.shape[-1]
    n = t // DN_CHUNK
    c = DN_CHUNK

    def blk(x):
        return jnp.moveaxis(x.reshape(b, n, c, h, *x.shape[3:]), 3, 1)

    q, k, v, g, beta = blk(q), blk(k), blk(v), blk(g), blk(beta)
    gc = jnp.cumsum(g, axis=-1)
    idx = jnp.arange(c)
    causal = idx[:, None] >= idx[None, :]
    strict = idx[:, None] > idx[None, :]
    decay = jnp.exp(jnp.where(causal, gc[..., :, None] - gc[..., None, :], -jnp.inf))
    kb = k * beta[..., None]
    vb = v * beta[..., None]
    lower = jnp.where(strict, jnp.einsum('bhncd,bhnsd->bhncs', kb, k) * decay, 0.0)
    a_mat = jnp.eye(c, dtype=jnp.float32) + lower
    rhs = jnp.concatenate([vb, kb * jnp.exp(gc)[..., None]], axis=-1)
    sol = lax.linalg.triangular_solve(a_mat, rhs, left_side=True, lower=True, unit_diagonal=True)
    u_c, w_c = sol[..., :dv], sol[..., dv:]
    qk = jnp.where(causal, jnp.einsum('bhncd,bhnsd->bhncs', q, k) * decay, 0.0)
    q_dec = q * jnp.exp(gc)[..., None]
    k_dec = k * jnp.exp(gc[..., -1:] - gc)[..., None]
    g_last = jnp.exp(gc[..., -1])

    def step(s, xs):
        u_i, w_i, qk_i, qd_i, kd_i, gl_i = xs
        v_new = u_i - jnp.einsum('bhcd,bhde->bhce', w_i, s)
        o_i = jnp.einsum('bhcd,bhde->bhce', qd_i, s) + jnp.einsum('bhcs,bhse->bhce', qk_i, v_new)
        s = s * gl_i[..., None, None] + jnp.einsum('bhcd,bhce->bhde', kd_i, v_new)
        return s, o_i

    xs = tuple(jnp.moveaxis(x, 2, 0) for x in (u_c, w_c, qk, q_dec, k_dec, g_last))
    s, o = lax.scan(step, s0, xs)
    o = jnp.transpose(o, (1, 0, 3, 2, 4)).reshape(b, t, h, dv)
    return o, s


def gdn_recurrent(q, k, v, g, beta, s0):
    def step(s, xs):
        q_t, k_t, v_t, g_t, b_t = xs
        s = s * jnp.exp(g_t)[..., None, None]
        kv = jnp.einsum('bhd,bhde->bhe', k_t, s)
        delta = (v_t - kv) * b_t[..., None]
        s = s + jnp.einsum('bhd,bhe->bhde', k_t, delta)
        return s, jnp.einsum('bhd,bhde->bhe', q_t, s)

    xs = tuple(jnp.moveaxis(x, 1, 0) for x in (q, k, v, g, beta))
    s, o = lax.scan(step, s0, xs)
    return jnp.moveaxis(o, 0, 1), s


def dn_mixer(u, s0, conv0, w_in, conv_w, a_log, dt_bias, g_out, w_o, chunked):
    b, t, _ = u.shape
    proj = u @ w_in
    qkv_raw = proj[..., :DN_QKV]
    z = proj[..., DN_QKV:DN_QKV + DN_Z].reshape(b, t, DN_HEADS, DN_DV)
    b_raw = proj[..., DN_QKV + DN_Z:DN_QKV + DN_Z + DN_HEADS]
    a_raw = proj[..., DN_QKV + DN_Z + DN_HEADS:]
    xc = jnp.concatenate([conv0.astype(qkv_raw.dtype), qkv_raw], axis=1)
    new_conv = xc[:, xc.shape[1] - (CONV_W - 1):]
    qkv = lax.conv_general_dilated(xc, conv_w.astype(xc.dtype)[:, None, :], (1,), 'VALID',
                                   dimension_numbers=('NWC', 'WIO', 'NWC'),
                                   feature_group_count=DN_QKV)
    qkv = jax.nn.silu(qkv).astype(jnp.float32)
    q = l2norm(qkv[..., :DN_HK].reshape(b, t, DN_HEADS, DN_DK)) * (DN_DK ** -0.5)
    k = l2norm(qkv[..., DN_HK:2 * DN_HK].reshape(b, t, DN_HEADS, DN_DK))
    v = qkv[..., 2 * DN_HK:].reshape(b, t, DN_HEADS, DN_DV)
    beta = jax.nn.sigmoid(b_raw.astype(jnp.float32))
    g = -jnp.exp(a_log.astype(jnp.float32)) * jax.nn.softplus(
        a_raw.astype(jnp.float32) + dt_bias.astype(jnp.float32))
    core = gdn_chunked if chunked else gdn_recurrent
    o, s = core(q, k, v, g, beta, s0.astype(jnp.float32))
    o = rmsnorm(o, g_out).astype(u.dtype) * jax.nn.silu(z)
    return o.reshape(b, t, DN_HEADS * DN_DV) @ w_o, s, new_conv


def swiglu(u, w_in, w_out):
    gu = u @ w_in
    return (jax.nn.silu(gu[..., :D_FF]) * gu[..., D_FF:]) @ w_out


def setup_inputs(seed: int = 0) -> dict:
    key = jax.random.key(seed)
    ks = jax.random.split(key, 24)
    f32 = jnp.float32

    def nrm(i, shape, scale=1.0):
        return jax.random.normal(ks[i], shape, f32) * scale

    n_pages = PAST_LEN // PAGE_SIZE
    n_used = DEC_BATCH * n_pages
    n_pool = n_used + n_used // 4
    page_table = jax.random.permutation(ks[0], n_pool)[:n_used].reshape(DEC_BATCH, n_pages).astype(jnp.int32)
    dt = jnp.exp(jax.random.uniform(ks[1], (N_DN, DN_HEADS), f32, math.log(1e-3), math.log(1e-1)))
    return {
        "x_prompt": nrm(2, (BATCH, SEQ, D_MODEL)),
        "x_sample": nrm(3, (DEC_BATCH, DEC_SEQ, D_MODEL)),
        "cache_mla": nrm(4, (N_MLA, n_pool, PAGE_SIZE, MLA_ROW)),
        "state_dn": nrm(5, (N_DN, DEC_BATCH, DN_HEADS, DN_DK, DN_DV), 0.1),
        "state_dn_conv": nrm(6, (N_DN, DEC_BATCH, CONV_W - 1, DN_QKV)),
        "page_table": page_table,
        "norm_w": 1.0 + nrm(7, (DEPTH, 4, D_MODEL), 0.02),
        "mla_w_in": nrm(8, (N_MLA, D_MODEL, Q_LORA + MLA_ROW), D_MODEL ** -0.5),
        "mla_g_q": 1.0 + nrm(9, (N_MLA, Q_LORA), 0.02),
        "mla_g_kv": 1.0 + nrm(10, (N_MLA, KV_LORA), 0.02),
        "mla_w_uq": nrm(11, (N_MLA, Q_LORA, MLA_HEADS * (QK_NOPE + QK_ROPE)), Q_LORA ** -0.5),
        "mla_w_uk": nrm(12, (N_MLA, MLA_HEADS, KV_LORA, QK_NOPE), KV_LORA ** -0.5),
        "mla_w_uv": nrm(13, (N_MLA, MLA_HEADS, KV_LORA, V_HEAD), KV_LORA ** -0.5),
        "mla_w_o": nrm(14, (N_MLA, MLA_HEADS * V_HEAD, D_MODEL), (MLA_HEADS * V_HEAD) ** -0.5),
        "dn_w_in": nrm(15, (N_DN, D_MODEL, DN_QKV + DN_Z + 2 * DN_HEADS), D_MODEL ** -0.5),
        "dn_conv_w": nrm(16, (N_DN, CONV_W, DN_QKV), CONV_W ** -0.5),
        "dn_a_log": jnp.log(jax.random.uniform(ks[17], (N_DN, DN_HEADS), f32, 1.0, 16.0)),
        "dn_dt_bias": dt + jnp.log(-jnp.expm1(-dt)),
        "dn_g_out": 1.0 + nrm(18, (N_DN, DN_DV), 0.02),
        "dn_w_o": nrm(19, (N_DN, DN_HEADS * DN_DV, D_MODEL), (DN_HEADS * DN_DV) ** -0.5),
        "ffn_w_in": nrm(20, (DEPTH, D_MODEL, 2 * D_FF), D_MODEL ** -0.5),
        "ffn_w_out": nrm(21, (DEPTH, D_FF, D_MODEL), D_FF ** -0.5),
    }


def reference(x_prompt, x_sample, cache_mla, state_dn, state_dn_conv, page_table, norm_w,
              mla_w_in, mla_g_q, mla_g_kv, mla_w_uq, mla_w_uk, mla_w_uv, mla_w_o,
              dn_w_in, dn_conv_w, dn_a_log, dn_dt_bias, dn_g_out, dn_w_o,
              ffn_w_in, ffn_w_out):
    bp, tp, _ = x_prompt.shape
    bs, ts, _ = x_sample.shape
    past = page_table.shape[1] * PAGE_SIZE
    pos_p = jnp.arange(tp)
    pos_s = past + jnp.arange(ts)
    pos_all = jnp.arange(past + ts)
    hp, hs = x_prompt, x_sample
    rows_p_l, rows_s_l, sp_l, ss_l, cp_l, cs_l = [], [], [], [], [], []
    for layer in range(DEPTH):
        j = layer // N_MIXERS
        nw = norm_w[layer]
        up = rmsnorm(hp, nw[0])
        us = rmsnorm(hs, nw[0])
        if layer % N_MIXERS == 0:
            ql, qr, rows_p = mla_project(up, pos_p, mla_w_in[j], mla_g_q[j], mla_g_kv[j], mla_w_uq[j], mla_w_uk[j])
            o_p = mla_prompt_attention(ql, qr, rows_p, pos_p)
            ql, qr, rows_s = mla_project(us, pos_s, mla_w_in[j], mla_g_q[j], mla_g_kv[j], mla_w_uq[j], mla_w_uk[j])
            past_rows = cache_mla[j, page_table].reshape(bs, past, MLA_ROW)
            all_rows = jnp.concatenate([past_rows.astype(rows_s.dtype), rows_s], axis=1)
            o_s = mla_core(ql, qr, all_rows, pos_s, pos_all)
            mix_p = mla_out(o_p, mla_w_uv[j], mla_w_o[j])
            mix_s = mla_out(o_s, mla_w_uv[j], mla_w_o[j])
            rows_p_l.append(rows_p)
            rows_s_l.append(rows_s)
        else:
            s0_p = jnp.zeros((bp, DN_HEADS, DN_DK, DN_DV), jnp.float32)
            c0_p = jnp.zeros((bp, CONV_W - 1, DN_QKV), up.dtype)
            mix_p, s_p, c_p = dn_mixer(up, s0_p, c0_p, dn_w_in[j], dn_conv_w[j], dn_a_log[j],
                                       dn_dt_bias[j], dn_g_out[j], dn_w_o[j], True)
            mix_s, s_s, c_s = dn_mixer(us, state_dn[j], state_dn_conv[j], dn_w_in[j], dn_conv_w[j],
                                       dn_a_log[j], dn_dt_bias[j], dn_g_out[j], dn_w_o[j], False)
            sp_l.append(s_p.astype(state_dn.dtype))
            ss_l.append(s_s.astype(state_dn.dtype))
            cp_l.append(c_p.astype(state_dn_conv.dtype))
            cs_l.append(c_s.astype(state_dn_conv.dtype))
        hp = hp + rmsnorm(mix_p, nw[1])
        hs = hs + rmsnorm(mix_s, nw[1])
        hp = hp + rmsnorm(swiglu(rmsnorm(hp, nw[2]), ffn_w_in[layer], ffn_w_out[layer]), nw[3])
        hs = hs + rmsnorm(swiglu(rmsnorm(hs, nw[2]), ffn_w_in[layer], ffn_w_out[layer]), nw[3])
    mla_rows_prompt = jnp.stack(rows_p_l)
    mla_rows_sample = jnp.stack(rows_s_l)
    dn_state_prompt = jnp.stack(sp_l)
    dn_state_sample = jnp.stack(ss_l)
    dn_conv_prompt = jnp.stack(cp_l)
    dn_conv_sample = jnp.stack(cs_l)
    return (hp, hs, mla_rows_prompt, mla_rows_sample, dn_state_prompt, dn_state_sample, dn_conv_prompt, dn_conv_sample)
```

```python
import functools
import math

import jax
import jax.numpy as jnp
from jax import lax
from jax.experimental import pallas as pl
from jax.experimental.pallas import tpu as pltpu

F32 = jnp.float32
BF16 = jnp.bfloat16

D_MODEL = 1024
PAGE_SIZE = 128
N_MIXERS = 2

MLA_HEADS = 8
QK_NOPE = 128
QK_ROPE = 64
V_HEAD = 128
KV_LORA = 256
Q_LORA = 384
MLA_ROW = KV_LORA + QK_ROPE
MLA_SCALE = (QK_NOPE + QK_ROPE) ** -0.5
ROPE_THETA = 10000.0

DN_HEADS = 8
DN_DK = 128
DN_DV = 128
CONV_W = 4
DN_HK = DN_HEADS * DN_DK
DN_QKV = DN_HEADS * (2 * DN_DK + DN_DV)
DN_Z = DN_HEADS * DN_DV

RMS_EPS = 1e-6
L2_EPS = 1e-6

LANES = 128
SUBLANES = 8
VMEM_LIMIT_BYTES = 56 * 1024 * 1024

TM_TOKENS = 512
TM_DN_PROJ = 256
TQ_ATTN = 256
DN_CHUNK = 64
DN_CHUNK_SAMPLE = 16
DEC_PAGES_PER_STEP = 8
FFN_CHUNK = 256
CONV_CARRY_ROWS = SUBLANES


def _params(*sem):
    return pltpu.CompilerParams(dimension_semantics=sem, vmem_limit_bytes=VMEM_LIMIT_BYTES)


def _const_spec(shape):
    nd = len(shape)
    return pl.BlockSpec(shape, lambda *_: (0,) * nd)


def _dot(a, b):
    return jnp.dot(a, b, preferred_element_type=F32)


def _dot_nt(a, b):
    return lax.dot_general(a, b, (((1,), (1,)), ((), ())), preferred_element_type=F32)


def _dot_tn(a, b):
    return lax.dot_general(a, b, (((0,), (0,)), ((), ())), preferred_element_type=F32)


def _split2(x):
    hi = x.astype(BF16)
    lo = (x - hi.astype(F32)).astype(BF16)
    return hi, lo


def _split3(x):
    hi = x.astype(BF16)
    r = x - hi.astype(F32)
    mid = r.astype(BF16)
    lo = (r - mid.astype(F32)).astype(BF16)
    return hi, mid, lo


def _dot_hl(a, b):
    ah, al = _split2(a)
    bh, bl = _split2(b)
    return _dot(ah, bh) + (_dot(ah, bl) + _dot(al, bh))


def _rms(x, w):
    return x * lax.rsqrt(jnp.mean(x * x, axis=-1, keepdims=True) + RMS_EPS) * w


def _sigmoid(x):
    return 1.0 / (1.0 + jnp.exp(-x))


def _silu(x):
    return x * _sigmoid(x)


def _softplus(x):
    return jnp.maximum(x, 0.0) + jnp.log1p(jnp.exp(-jnp.abs(x)))


def _rope(x, cs, sn):
    half = x.shape[-1] // 2
    swapped = jnp.concatenate([x[:, half:], x[:, :half]], axis=1)
    return x * cs + swapped * sn


def _rope_tables(pos):
    half = QK_ROPE // 2
    freq = ROPE_THETA ** (-jnp.arange(half, dtype=F32) / half)
    ang = pos.astype(F32)[:, None] * freq[None, :]
    cos, sin = jnp.cos(ang), jnp.sin(ang)
    return jnp.concatenate([cos, cos], axis=1), jnp.concatenate([-sin, sin], axis=1)


def _mla_proj_body(x_ref, nw_ref, win_ref, gq_ref, gkv_ref, wuq_ref, wukt_ref, cs_ref, sn_ref,
                   rows_ref, kv_ref, q_ref):
    u = _rms(x_ref[...], nw_ref[...]).astype(BF16)
    a = _dot(u, win_ref[...])
    c_q = _rms(a[:, :Q_LORA], gq_ref[...]).astype(BF16)
    c_kv = _rms(a[:, Q_LORA:Q_LORA + KV_LORA], gkv_ref[...])
    cs = cs_ref[...]
    sn = sn_ref[...]
    k_r = _rope(a[:, Q_LORA + KV_LORA:], cs, sn)
    rows_ref[:, :KV_LORA] = c_kv
    rows_ref[:, KV_LORA:] = k_r
    kv_ref[:, :KV_LORA] = c_kv.astype(BF16)
    kv_ref[:, KV_LORA:] = k_r.astype(BF16)
    q = _dot(c_q, wuq_ref[...])
    for h in range(MLA_HEADS):
        base = h * (QK_NOPE + QK_ROPE)
        q_lat = _dot(q[:, base:base + QK_NOPE].astype(BF16), wukt_ref[h])
        q_rope = _rope(q[:, base + QK_NOPE:base + QK_NOPE + QK_ROPE], cs, sn)
        q_ref[h, :, :KV_LORA] = q_lat.astype(BF16)
        q_ref[h, :, KV_LORA:] = q_rope.astype(BF16)


def _mla_project(x, nw, w_in, g_q, g_kv, w_uq, w_ukt, cs, sn, tm):
    m = x.shape[0]
    row = lambda i: (i, 0)
    return pl.pallas_call(
        _mla_proj_body,
        grid=(m // tm,),
        in_specs=[
            pl.BlockSpec((tm, D_MODEL), row),
            _const_spec((1, D_MODEL)),
            _const_spec(w_in.shape),
            _const_spec((1, Q_LORA)),
            _const_spec((1, KV_LORA)),
            _const_spec(w_uq.shape),
            _const_spec(w_ukt.shape),
            pl.BlockSpec((tm, QK_ROPE), row),
            pl.BlockSpec((tm, QK_ROPE), row),
        ],
        out_specs=[
            pl.BlockSpec((tm, MLA_ROW), row),
            pl.BlockSpec((tm, MLA_ROW), row),
            pl.BlockSpec((MLA_HEADS, tm, MLA_ROW), lambda i: (0, i, 0)),
        ],
        out_shape=[
            jax.ShapeDtypeStruct((m, MLA_ROW), F32),
            jax.ShapeDtypeStruct((m, MLA_ROW), BF16),
            jax.ShapeDtypeStruct((MLA_HEADS, m, MLA_ROW), BF16),
        ],
        compiler_params=_params("parallel"),
        name="mla_project",
    )(x, nw, w_in, g_q, g_kv, w_uq, w_ukt, cs, sn)


def _softmax_update(s, v, m_ref, l_ref, acc_ref):
    m_prev = m_ref[...]
    m_new = jnp.maximum(m_prev, jnp.max(s, axis=1, keepdims=True))
    alpha = jnp.exp(m_prev - m_new)
    p = jnp.exp(s - m_new)
    l_ref[...] = alpha * l_ref[...] + jnp.sum(p, axis=1, keepdims=True)
    acc_ref[...] = alpha * acc_ref[...] + _dot(p.astype(BF16), v)
    m_ref[...] = m_new


def _softmax_init(m_ref, l_ref, acc_ref):
    m_ref[...] = jnp.full(m_ref.shape, -jnp.inf, F32)
    l_ref[...] = jnp.zeros(l_ref.shape, F32)
    acc_ref[...] = jnp.zeros(acc_ref.shape, F32)


def _attn_prompt_body(q_ref, kv_ref, o_ref, m_ref, l_ref, acc_ref, *, tq):
    i = pl.program_id(1)
    heads = q_ref.shape[0]
    rows = heads * tq
    q = q_ref[...].reshape(rows, MLA_ROW)
    _softmax_init(m_ref, l_ref, acc_ref)

    def step(j, masked):
        k = kv_ref[pl.ds(pl.multiple_of(j * tq, tq), tq), :]
        s = _dot_nt(q, k) * MLA_SCALE
        if masked:
            tok = lax.broadcasted_iota(jnp.int32, (rows, tq), 0) % tq
            key = lax.broadcasted_iota(jnp.int32, (rows, tq), 1)
            s = jnp.where(key <= tok, s, -jnp.inf)
        _softmax_update(s, k[:, :KV_LORA], m_ref, l_ref, acc_ref)

    def body(j, carry):
        step(j, False)
        return carry

    lax.fori_loop(0, i, body, 0)
    step(i, True)
    o = acc_ref[...] / l_ref[...]
    o_ref[...] = o.reshape(heads, tq, KV_LORA).astype(BF16)


def _attn_prompt(q, kv, batch, seq, tq):
    heads, m, _ = q.shape
    nq = seq // tq
    return pl.pallas_call(
        functools.partial(_attn_prompt_body, tq=tq),
        grid=(batch, nq),
        in_specs=[
            pl.BlockSpec((heads, tq, MLA_ROW), lambda b, i: (0, b * nq + i, 0)),
            pl.BlockSpec((seq, MLA_ROW), lambda b, i: (b, 0)),
        ],
        out_specs=pl.BlockSpec((heads, tq, KV_LORA), lambda b, i: (0, b * nq + i, 0)),
        out_shape=jax.ShapeDtypeStruct((heads, m, KV_LORA), BF16),
        scratch_shapes=[
            pltpu.VMEM((heads * tq, 1), F32),
            pltpu.VMEM((heads * tq, 1), F32),
            pltpu.VMEM((heads * tq, KV_LORA), F32),
        ],
        compiler_params=_params("parallel", "parallel"),
        name="mla_attn_prompt",
    )(q, kv)


def _attn_decode_body(pt_ref, q_ref, kn_ref, *rest, pages, ts):
    del pt_ref
    cache_refs = rest[:pages]
    o_ref, m_ref, l_ref, acc_ref = rest[pages:]
    c = pl.program_id(1)

    @pl.when(c == 0)
    def _():
        _softmax_init(m_ref, l_ref, acc_ref)

    q = q_ref[0]
    k = jnp.concatenate([r[0, 0].astype(BF16) for r in cache_refs], axis=0)
    _softmax_update(_dot_nt(q, k) * MLA_SCALE, k[:, :KV_LORA], m_ref, l_ref, acc_ref)

    @pl.when(c == pl.num_programs(1) - 1)
    def _():
        kn = kn_ref[0]
        s = _dot_nt(q, kn) * MLA_SCALE
        tok = lax.broadcasted_iota(jnp.int32, s.shape, 0) % ts
        key = lax.broadcasted_iota(jnp.int32, s.shape, 1)
        s = jnp.where(key <= tok, s, -jnp.inf)
        _softmax_update(s, kn[:, :KV_LORA], m_ref, l_ref, acc_ref)
        o_ref[0] = (acc_ref[...] / l_ref[...]).astype(BF16)


def _attn_decode(q, k_new, cache, page_table, layer_slot, ts):
    bs, rows, _ = q.shape
    n_pages = page_table.shape[1]
    pages = DEC_PAGES_PER_STEP
    assert n_pages % pages == 0
    cache_specs = [
        pl.BlockSpec((1, 1, PAGE_SIZE, MLA_ROW),
                     lambda b, c, pt, p=p: (layer_slot, pt[b, c * pages + p], 0, 0))
        for p in range(pages)
    ]
    grid_spec = pltpu.PrefetchScalarGridSpec(
        num_scalar_prefetch=1,
        grid=(bs, n_pages // pages),
        in_specs=[
            pl.BlockSpec((1, rows, MLA_ROW), lambda b, c, pt: (b, 0, 0)),
            pl.BlockSpec((1, k_new.shape[1], MLA_ROW), lambda b, c, pt: (b, 0, 0)),
        ] + cache_specs,
        out_specs=pl.BlockSpec((1, rows, KV_LORA), lambda b, c, pt: (b, 0, 0)),
        scratch_shapes=[
            pltpu.VMEM((rows, 1), F32),
            pltpu.VMEM((rows, 1), F32),
            pltpu.VMEM((rows, KV_LORA), F32),
        ],
    )
    return pl.pallas_call(
        functools.partial(_attn_decode_body, pages=pages, ts=ts),
        grid_spec=grid_spec,
        out_shape=jax.ShapeDtypeStruct((bs, rows, KV_LORA), BF16),
        compiler_params=_params("parallel", "arbitrary"),
        name="mla_attn_decode",
    )(page_table, q, k_new, *([cache] * pages))


def _mla_out_body(o_ref, wuv_ref, wo_ref, h_ref, nw_ref, out_ref):
    vs = [_dot(o_ref[h], wuv_ref[h]).astype(BF16) for h in range(MLA_HEADS)]
    mix = _dot(jnp.concatenate(vs, axis=1), wo_ref[...])
    out_ref[...] = h_ref[...] + _rms(mix, nw_ref[...])


def _mla_out(o_lat, w_uv, w_o, h, nw, tm):
    m = h.shape[0]
    row = lambda i: (i, 0)
    return pl.pallas_call(
        _mla_out_body,
        grid=(m // tm,),
        in_specs=[
            pl.BlockSpec((MLA_HEADS, tm, KV_LORA), lambda i: (0, i, 0)),
            _const_spec(w_uv.shape),
            _const_spec(w_o.shape),
            pl.BlockSpec((tm, D_MODEL), row),
            _const_spec((1, D_MODEL)),
        ],
        out_specs=pl.BlockSpec((tm, D_MODEL), row),
        out_shape=jax.ShapeDtypeStruct((m, D_MODEL), F32),
        compiler_params=_params("parallel"),
        name="mla_out",
    )(o_lat, w_uv, w_o, h, nw)


def _ffn_body(h_ref, nw_in_ref, win_ref, wout_ref, nw_out_ref, out_ref, *, d_ff, chunk):
    x = h_ref[...]
    u = _rms(x, nw_in_ref[...]).astype(BF16)
    acc = jnp.zeros(x.shape, F32)
    for c in range(d_ff // chunk):
        lo = c * chunk
        gate = _dot(u, win_ref[:, lo:lo + chunk])
        up = _dot(u, win_ref[:, d_ff + lo:d_ff + lo + chunk])
        act = (_silu(gate) * up).astype(BF16)
        acc = acc + _dot(act, wout_ref[lo:lo + chunk, :])
    out_ref[...] = x + _rms(acc, nw_out_ref[...])


def _ffn(h, nw_in, w_in, w_out, nw_out, tm):
    m = h.shape[0]
    d_ff = w_out.shape[0]
    assert d_ff % FFN_CHUNK == 0
    row = lambda i: (i, 0)
    return pl.pallas_call(
        functools.partial(_ffn_body, d_ff=d_ff, chunk=FFN_CHUNK),
        grid=(m // tm,),
        in_specs=[
            pl.BlockSpec((tm, D_MODEL), row),
            _const_spec((1, D_MODEL)),
            _const_spec(w_in.shape),
            _const_spec(w_out.shape),
            _const_spec((1, D_MODEL)),
        ],
        out_specs=pl.BlockSpec((tm, D_MODEL), row),
        out_shape=jax.ShapeDtypeStruct((m, D_MODEL), F32),
        compiler_params=_params("parallel"),
        name="ffn",
    )(h, nw_in, w_in, w_out, nw_out)


def _dn_proj_body(*refs, tm, seq_len, has_hist, tail_rows):
    if has_hist:
        (x_ref, nw_ref, wqkv_ref, wz_ref, wba_ref, cw_ref, alog_ref, dtb_ref, hist_ref,
         qkv_ref, z_ref, bg_ref, tail_ref, ext_ref) = refs
    else:
        (x_ref, nw_ref, wqkv_ref, wz_ref, wba_ref, cw_ref, alog_ref, dtb_ref,
         qkv_ref, z_ref, bg_ref, tail_ref, ext_ref) = refs
        hist_ref = None
    i = pl.program_id(0)
    carry = CONV_CARRY_ROWS

    @pl.when(i == 0)
    def _():
        ext_ref[0:carry, :] = jnp.zeros((carry, DN_QKV), F32)

    u = _rms(x_ref[...], nw_ref[...]).astype(BF16)
    raw = _dot(u, wqkv_ref[...])
    ext_ref[carry:carry + tm, :] = raw
    tpos = (i * tm + lax.broadcasted_iota(jnp.int32, (tm, 1), 0)) % seq_len
    cw = cw_ref[...]
    acc = raw * cw[CONV_W - 1:CONV_W, :]
    for k in range(1, CONV_W):
        shifted = ext_ref[carry - k:carry - k + tm, :]
        before = hist_ref[k - 1] if has_hist else 0.0
        acc = acc + jnp.where(tpos >= k, shifted, before) * cw[CONV_W - 1 - k:CONV_W - k, :]
    ext_ref[0:carry, :] = raw[tm - carry:, :]
    tail_ref[0] = raw[tm - tail_rows:, :]
    act = _silu(acc)
    for h in range(DN_HEADS):
        qh = act[:, h * DN_DK:(h + 1) * DN_DK]
        kh = act[:, DN_HK + h * DN_DK:DN_HK + (h + 1) * DN_DK]
        qn = qh * lax.rsqrt(jnp.sum(qh * qh, axis=-1, keepdims=True) + L2_EPS) * (DN_DK ** -0.5)
        kn = kh * lax.rsqrt(jnp.sum(kh * kh, axis=-1, keepdims=True) + L2_EPS)
        qkv_ref[:, h * DN_DK:(h + 1) * DN_DK] = qn
        qkv_ref[:, DN_HK + h * DN_DK:DN_HK + (h + 1) * DN_DK] = kn
    qkv_ref[:, 2 * DN_HK:] = act[:, 2 * DN_HK:]
    z_ref[...] = _dot(u, wz_ref[...])
    ba = _dot(u, wba_ref[...])
    beta = _sigmoid(ba[:, :DN_HEADS])
    g = -jnp.exp(alog_ref[...]) * _softplus(ba[:, DN_HEADS:2 * DN_HEADS] + dtb_ref[...])
    bg_ref[:, :DN_HEADS] = beta
    bg_ref[:, DN_HEADS:] = g


def _dn_project(x, nw, w_qkv, w_z, w_ba, conv_w, a_log, dt_bias, hist, tm, seq_len, tail_rows):
    m = x.shape[0]
    n_tiles = m // tm
    has_hist = hist is not None
    tiles_per_tail = max(seq_len // tm, 1)
    row = lambda i: (i, 0)
    in_specs = [
        pl.BlockSpec((tm, D_MODEL), row),
        _const_spec((1, D_MODEL)),
        _const_spec(w_qkv.shape),
        _const_spec(w_z.shape),
        _const_spec(w_ba.shape),
        _const_spec(conv_w.shape),
        _const_spec((1, DN_HEADS)),
        _const_spec((1, DN_HEADS)),
    ]
    args = [x, nw, w_qkv, w_z, w_ba, conv_w, a_log, dt_bias]
    if has_hist:
        in_specs.append(pl.BlockSpec((CONV_W - 1, tm, DN_QKV), lambda i: (0, i, 0)))
        args.append(hist)
    return pl.pallas_call(
        functools.partial(_dn_proj_body, tm=tm, seq_len=seq_len, has_hist=has_hist,
                          tail_rows=tail_rows),
        grid=(n_tiles,),
        in_specs=in_specs,
        out_specs=[
            pl.BlockSpec((tm, DN_QKV), row),
            pl.BlockSpec((tm, DN_Z), row),
            pl.BlockSpec((tm, 2 * DN_HEADS), row),
            pl.BlockSpec((1, tail_rows, DN_QKV), lambda i: (i // tiles_per_tail, 0, 0)),
        ],
        out_shape=[
            jax.ShapeDtypeStruct((m, DN_QKV), F32),
            jax.ShapeDtypeStruct((m, DN_Z), F32),
            jax.ShapeDtypeStruct((m, 2 * DN_HEADS), F32),
            jax.ShapeDtypeStruct((n_tiles // tiles_per_tail, tail_rows, DN_QKV), F32),
        ],
        scratch_shapes=[pltpu.VMEM((CONV_CARRY_ROWS + tm, DN_QKV), F32)],
        compiler_params=_params("arbitrary"),
        name="dn_project",
    )(*args)


def _gdn_body(q_ref, k_ref, v_ref, bg_ref, bgt_ref, s0_ref, o_ref, s_ref, *, chunk):
    c = pl.program_id(1)

    @pl.when(c == 0)
    def _():
        s_ref[...] = s0_ref[...]

    n_heads = DN_HEADS
    ri = lax.broadcasted_iota(jnp.int32, (chunk, chunk), 0)
    ci = lax.broadcasted_iota(jnp.int32, (chunk, chunk), 1)
    causal = ri >= ci
    strict = ri > ci
    eye = jnp.where(ri == ci, 1.0, 0.0).astype(F32)
    tril = jnp.where(causal, 1.0, 0.0).astype(BF16)
    triu = jnp.where(ri <= ci, 1.0, 0.0).astype(BF16)
    bg = bg_ref[...]
    bgt = bgt_ref[0]
    gc_col = sum(_dot(tril, part) for part in _split3(bg))
    gc_row = sum(_dot(part, triu) for part in _split3(bgt))
    n_double = int(math.log2(chunk)) - 1
    for h in range(n_heads):
        sl = slice(h * DN_DK, (h + 1) * DN_DK)
        q = q_ref[:, sl]
        k = k_ref[:, sl]
        v = v_ref[:, sl]
        beta = bg[:, h:h + 1]
        gcc = gc_col[:, n_heads + h:n_heads + h + 1]
        gcr = gc_row[n_heads + h:n_heads + h + 1, :]
        decay = jnp.where(causal, jnp.exp(gcc - gcr), 0.0)
        eg = jnp.exp(gcc)
        kb = k * beta
        vb = v * beta
        k16 = k.astype(BF16)
        lower = jnp.where(strict, _dot_nt(kb.astype(BF16), k16) * decay, 0.0)
        power = lower
        inv = eye - lower
        for _ in range(n_double):
            power = _dot_hl(power, power)
            inv = inv + _dot_hl(inv, power)
        sol = _dot_hl(inv, jnp.concatenate([vb, kb * eg], axis=1))
        u = sol[:, :DN_DV]
        w = sol[:, DN_DV:]
        qk = jnp.where(causal, _dot_nt(q.astype(BF16), k16) * decay, 0.0)
        g_last = gcc[chunk - 1:chunk, :]
        k_dec = k * jnp.exp(g_last - gcc)
        s = s_ref[0, h]
        s16 = s.astype(BF16)
        v_new = u - _dot(w.astype(BF16), s16)
        v16 = v_new.astype(BF16)
        o_ref[:, sl] = _dot((q * eg).astype(BF16), s16) + _dot(qk.astype(BF16), v16)
        s_ref[0, h] = s * jnp.exp(g_last) + _dot_tn(k_dec.astype(BF16), v16)


def _gdn(qkv, bg, s0, n_seq, seq_len, chunk):
    m = qkv.shape[0]
    nc = seq_len // chunk
    bgt = bg.reshape(m // chunk, chunk, 2 * DN_HEADS).transpose(0, 2, 1)
    blk = lambda col: pl.BlockSpec((chunk, DN_HK), lambda b, c, col=col: (b * nc + c, col))
    state_spec = pl.BlockSpec((1, DN_HEADS, DN_DK, DN_DV), lambda b, c: (b, 0, 0, 0))
    return pl.pallas_call(
        functools.partial(_gdn_body, chunk=chunk),
        grid=(n_seq, nc),
        in_specs=[
            blk(0), blk(1), blk(2),
            pl.BlockSpec((chunk, 2 * DN_HEADS), lambda b, c: (b * nc + c, 0)),
            pl.BlockSpec((1, 2 * DN_HEADS, chunk), lambda b, c: (b * nc + c, 0, 0)),
            state_spec,
        ],
        out_specs=[
            pl.BlockSpec((chunk, DN_Z), lambda b, c: (b * nc + c, 0)),
            state_spec,
        ],
        out_shape=[
            jax.ShapeDtypeStruct((m, DN_Z), F32),
            jax.ShapeDtypeStruct(s0.shape, F32),
        ],
        compiler_params=_params("parallel", "arbitrary"),
        name="gdn_chunk",
    )(qkv, qkv, qkv, bg, bgt, s0)


def _dn_out_body(o_ref, z_ref, gout_ref, wo_ref, h_ref, nw_ref, out_ref):
    gout = gout_ref[...]
    ys = []
    for h in range(DN_HEADS):
        sl = slice(h * DN_DV, (h + 1) * DN_DV)
        ys.append((_rms(o_ref[:, sl], gout) * _silu(z_ref[:, sl])).astype(BF16))
    mix = _dot(jnp.concatenate(ys, axis=1), wo_ref[...])
    out_ref[...] = h_ref[...] + _rms(mix, nw_ref[...])


def _dn_out(o, z, g_out, w_o, h, nw, tm):
    m = h.shape[0]
    row = lambda i: (i, 0)
    return pl.pallas_call(
        _dn_out_body,
        grid=(m // tm,),
        in_specs=[
            pl.BlockSpec((tm, DN_Z), row),
            pl.BlockSpec((tm, DN_Z), row),
            _const_spec((1, DN_DV)),
            _const_spec(w_o.shape),
            pl.BlockSpec((tm, D_MODEL), row),
            _const_spec((1, D_MODEL)),
        ],
        out_specs=pl.BlockSpec((tm, D_MODEL), row),
        out_shape=jax.ShapeDtypeStruct((m, D_MODEL), F32),
        compiler_params=_params("parallel"),
        name="dn_out",
    )(o, z, g_out, w_o, h, nw)


def _mla_layer(hp, hs, cache_mla, page_table, slot, nw, w_in, g_q, g_kv, w_uq, w_uk, w_uv, w_o,
               bp, tp, bs, ts):
    past = page_table.shape[1] * PAGE_SIZE
    w_in16 = w_in.astype(BF16)
    w_uq16 = w_uq.astype(BF16)
    w_ukt16 = jnp.swapaxes(w_uk, 1, 2).astype(BF16)
    w_uv16 = w_uv.astype(BF16)
    w_o16 = w_o.astype(BF16)
    g_q = g_q.reshape(1, Q_LORA)
    g_kv = g_kv.reshape(1, KV_LORA)
    nw0 = nw[0].reshape(1, D_MODEL)
    nw1 = nw[1].reshape(1, D_MODEL)

    cs_p, sn_p = _rope_tables(jnp.tile(jnp.arange(tp), bp))
    cs_s, sn_s = _rope_tables(past + jnp.tile(jnp.arange(ts), bs))
    ms = bs * ts
    rows_p, kv_p, q_p = _mla_project(hp, nw0, w_in16, g_q, g_kv, w_uq16, w_ukt16, cs_p, sn_p,
                                     TM_TOKENS)
    rows_s, kv_s, q_s = _mla_project(hs, nw0, w_in16, g_q, g_kv, w_uq16, w_ukt16, cs_s, sn_s, ms)

    o_p = _attn_prompt(q_p, kv_p, bp, tp, TQ_ATTN)

    q_sb = q_s.reshape(MLA_HEADS, bs, ts, MLA_ROW).transpose(1, 0, 2, 3).reshape(
        bs, MLA_HEADS * ts, MLA_ROW)
    new_rows = 2 * SUBLANES
    k_new = jnp.pad(kv_s.reshape(bs, ts, MLA_ROW), ((0, 0), (0, new_rows - ts), (0, 0)))
    o_sb = _attn_decode(q_sb, k_new, cache_mla, page_table, slot, ts)
    o_s = o_sb.reshape(bs, MLA_HEADS, ts, KV_LORA).transpose(1, 0, 2, 3).reshape(
        MLA_HEADS, ms, KV_LORA)

    hp = _mla_out(o_p, w_uv16, w_o16, hp, nw1, TM_TOKENS)
    hs = _mla_out(o_s, w_uv16, w_o16, hs, nw1, ms)
    return hp, hs, rows_p.reshape(bp, tp, MLA_ROW), rows_s.reshape(bs, ts, MLA_ROW)


def _dn_layer(hp, hs, s0_s, conv0_s, nw, w_in, conv_w, a_log, dt_bias, g_out, w_o, bp, tp, bs, ts):
    w_qkv16 = w_in[:, :DN_QKV].astype(BF16)
    w_z16 = w_in[:, DN_QKV:DN_QKV + DN_Z].astype(BF16)
    w_ba16 = jnp.pad(w_in[:, DN_QKV + DN_Z:], ((0, 0), (0, LANES - 2 * DN_HEADS))).astype(BF16)
    w_o16 = w_o.astype(BF16)
    a_log = a_log.reshape(1, DN_HEADS).astype(F32)
    dt_bias = dt_bias.reshape(1, DN_HEADS).astype(F32)
    g_out = g_out.reshape(1, DN_DV)
    nw0 = nw[0].reshape(1, D_MODEL)
    nw1 = nw[1].reshape(1, D_MODEL)
    hist_rows = CONV_W - 1
    ms = bs * ts

    qkv_p, z_p, bg_p, tail_p = _dn_project(hp, nw0, w_qkv16, w_z16, w_ba16, conv_w, a_log, dt_bias,
                                           None, TM_DN_PROJ, tp, SUBLANES)
    conv_p = tail_p[:, SUBLANES - hist_rows:, :]
    s0_p = jnp.zeros((bp, DN_HEADS, DN_DK, DN_DV), F32)
    o_p, s_p = _gdn(qkv_p, bg_p, s0_p, bp, tp, DN_CHUNK)
    hp = _dn_out(o_p, z_p, g_out, w_o16, hp, nw1, TM_TOKENS)

    tok = jnp.arange(ts)
    hist = jnp.stack([
        conv0_s[:, jnp.clip(hist_rows - k + tok, 0, hist_rows - 1), :].reshape(ms, DN_QKV)
        for k in range(1, CONV_W)
    ]).astype(F32)
    qkv_s, z_s, bg_s, tail_s = _dn_project(hs, nw0, w_qkv16, w_z16, w_ba16, conv_w, a_log, dt_bias,
                                           hist, ms, ts, ms)
    raw_s = tail_s.reshape(bs, ts, DN_QKV)
    conv_s = jnp.concatenate([conv0_s.astype(F32), raw_s], axis=1)[:, ts:, :]
    cs = DN_CHUNK_SAMPLE
    pad = lambda a: jnp.pad(a.reshape(bs, ts, -1), ((0, 0), (0, cs - ts), (0, 0))).reshape(
        bs * cs, -1)
    o_s_pad, s_s = _gdn(pad(qkv_s), pad(bg_s), s0_s.astype(F32), bs, cs, cs)
    o_s = o_s_pad.reshape(bs, cs, DN_Z)[:, :ts, :].reshape(ms, DN_Z)
    hs = _dn_out(o_s, z_s, g_out, w_o16, hs, nw1, ms)
    return hp, hs, s_p, s_s, conv_p, conv_s


def kernel(x_prompt, x_sample, cache_mla, state_dn, state_dn_conv, page_table, norm_w, mla_w_in,
           mla_g_q, mla_g_kv, mla_w_uq, mla_w_uk, mla_w_uv, mla_w_o, dn_w_in, dn_conv_w, dn_a_log,
           dn_dt_bias, dn_g_out, dn_w_o, ffn_w_in, ffn_w_out):
    bp, tp, _ = x_prompt.shape
    bs, ts, _ = x_sample.shape
    depth = norm_w.shape[0]
    hp = x_prompt.reshape(bp * tp, D_MODEL)
    hs = x_sample.reshape(bs * ts, D_MODEL)
    rows_p_l, rows_s_l, sp_l, ss_l, cp_l, cs_l = [], [], [], [], [], []
    for layer in range(depth):
        j = layer // N_MIXERS
        nw = norm_w[layer]
        if layer % N_MIXERS == 0:
            hp, hs, rows_p, rows_s = _mla_layer(
                hp, hs, cache_mla, page_table, j, nw, mla_w_in[j], mla_g_q[j], mla_g_kv[j],
                mla_w_uq[j], mla_w_uk[j], mla_w_uv[j], mla_w_o[j], bp, tp, bs, ts)
            rows_p_l.append(rows_p)
            rows_s_l.append(rows_s)
        else:
            hp, hs, s_p, s_s, c_p, c_s = _dn_layer(
                hp, hs, state_dn[j], state_dn_conv[j], nw, dn_w_in[j], dn_conv_w[j], dn_a_log[j],
                dn_dt_bias[j], dn_g_out[j], dn_w_o[j], bp, tp, bs, ts)
            sp_l.append(s_p.astype(state_dn.dtype))
            ss_l.append(s_s.astype(state_dn.dtype))
            cp_l.append(c_p.astype(state_dn_conv.dtype))
            cs_l.append(c_s.astype(state_dn_conv.dtype))
        w_in16 = ffn_w_in[layer].astype(BF16)
        w_out16 = ffn_w_out[layer].astype(BF16)
        nw2 = nw[2].reshape(1, D_MODEL)
        nw3 = nw[3].reshape(1, D_MODEL)
        hp = _ffn(hp, nw2, w_in16, w_out16, nw3, TM_TOKENS)
        hs = _ffn(hs, nw2, w_in16, w_out16, nw3, bs * ts)
    return (hp.reshape(bp, tp, D_MODEL), hs.reshape(bs, ts, D_MODEL),
            jnp.stack(rows_p_l), jnp.stack(rows_s_l), jnp.stack(sp_l), jnp.stack(ss_l),
            jnp.stack(cp_l), jnp.stack(cs_l))
```

```python
import functools
import math

import jax
import jax.numpy as jnp
from jax import lax
from jax.experimental import pallas as pl
from jax.experimental.pallas import tpu as pltpu

F32 = jnp.float32
BF16 = jnp.bfloat16

D_MODEL = 1024
PAGE_SIZE = 128
N_MIXERS = 2

MLA_HEADS = 8
QK_NOPE = 128
QK_ROPE = 64
V_HEAD = 128
KV_LORA = 256
Q_LORA = 384
MLA_ROW = KV_LORA + QK_ROPE
MLA_SCALE = (QK_NOPE + QK_ROPE) ** -0.5
ROPE_THETA = 10000.0

DN_HEADS = 8
DN_DK = 128
DN_DV = 128
CONV_W = 4
DN_HK = DN_HEADS * DN_DK
DN_QKV = DN_HEADS * (2 * DN_DK + DN_DV)
DN_Z = DN_HEADS * DN_DV

RMS_EPS = 1e-6
L2_EPS = 1e-6

LANES = 128
SUBLANES = 8
VMEM_LIMIT_BYTES = 56 * 1024 * 1024

TM_TOKENS = 512
TM_DN_PROJ = 256
TQ_ATTN = 256
DN_CHUNK = 64
GDN_GROUP = 4
GDN_CHUNKS_PER_STEP = 2
DEC_PAGES_PER_STEP = 32
FFN_CHUNK = 256
CONV_CARRY_ROWS = SUBLANES


def _params(*sem):
    return pltpu.CompilerParams(dimension_semantics=sem, vmem_limit_bytes=VMEM_LIMIT_BYTES)


def _const_spec(shape):
    nd = len(shape)
    return pl.BlockSpec(shape, lambda *_: (0,) * nd)


def _dot(a, b):
    return jnp.dot(a, b, preferred_element_type=F32)


def _dot_nt(a, b):
    return lax.dot_general(a, b, (((1,), (1,)), ((), ())), preferred_element_type=F32)


def _dot_tn(a, b):
    return lax.dot_general(a, b, (((0,), (0,)), ((), ())), preferred_element_type=F32)


def _split2(x):
    hi = x.astype(BF16)
    lo = (x - hi.astype(F32)).astype(BF16)
    return hi, lo


def _split3(x):
    hi = x.astype(BF16)
    r = x - hi.astype(F32)
    mid = r.astype(BF16)
    lo = (r - mid.astype(F32)).astype(BF16)
    return hi, mid, lo


def _dot_hl(a, b):
    ah, al = _split2(a)
    bh, bl = _split2(b)
    return _dot(ah, bh) + (_dot(ah, bl) + _dot(al, bh))


def _rms(x, w):
    return x * lax.rsqrt(jnp.mean(x * x, axis=-1, keepdims=True) + RMS_EPS) * w


def _sigmoid(x):
    return 1.0 / (1.0 + jnp.exp(-x))


def _silu(x):
    return x * _sigmoid(x)


def _softplus(x):
    return jnp.maximum(x, 0.0) + jnp.log1p(jnp.exp(-jnp.abs(x)))


def _rope(x, cs, sn):
    half = x.shape[-1] // 2
    swapped = jnp.concatenate([x[:, half:], x[:, :half]], axis=1)
    return x * cs + swapped * sn


def _rope_tables(pos):
    half = QK_ROPE // 2
    freq = ROPE_THETA ** (-jnp.arange(half, dtype=F32) / half)
    ang = pos.astype(F32)[:, None] * freq[None, :]
    cos, sin = jnp.cos(ang), jnp.sin(ang)
    return jnp.concatenate([cos, cos], axis=1), jnp.concatenate([-sin, sin], axis=1)


def _mla_proj_body(x_ref, nw_ref, win_ref, gq_ref, gkv_ref, wuq_ref, wukt_ref, cs_ref, sn_ref,
                   rows_ref, kv_ref, q_ref):
    u = _rms(x_ref[...], nw_ref[...]).astype(BF16)
    a = _dot(u, win_ref[...])
    c_q = _rms(a[:, :Q_LORA], gq_ref[...]).astype(BF16)
    c_kv = _rms(a[:, Q_LORA:Q_LORA + KV_LORA], gkv_ref[...])
    cs = cs_ref[...]
    sn = sn_ref[...]
    k_r = _rope(a[:, Q_LORA + KV_LORA:], cs, sn)
    rows_ref[:, :KV_LORA] = c_kv
    rows_ref[:, KV_LORA:] = k_r
    kv_ref[:, :KV_LORA] = c_kv.astype(BF16)
    kv_ref[:, KV_LORA:] = k_r.astype(BF16)
    q = _dot(c_q, wuq_ref[...])
    for h in range(MLA_HEADS):
        base = h * (QK_NOPE + QK_ROPE)
        q_lat = _dot(q[:, base:base + QK_NOPE].astype(BF16), wukt_ref[h])
        q_rope = _rope(q[:, base + QK_NOPE:base + QK_NOPE + QK_ROPE], cs, sn)
        q_ref[h, :, :KV_LORA] = q_lat.astype(BF16)
        q_ref[h, :, KV_LORA:] = q_rope.astype(BF16)


def _mla_project(x, nw, w_in, g_q, g_kv, w_uq, w_ukt, cs, sn, tm):
    m = x.shape[0]
    row = lambda i: (i, 0)
    return pl.pallas_call(
        _mla_proj_body,
        grid=(m // tm,),
        in_specs=[
            pl.BlockSpec((tm, D_MODEL), row),
            _const_spec((1, D_MODEL)),
            _const_spec(w_in.shape),
            _const_spec((1, Q_LORA)),
            _const_spec((1, KV_LORA)),
            _const_spec(w_uq.shape),
            _const_spec(w_ukt.shape),
            pl.BlockSpec((tm, QK_ROPE), row),
            pl.BlockSpec((tm, QK_ROPE), row),
        ],
        out_specs=[
            pl.BlockSpec((tm, MLA_ROW), row),
            pl.BlockSpec((tm, MLA_ROW), row),
            pl.BlockSpec((MLA_HEADS, tm, MLA_ROW), lambda i: (0, i, 0)),
        ],
        out_shape=[
            jax.ShapeDtypeStruct((m, MLA_ROW), F32),
            jax.ShapeDtypeStruct((m, MLA_ROW), BF16),
            jax.ShapeDtypeStruct((MLA_HEADS, m, MLA_ROW), BF16),
        ],
        compiler_params=_params("parallel"),
        name="mla_project",
    )(x, nw, w_in, g_q, g_kv, w_uq, w_ukt, cs, sn)


def _softmax_update(s, v, m_ref, l_ref, acc_ref, v_transposed=False):
    m_prev = m_ref[...]
    m_new = jnp.maximum(m_prev, jnp.max(s, axis=1, keepdims=True))
    alpha = jnp.exp(m_prev - m_new)
    p = jnp.exp(s - m_new)
    p16 = p.astype(BF16)
    pv = _dot_nt(p16, v) if v_transposed else _dot(p16, v)
    l_ref[...] = alpha * l_ref[...] + jnp.sum(p, axis=1, keepdims=True)
    acc_ref[...] = alpha * acc_ref[...] + pv
    m_ref[...] = m_new


def _softmax_init(m_ref, l_ref, acc_ref):
    m_ref[...] = jnp.full(m_ref.shape, -jnp.inf, F32)
    l_ref[...] = jnp.zeros(l_ref.shape, F32)
    acc_ref[...] = jnp.zeros(acc_ref.shape, F32)


def _attn_prompt_body(q_ref, kv_ref, o_ref, m_ref, l_ref, acc_ref, *, tq):
    i = pl.program_id(1)
    heads = q_ref.shape[0]
    rows = heads * tq
    q = q_ref[...].reshape(rows, MLA_ROW)
    _softmax_init(m_ref, l_ref, acc_ref)

    def step(j, masked):
        k = kv_ref[pl.ds(pl.multiple_of(j * tq, tq), tq), :]
        s = _dot_nt(q, k) * MLA_SCALE
        if masked:
            tok = lax.broadcasted_iota(jnp.int32, (rows, tq), 0) % tq
            key = lax.broadcasted_iota(jnp.int32, (rows, tq), 1)
            s = jnp.where(key <= tok, s, -jnp.inf)
        _softmax_update(s, k[:, :KV_LORA], m_ref, l_ref, acc_ref)

    def body(j, carry):
        step(j, False)
        return carry

    lax.fori_loop(0, i, body, 0)
    step(i, True)
    o = acc_ref[...] / l_ref[...]
    o_ref[...] = o.reshape(heads, tq, KV_LORA).astype(BF16)


def _attn_prompt(q, kv, batch, seq, tq):
    heads, m, _ = q.shape
    nq = seq // tq
    return pl.pallas_call(
        functools.partial(_attn_prompt_body, tq=tq),
        grid=(batch, nq),
        in_specs=[
            pl.BlockSpec((heads, tq, MLA_ROW), lambda b, i: (0, b * nq + i, 0)),
            pl.BlockSpec((seq, MLA_ROW), lambda b, i: (b, 0)),
        ],
        out_specs=pl.BlockSpec((heads, tq, KV_LORA), lambda b, i: (0, b * nq + i, 0)),
        out_shape=jax.ShapeDtypeStruct((heads, m, KV_LORA), BF16),
        scratch_shapes=[
            pltpu.VMEM((heads * tq, 1), F32),
            pltpu.VMEM((heads * tq, 1), F32),
            pltpu.VMEM((heads * tq, KV_LORA), F32),
        ],
        compiler_params=_params("parallel", "parallel"),
        name="mla_attn_prompt",
    )(q, kv)


def _attn_decode_body(pt_ref, q_ref, kn_ref, *rest, pages, ts):
    del pt_ref
    cache_refs = rest[:pages]
    o_ref, m_ref, l_ref, acc_ref = rest[pages:]
    c = pl.program_id(1)

    @pl.when(c == 0)
    def _():
        _softmax_init(m_ref, l_ref, acc_ref)

    q = q_ref[0]
    kt = jnp.concatenate([r[0, 0].astype(BF16) for r in cache_refs], axis=1)
    _softmax_update(_dot(q, kt) * MLA_SCALE, kt[:KV_LORA, :], m_ref, l_ref, acc_ref,
                    v_transposed=True)

    @pl.when(c == pl.num_programs(1) - 1)
    def _():
        kn = kn_ref[0]
        s = _dot_nt(q, kn) * MLA_SCALE
        tok = lax.broadcasted_iota(jnp.int32, s.shape, 0) % ts
        key = lax.broadcasted_iota(jnp.int32, s.shape, 1)
        s = jnp.where(key <= tok, s, -jnp.inf)
        _softmax_update(s, kn[:, :KV_LORA], m_ref, l_ref, acc_ref)
        o_ref[0] = (acc_ref[...] / l_ref[...]).astype(BF16)


def _attn_decode(q, k_new, cache_t, page_table, layer_slot, ts):
    bs, rows, _ = q.shape
    n_pages = page_table.shape[1]
    pages = DEC_PAGES_PER_STEP
    assert n_pages % pages == 0
    cache = cache_t
    cache_specs = [
        pl.BlockSpec((1, 1, MLA_ROW, PAGE_SIZE),
                     lambda b, c, pt, p=p: (layer_slot, pt[b, c * pages + p], 0, 0))
        for p in range(pages)
    ]
    grid_spec = pltpu.PrefetchScalarGridSpec(
        num_scalar_prefetch=1,
        grid=(bs, n_pages // pages),
        in_specs=[
            pl.BlockSpec((1, rows, MLA_ROW), lambda b, c, pt: (b, 0, 0)),
            pl.BlockSpec((1, k_new.shape[1], MLA_ROW), lambda b, c, pt: (b, 0, 0)),
        ] + cache_specs,
        out_specs=pl.BlockSpec((1, rows, KV_LORA), lambda b, c, pt: (b, 0, 0)),
        scratch_shapes=[
            pltpu.VMEM((rows, 1), F32),
            pltpu.VMEM((rows, 1), F32),
            pltpu.VMEM((rows, KV_LORA), F32),
        ],
    )
    return pl.pallas_call(
        functools.partial(_attn_decode_body, pages=pages, ts=ts),
        grid_spec=grid_spec,
        out_shape=jax.ShapeDtypeStruct((bs, rows, KV_LORA), BF16),
        compiler_params=_params("parallel", "arbitrary"),
        name="mla_attn_decode",
    )(page_table, q, k_new, *([cache] * pages))


def _mla_out_body(o_ref, wuv_ref, wo_ref, h_ref, nw_ref, out_ref):
    vs = [_dot(o_ref[h], wuv_ref[h]).astype(BF16) for h in range(MLA_HEADS)]
    mix = _dot(jnp.concatenate(vs, axis=1), wo_ref[...])
    out_ref[...] = h_ref[...] + _rms(mix, nw_ref[...])


def _mla_out(o_lat, w_uv, w_o, h, nw, tm):
    m = h.shape[0]
    row = lambda i: (i, 0)
    return pl.pallas_call(
        _mla_out_body,
        grid=(m // tm,),
        in_specs=[
            pl.BlockSpec((MLA_HEADS, tm, KV_LORA), lambda i: (0, i, 0)),
            _const_spec(w_uv.shape),
            _const_spec(w_o.shape),
            pl.BlockSpec((tm, D_MODEL), row),
            _const_spec((1, D_MODEL)),
        ],
        out_specs=pl.BlockSpec((tm, D_MODEL), row),
        out_shape=jax.ShapeDtypeStruct((m, D_MODEL), F32),
        compiler_params=_params("parallel"),
        name="mla_out",
    )(o_lat, w_uv, w_o, h, nw)


def _ffn_body(h_ref, nw_in_ref, win_ref, wout_ref, nw_out_ref, out_ref, *, d_ff, chunk):
    x = h_ref[...]
    u = _rms(x, nw_in_ref[...]).astype(BF16)
    acc = jnp.zeros(x.shape, F32)
    for c in range(d_ff // chunk):
        lo = c * chunk
        gate = _dot(u, win_ref[:, lo:lo + chunk])
        up = _dot(u, win_ref[:, d_ff + lo:d_ff + lo + chunk])
        act = (_silu(gate) * up).astype(BF16)
        acc = acc + _dot(act, wout_ref[lo:lo + chunk, :])
    out_ref[...] = x + _rms(acc, nw_out_ref[...])


def _ffn(h, nw_in, w_in, w_out, nw_out, tm):
    m = h.shape[0]
    d_ff = w_out.shape[0]
    assert d_ff % FFN_CHUNK == 0
    row = lambda i: (i, 0)
    return pl.pallas_call(
        functools.partial(_ffn_body, d_ff=d_ff, chunk=FFN_CHUNK),
        grid=(m // tm,),
        in_specs=[
            pl.BlockSpec((tm, D_MODEL), row),
            _const_spec((1, D_MODEL)),
            _const_spec(w_in.shape),
            _const_spec(w_out.shape),
            _const_spec((1, D_MODEL)),
        ],
        out_specs=pl.BlockSpec((tm, D_MODEL), row),
        out_shape=jax.ShapeDtypeStruct((m, D_MODEL), F32),
        compiler_params=_params("parallel"),
        name="ffn",
    )(h, nw_in, w_in, w_out, nw_out)


def _dn_proj_body(*refs, tm, seq_len, has_hist, tail_rows):
    if has_hist:
        (x_ref, nw_ref, wqkv_ref, wz_ref, wba_ref, cw_ref, alog_ref, dtb_ref, hist_ref,
         qkv_ref, z_ref, bg_ref, tail_ref, ext_ref) = refs
    else:
        (x_ref, nw_ref, wqkv_ref, wz_ref, wba_ref, cw_ref, alog_ref, dtb_ref,
         qkv_ref, z_ref, bg_ref, tail_ref, ext_ref) = refs
        hist_ref = None
    i = pl.program_id(0)
    carry = CONV_CARRY_ROWS

    @pl.when(i == 0)
    def _():
        ext_ref[0:carry, :] = jnp.zeros((carry, DN_QKV), F32)

    u = _rms(x_ref[...], nw_ref[...]).astype(BF16)
    raw = _dot(u, wqkv_ref[...])
    ext_ref[carry:carry + tm, :] = raw
    tpos = (i * tm + lax.broadcasted_iota(jnp.int32, (tm, 1), 0)) % seq_len
    cw = cw_ref[...]
    acc = raw * cw[CONV_W - 1:CONV_W, :]
    for k in range(1, CONV_W):
        shifted = ext_ref[carry - k:carry - k + tm, :]
        before = hist_ref[k - 1] if has_hist else 0.0
        acc = acc + jnp.where(tpos >= k, shifted, before) * cw[CONV_W - 1 - k:CONV_W - k, :]
    ext_ref[0:carry, :] = raw[tm - carry:, :]
    tail_ref[0] = raw[tm - tail_rows:, :]
    act = _silu(acc)
    for h in range(DN_HEADS):
        qh = act[:, h * DN_DK:(h + 1) * DN_DK]
        kh = act[:, DN_HK + h * DN_DK:DN_HK + (h + 1) * DN_DK]
        qn = qh * lax.rsqrt(jnp.sum(qh * qh, axis=-1, keepdims=True) + L2_EPS) * (DN_DK ** -0.5)
        kn = kh * lax.rsqrt(jnp.sum(kh * kh, axis=-1, keepdims=True) + L2_EPS)
        qkv_ref[:, h * DN_DK:(h + 1) * DN_DK] = qn
        qkv_ref[:, DN_HK + h * DN_DK:DN_HK + (h + 1) * DN_DK] = kn
    qkv_ref[:, 2 * DN_HK:] = act[:, 2 * DN_HK:]
    z_ref[...] = _dot(u, wz_ref[...])
    ba = _dot(u, wba_ref[...])
    beta = _sigmoid(ba[:, :DN_HEADS])
    g = -jnp.exp(alog_ref[...]) * _softplus(ba[:, DN_HEADS:2 * DN_HEADS] + dtb_ref[...])
    bg_ref[:, :DN_HEADS] = beta
    bg_ref[:, DN_HEADS:] = g


def _dn_project(x, nw, w_qkv, w_z, w_ba, conv_w, a_log, dt_bias, hist, tm, seq_len, tail_rows):
    m = x.shape[0]
    n_tiles = m // tm
    has_hist = hist is not None
    tiles_per_tail = max(seq_len // tm, 1)
    row = lambda i: (i, 0)
    in_specs = [
        pl.BlockSpec((tm, D_MODEL), row),
        _const_spec((1, D_MODEL)),
        _const_spec(w_qkv.shape),
        _const_spec(w_z.shape),
        _const_spec(w_ba.shape),
        _const_spec(conv_w.shape),
        _const_spec((1, DN_HEADS)),
        _const_spec((1, DN_HEADS)),
    ]
    args = [x, nw, w_qkv, w_z, w_ba, conv_w, a_log, dt_bias]
    if has_hist:
        in_specs.append(pl.BlockSpec((CONV_W - 1, tm, DN_QKV), lambda i: (0, i, 0)))
        args.append(hist)
    return pl.pallas_call(
        functools.partial(_dn_proj_body, tm=tm, seq_len=seq_len, has_hist=has_hist,
                          tail_rows=tail_rows),
        grid=(n_tiles,),
        in_specs=in_specs,
        out_specs=[
            pl.BlockSpec((tm, DN_QKV), row),
            pl.BlockSpec((tm, DN_Z), row),
            pl.BlockSpec((tm, 2 * DN_HEADS), row),
            pl.BlockSpec((1, tail_rows, DN_QKV), lambda i: (i // tiles_per_tail, 0, 0)),
        ],
        out_shape=[
            jax.ShapeDtypeStruct((m, DN_QKV), F32),
            jax.ShapeDtypeStruct((m, DN_Z), F32),
            jax.ShapeDtypeStruct((m, 2 * DN_HEADS), F32),
            jax.ShapeDtypeStruct((n_tiles // tiles_per_tail, tail_rows, DN_QKV), F32),
        ],
        scratch_shapes=[pltpu.VMEM((CONV_CARRY_ROWS + tm, DN_QKV), F32)],
        compiler_params=_params("arbitrary"),
        name="dn_project",
    )(*args)


def _stack_heads(ref, rows, g):
    return jnp.concatenate(
        [ref[rows, (g * GDN_GROUP + hh) * DN_DK:(g * GDN_GROUP + hh + 1) * DN_DK]
         for hh in range(GDN_GROUP)], axis=0)


def _gdn_local_body(q_ref, k_ref, v_ref, bg_ref, bgt_ref,
                    u_ref, w_ref, qd_ref, kdt_ref, qk_ref, egl_ref, *, chunk, chunks_per_step):
    n_heads = DN_HEADS
    stack = GDN_GROUP * chunk
    ri = lax.broadcasted_iota(jnp.int32, (stack, stack), 0)
    ci = lax.broadcasted_iota(jnp.int32, (stack, stack), 1)
    same_head = (ri // chunk) == (ci // chunk)
    causal = same_head & (ri >= ci)
    strict = same_head & (ri > ci)
    r1 = lax.broadcasted_iota(jnp.int32, (chunk, chunk), 0)
    c1 = lax.broadcasted_iota(jnp.int32, (chunk, chunk), 1)
    tril = jnp.where(r1 >= c1, 1.0, 0.0).astype(BF16)
    triu = jnp.where(r1 <= c1, 1.0, 0.0).astype(BF16)
    n_double = int(math.log2(chunk)) - 1
    for cp in range(chunks_per_step):
        rows = slice(cp * chunk, (cp + 1) * chunk)
        bg = bg_ref[rows, :]
        bgt = bgt_ref[cp]
        gc_col = sum(_dot(tril, part) for part in _split3(bg))
        gc_row = sum(_dot(part, triu) for part in _split3(bgt))
        egl_ref[cp] = jnp.broadcast_to(jnp.exp(gc_row[n_heads:, chunk - 1:chunk]),
                                       (n_heads, LANES))
        for g in range(n_heads // GDN_GROUP):
            heads = [g * GDN_GROUP + hh for hh in range(GDN_GROUP)]
            q = _stack_heads(q_ref, rows, g)
            k = _stack_heads(k_ref, rows, g)
            v = _stack_heads(v_ref, rows, g)
            beta = jnp.concatenate([bg[:, h:h + 1] for h in heads], axis=0)
            gcc = jnp.concatenate([gc_col[:, n_heads + h:n_heads + h + 1] for h in heads], axis=0)
            gcr = jnp.concatenate([gc_row[n_heads + h:n_heads + h + 1, :] for h in heads], axis=1)
            g_last = jnp.concatenate(
                [jnp.broadcast_to(gc_col[chunk - 1:chunk, n_heads + h:n_heads + h + 1], (chunk, 1))
                 for h in heads], axis=0)
            decay = jnp.where(causal, jnp.exp(gcc - gcr), 0.0)
            eg = jnp.exp(gcc)
            kb = k * beta
            vb = v * beta
            k16 = k.astype(BF16)
            lower = jnp.where(strict, _dot_nt(kb.astype(BF16), k16) * decay, 0.0)
            power = lower
            off = -lower
            for _ in range(n_double):
                p16 = power.astype(BF16)
                power = _dot(p16, p16)
                off = off + power + _dot(off.astype(BF16), power.astype(BF16))
            rhs = jnp.concatenate([vb, kb * eg], axis=1)
            sol = rhs + _dot(off.astype(BF16), rhs.astype(BF16))
            qk = jnp.where(causal, _dot_nt(q.astype(BF16), k16) * decay, 0.0)
            qk_packed = sum(qk[hh * chunk:(hh + 1) * chunk, :] for hh in range(GDN_GROUP))
            k_dec = k * jnp.exp(g_last - gcc)
            q_dec = (q * eg).astype(BF16)
            w16 = sol[:, DN_DV:].astype(BF16)
            for hh, h in enumerate(heads):
                blk = slice(hh * chunk, (hh + 1) * chunk)
                sl = slice(h * DN_DK, (h + 1) * DN_DK)
                u_ref[rows, sl] = sol[blk, :DN_DV]
                w_ref[rows, sl] = w16[blk, :]
                qd_ref[rows, sl] = q_dec[blk, :]
            kdt_ref[cp, :, g * stack:(g + 1) * stack] = k_dec.T.astype(BF16)
            qk_ref[rows, g * stack:(g + 1) * stack] = qk_packed.astype(BF16)


def _gdn_scan_body(u_ref, w_ref, qd_ref, kdt_ref, qk_ref, egl_ref, s0_ref, o_ref, s_ref, *, chunk):
    c = pl.program_id(1)

    @pl.when(c == 0)
    def _():
        s_ref[...] = s0_ref[...]

    stack = GDN_GROUP * chunk
    rows = slice(0, chunk)
    lhs_mask = ((lax.broadcasted_iota(jnp.int32, (2 * stack, GDN_GROUP * DN_DK), 0) // chunk)
                % GDN_GROUP
                == lax.broadcasted_iota(jnp.int32, (2 * stack, GDN_GROUP * DN_DK), 1) // DN_DK)
    qk_mask = (lax.broadcasted_iota(jnp.int32, (stack, stack), 0) // chunk
               == lax.broadcasted_iota(jnp.int32, (stack, stack), 1) // chunk)
    kd_mask = (lax.broadcasted_iota(jnp.int32, (GDN_GROUP * DN_DK, stack), 0) // DN_DK
               == lax.broadcasted_iota(jnp.int32, (GDN_GROUP * DN_DK, stack), 1) // chunk)
    zero16 = jnp.zeros((), BF16)
    for g in range(DN_HEADS // GDN_GROUP):
        heads = [g * GDN_GROUP + hh for hh in range(GDN_GROUP)]
        s = s_ref[0, g * GDN_GROUP:(g + 1) * GDN_GROUP].reshape(GDN_GROUP * DN_DK, DN_DV)
        s16 = s.astype(BF16)
        lhs = jnp.concatenate([_stack_heads(w_ref, rows, g), _stack_heads(qd_ref, rows, g)], axis=0)
        lhs_bd = jnp.where(lhs_mask, jnp.concatenate([lhs] * GDN_GROUP, axis=1), zero16)
        res = _dot(lhs_bd, s16)
        v_new = _stack_heads(u_ref, rows, g) - res[:stack]
        v16 = v_new.astype(BF16)
        qk = qk_ref[:, g * stack:(g + 1) * stack]
        qk_bd = jnp.where(qk_mask, jnp.concatenate([qk] * GDN_GROUP, axis=0), zero16)
        o = res[stack:] + _dot(qk_bd, v16)
        for hh, h in enumerate(heads):
            o_ref[:, h * DN_DV:(h + 1) * DN_DV] = o[hh * chunk:(hh + 1) * chunk, :]
        kdt = kdt_ref[0, :, g * stack:(g + 1) * stack]
        kdt_bd = jnp.where(kd_mask, jnp.concatenate([kdt] * GDN_GROUP, axis=0), zero16)
        gate = jnp.concatenate(
            [jnp.broadcast_to(egl_ref[0, h:h + 1, :], (DN_DK, DN_DV)) for h in heads], axis=0)
        s_new = s * gate + _dot(kdt_bd, v16)
        s_ref[0, g * GDN_GROUP:(g + 1) * GDN_GROUP] = s_new.reshape(GDN_GROUP, DN_DK, DN_DV)


def _gdn(qkv, bg, s0, n_seq, seq_len):
    chunk = DN_CHUNK
    m = qkv.shape[0]
    n_chunks = m // chunk
    nc = seq_len // chunk
    cps = GDN_CHUNKS_PER_STEP
    assert n_chunks % cps == 0
    bgt = bg.reshape(n_chunks, chunk, 2 * DN_HEADS).transpose(0, 2, 1)
    half = DN_HEADS * chunk
    row = lambda i: (i, 0)
    blk = lambda col: pl.BlockSpec((cps * chunk, DN_HK), lambda i, col=col: (i, col))
    u, w16, qd16, kdt16, qk16, egl = pl.pallas_call(
        functools.partial(_gdn_local_body, chunk=chunk, chunks_per_step=cps),
        grid=(n_chunks // cps,),
        in_specs=[
            blk(0), blk(1), blk(2),
            pl.BlockSpec((cps * chunk, 2 * DN_HEADS), row),
            pl.BlockSpec((cps, 2 * DN_HEADS, chunk), lambda i: (i, 0, 0)),
        ],
        out_specs=[
            pl.BlockSpec((cps * chunk, DN_Z), row),
            pl.BlockSpec((cps * chunk, DN_HK), row),
            pl.BlockSpec((cps * chunk, DN_HK), row),
            pl.BlockSpec((cps, DN_DK, half), lambda i: (i, 0, 0)),
            pl.BlockSpec((cps * chunk, half), row),
            pl.BlockSpec((cps, DN_HEADS, LANES), lambda i: (i, 0, 0)),
        ],
        out_shape=[
            jax.ShapeDtypeStruct((m, DN_Z), F32),
            jax.ShapeDtypeStruct((m, DN_HK), BF16),
            jax.ShapeDtypeStruct((m, DN_HK), BF16),
            jax.ShapeDtypeStruct((n_chunks, DN_DK, half), BF16),
            jax.ShapeDtypeStruct((m, half), BF16),
            jax.ShapeDtypeStruct((n_chunks, DN_HEADS, LANES), F32),
        ],
        compiler_params=_params("parallel"),
        name="gdn_local",
    )(qkv, qkv, qkv, bg, bgt)

    chunk_row = lambda b, c: (b * nc + c, 0)
    chunk_lead = lambda b, c: (b * nc + c, 0, 0)
    state_spec = pl.BlockSpec((1, DN_HEADS, DN_DK, DN_DV), lambda b, c: (b, 0, 0, 0))
    return pl.pallas_call(
        functools.partial(_gdn_scan_body, chunk=chunk),
        grid=(n_seq, nc),
        in_specs=[
            pl.BlockSpec((chunk, DN_Z), chunk_row),
            pl.BlockSpec((chunk, DN_HK), chunk_row),
            pl.BlockSpec((chunk, DN_HK), chunk_row),
            pl.BlockSpec((1, DN_DK, half), chunk_lead),
            pl.BlockSpec((chunk, half), chunk_row),
            pl.BlockSpec((1, DN_HEADS, LANES), chunk_lead),
            state_spec,
        ],
        out_specs=[pl.BlockSpec((chunk, DN_Z), chunk_row), state_spec],
        out_shape=[
            jax.ShapeDtypeStruct((m, DN_Z), F32),
            jax.ShapeDtypeStruct(s0.shape, F32),
        ],
        compiler_params=_params("parallel", "arbitrary"),
        name="gdn_scan",
    )(u, w16, qd16, kdt16, qk16, egl, s0)


def _dn_out_body(o_ref, z_ref, gout_ref, wo_ref, h_ref, nw_ref, out_ref):
    gout = gout_ref[...]
    ys = []
    for h in range(DN_HEADS):
        sl = slice(h * DN_DV, (h + 1) * DN_DV)
        ys.append((_rms(o_ref[:, sl], gout) * _silu(z_ref[:, sl])).astype(BF16))
    mix = _dot(jnp.concatenate(ys, axis=1), wo_ref[...])
    out_ref[...] = h_ref[...] + _rms(mix, nw_ref[...])


def _dn_out(o, z, g_out, w_o, h, nw, tm):
    m = h.shape[0]
    row = lambda i: (i, 0)
    return pl.pallas_call(
        _dn_out_body,
        grid=(m // tm,),
        in_specs=[
            pl.BlockSpec((tm, DN_Z), row),
            pl.BlockSpec((tm, DN_Z), row),
            _const_spec((1, DN_DV)),
            _const_spec(w_o.shape),
            pl.BlockSpec((tm, D_MODEL), row),
            _const_spec((1, D_MODEL)),
        ],
        out_specs=pl.BlockSpec((tm, D_MODEL), row),
        out_shape=jax.ShapeDtypeStruct((m, D_MODEL), F32),
        compiler_params=_params("parallel"),
        name="dn_out",
    )(o, z, g_out, w_o, h, nw)


def _mla_layer(hp, hs, cache_mla, page_table, slot, nw, w_in, g_q, g_kv, w_uq, w_uk, w_uv, w_o,
               bp, tp, bs, ts):
    past = page_table.shape[1] * PAGE_SIZE
    w_in16 = w_in.astype(BF16)
    w_uq16 = w_uq.astype(BF16)
    w_ukt16 = jnp.swapaxes(w_uk, 1, 2).astype(BF16)
    w_uv16 = w_uv.astype(BF16)
    w_o16 = w_o.astype(BF16)
    g_q = g_q.reshape(1, Q_LORA)
    g_kv = g_kv.reshape(1, KV_LORA)
    nw0 = nw[0].reshape(1, D_MODEL)
    nw1 = nw[1].reshape(1, D_MODEL)

    cs_p, sn_p = _rope_tables(jnp.tile(jnp.arange(tp), bp))
    cs_s, sn_s = _rope_tables(past + jnp.tile(jnp.arange(ts), bs))
    ms = bs * ts
    rows_p, kv_p, q_p = _mla_project(hp, nw0, w_in16, g_q, g_kv, w_uq16, w_ukt16, cs_p, sn_p,
                                     TM_TOKENS)
    rows_s, kv_s, q_s = _mla_project(hs, nw0, w_in16, g_q, g_kv, w_uq16, w_ukt16, cs_s, sn_s, ms)

    o_p = _attn_prompt(q_p, kv_p, bp, tp, TQ_ATTN)

    q_sb = q_s.reshape(MLA_HEADS, bs, ts, MLA_ROW).transpose(1, 0, 2, 3).reshape(
        bs, MLA_HEADS * ts, MLA_ROW)
    new_rows = 2 * SUBLANES
    k_new = jnp.pad(kv_s.reshape(bs, ts, MLA_ROW), ((0, 0), (0, new_rows - ts), (0, 0)))
    o_sb = _attn_decode(q_sb, k_new, jnp.swapaxes(cache_mla, 2, 3), page_table, slot, ts)
    o_s = o_sb.reshape(bs, MLA_HEADS, ts, KV_LORA).transpose(1, 0, 2, 3).reshape(
        MLA_HEADS, ms, KV_LORA)

    hp = _mla_out(o_p, w_uv16, w_o16, hp, nw1, TM_TOKENS)
    hs = _mla_out(o_s, w_uv16, w_o16, hs, nw1, ms)
    return hp, hs, rows_p.reshape(bp, tp, MLA_ROW), rows_s.reshape(bs, ts, MLA_ROW)


def _dn_layer(hp, hs, s0_s, conv0_s, nw, w_in, conv_w, a_log, dt_bias, g_out, w_o, bp, tp, bs, ts):
    w_qkv16 = w_in[:, :DN_QKV].astype(BF16)
    w_z16 = w_in[:, DN_QKV:DN_QKV + DN_Z].astype(BF16)
    w_ba16 = jnp.pad(w_in[:, DN_QKV + DN_Z:], ((0, 0), (0, LANES - 2 * DN_HEADS))).astype(BF16)
    w_o16 = w_o.astype(BF16)
    a_log = a_log.reshape(1, DN_HEADS).astype(F32)
    dt_bias = dt_bias.reshape(1, DN_HEADS).astype(F32)
    g_out = g_out.reshape(1, DN_DV)
    nw0 = nw[0].reshape(1, D_MODEL)
    nw1 = nw[1].reshape(1, D_MODEL)
    hist_rows = CONV_W - 1
    ms = bs * ts

    qkv_p, z_p, bg_p, tail_p = _dn_project(hp, nw0, w_qkv16, w_z16, w_ba16, conv_w, a_log, dt_bias,
                                           None, TM_DN_PROJ, tp, SUBLANES)
    conv_p = tail_p[:, SUBLANES - hist_rows:, :]
    s0_p = jnp.zeros((bp, DN_HEADS, DN_DK, DN_DV), F32)
    o_p, s_p = _gdn(qkv_p, bg_p, s0_p, bp, tp)
    hp = _dn_out(o_p, z_p, g_out, w_o16, hp, nw1, TM_TOKENS)

    tok = jnp.arange(ts)
    hist = jnp.stack([
        conv0_s[:, jnp.clip(hist_rows - k + tok, 0, hist_rows - 1), :].reshape(ms, DN_QKV)
        for k in range(1, CONV_W)
    ]).astype(F32)
    qkv_s, z_s, bg_s, tail_s = _dn_project(hs, nw0, w_qkv16, w_z16, w_ba16, conv_w, a_log, dt_bias,
                                           hist, ms, ts, ms)
    raw_s = tail_s.reshape(bs, ts, DN_QKV)
    conv_s = jnp.concatenate([conv0_s.astype(F32), raw_s], axis=1)[:, ts:, :]
    cs = DN_CHUNK
    pad = lambda a: jnp.pad(a.reshape(bs, ts, -1), ((0, 0), (0, cs - ts), (0, 0))).reshape(
        bs * cs, -1)
    o_s_pad, s_s = _gdn(pad(qkv_s), pad(bg_s), s0_s.astype(F32), bs, cs)
    o_s = o_s_pad.reshape(bs, cs, DN_Z)[:, :ts, :].reshape(ms, DN_Z)
    hs = _dn_out(o_s, z_s, g_out, w_o16, hs, nw1, ms)
    return hp, hs, s_p, s_s, conv_p, conv_s


def kernel(x_prompt, x_sample, cache_mla, state_dn, state_dn_conv, page_table, norm_w, mla_w_in,
           mla_g_q, mla_g_kv, mla_w_uq, mla_w_uk, mla_w_uv, mla_w_o, dn_w_in, dn_conv_w, dn_a_log,
           dn_dt_bias, dn_g_out, dn_w_o, ffn_w_in, ffn_w_out):
    bp, tp, _ = x_prompt.shape
    bs, ts, _ = x_sample.shape
    depth = norm_w.shape[0]
    hp = x_prompt.reshape(bp * tp, D_MODEL)
    hs = x_sample.reshape(bs * ts, D_MODEL)
    rows_p_l, rows_s_l, sp_l, ss_l, cp_l, cs_l = [], [], [], [], [], []
    for layer in range(depth):
        j = layer // N_MIXERS
        nw = norm_w[layer]
        if layer % N_MIXERS == 0:
            hp, hs, rows_p, rows_s = _mla_layer(
                hp, hs, cache_mla, page_table, j, nw, mla_w_in[j], mla_g_q[j], mla_g_kv[j],
                mla_w_uq[j], mla_w_uk[j], mla_w_uv[j], mla_w_o[j], bp, tp, bs, ts)
            rows_p_l.append(rows_p)
            rows_s_l.append(rows_s)
        else:
            hp, hs, s_p, s_s, c_p, c_s = _dn_layer(
                hp, hs, state_dn[j], state_dn_conv[j], nw, dn_w_in[j], dn_conv_w[j], dn_a_log[j],
                dn_dt_bias[j], dn_g_out[j], dn_w_o[j], bp, tp, bs, ts)
            sp_l.append(s_p.astype(state_dn.dtype))
            ss_l.append(s_s.astype(state_dn.dtype))
            cp_l.append(c_p.astype(state_dn_conv.dtype))
            cs_l.append(c_s.astype(state_dn_conv.dtype))
        w_in16 = ffn_w_in[layer].astype(BF16)
        w_out16 = ffn_w_out[layer].astype(BF16)
        nw2 = nw[2].reshape(1, D_MODEL)
        nw3 = nw[3].reshape(1, D_MODEL)
        hp = _ffn(hp, nw2, w_in16, w_out16, nw3, TM_TOKENS)
        hs = _ffn(hs, nw2, w_in16, w_out16, nw3, bs * ts)
    return (hp.reshape(bp, tp, D_MODEL), hs.reshape(bs, ts, D_MODEL),
            jnp.stack(rows_p_l), jnp.stack(rows_s_l), jnp.stack(sp_l), jnp.stack(ss_l),
            jnp.stack(cp_l), jnp.stack(cs_l))
```

```python
import functools
import math

import jax
import jax.numpy as jnp
from jax import lax
from jax.experimental import pallas as pl
from jax.experimental.pallas import tpu as pltpu

F32 = jnp.float32
BF16 = jnp.bfloat16

D_MODEL = 1024
PAGE_SIZE = 128
N_MIXERS = 2

MLA_HEADS = 8
QK_NOPE = 128
QK_ROPE = 64
V_HEAD = 128
KV_LORA = 256
Q_LORA = 384
MLA_ROW = KV_LORA + QK_ROPE
MLA_SCALE = (QK_NOPE + QK_ROPE) ** -0.5
ROPE_THETA = 10000.0

DN_HEADS = 8
DN_DK = 128
DN_DV = 128
CONV_W = 4
DN_HK = DN_HEADS * DN_DK
DN_QKV = DN_HEADS * (2 * DN_DK + DN_DV)
DN_Z = DN_HEADS * DN_DV

RMS_EPS = 1e-6
L2_EPS = 1e-6

LANES = 128
SUBLANES = 8
VMEM_LIMIT_BYTES = 56 * 1024 * 1024

TM_TOKENS = 512
TM_DN_PROJ = 256
TQ_ATTN = 256
DN_CHUNK = 64
GDN_GROUP = 4
GDN_CHUNKS_PER_STEP = 4
GDN_SEQS_PER_STEP = 2
DEC_PAGES_PER_STEP = 64
FFN_CHUNK = 256
CONV_CARRY_ROWS = SUBLANES


def _params(*sem):
    return pltpu.CompilerParams(dimension_semantics=sem, vmem_limit_bytes=VMEM_LIMIT_BYTES)


def _const_spec(shape):
    nd = len(shape)
    return pl.BlockSpec(shape, lambda *_: (0,) * nd)


def _dot(a, b):
    return jnp.dot(a, b, preferred_element_type=F32)


def _dot_nt(a, b):
    return lax.dot_general(a, b, (((1,), (1,)), ((), ())), preferred_element_type=F32)


def _dot_tn(a, b):
    return lax.dot_general(a, b, (((0,), (0,)), ((), ())), preferred_element_type=F32)


def _split2(x):
    hi = x.astype(BF16)
    lo = (x - hi.astype(F32)).astype(BF16)
    return hi, lo


def _split3(x):
    hi = x.astype(BF16)
    r = x - hi.astype(F32)
    mid = r.astype(BF16)
    lo = (r - mid.astype(F32)).astype(BF16)
    return hi, mid, lo


def _dot_hl(a, b):
    ah, al = _split2(a)
    bh, bl = _split2(b)
    return _dot(ah, bh) + (_dot(ah, bl) + _dot(al, bh))


def _rms(x, w):
    return x * lax.rsqrt(jnp.mean(x * x, axis=-1, keepdims=True) + RMS_EPS) * w


def _sigmoid(x):
    return 1.0 / (1.0 + jnp.exp(-x))


def _silu(x):
    return x * _sigmoid(x)


def _softplus(x):
    return jnp.maximum(x, 0.0) + jnp.log1p(jnp.exp(-jnp.abs(x)))


def _rope(x, cs, sn):
    half = x.shape[-1] // 2
    swapped = jnp.concatenate([x[:, half:], x[:, :half]], axis=1)
    return x * cs + swapped * sn


def _rope_tables(pos):
    half = QK_ROPE // 2
    freq = ROPE_THETA ** (-jnp.arange(half, dtype=F32) / half)
    ang = pos.astype(F32)[:, None] * freq[None, :]
    cos, sin = jnp.cos(ang), jnp.sin(ang)
    return jnp.concatenate([cos, cos], axis=1), jnp.concatenate([-sin, sin], axis=1)


def _mla_proj_body(x_ref, nw_ref, win_ref, gq_ref, gkv_ref, wuq_ref, wukt_ref, cs_ref, sn_ref,
                   rows_ref, kv_ref, q_ref):
    u = _rms(x_ref[...], nw_ref[...]).astype(BF16)
    a = _dot(u, win_ref[...])
    c_q = _rms(a[:, :Q_LORA], gq_ref[...]).astype(BF16)
    c_kv = _rms(a[:, Q_LORA:Q_LORA + KV_LORA], gkv_ref[...])
    cs = cs_ref[...]
    sn = sn_ref[...]
    k_r = _rope(a[:, Q_LORA + KV_LORA:], cs, sn)
    rows_ref[:, :KV_LORA] = c_kv
    rows_ref[:, KV_LORA:] = k_r
    kv_ref[:, :KV_LORA] = c_kv.astype(BF16)
    kv_ref[:, KV_LORA:] = k_r.astype(BF16)
    q = _dot(c_q, wuq_ref[...])
    for h in range(MLA_HEADS):
        base = h * (QK_NOPE + QK_ROPE)
        q_lat = _dot(q[:, base:base + QK_NOPE].astype(BF16), wukt_ref[h])
        q_rope = _rope(q[:, base + QK_NOPE:base + QK_NOPE + QK_ROPE], cs, sn)
        q_ref[h, :, :KV_LORA] = (q_lat * MLA_SCALE).astype(BF16)
        q_ref[h, :, KV_LORA:] = (q_rope * MLA_SCALE).astype(BF16)


def _mla_project(x, nw, w_in, g_q, g_kv, w_uq, w_ukt, cs, sn, tm):
    m = x.shape[0]
    row = lambda i: (i, 0)
    return pl.pallas_call(
        _mla_proj_body,
        grid=(m // tm,),
        in_specs=[
            pl.BlockSpec((tm, D_MODEL), row),
            _const_spec((1, D_MODEL)),
            _const_spec(w_in.shape),
            _const_spec((1, Q_LORA)),
            _const_spec((1, KV_LORA)),
            _const_spec(w_uq.shape),
            _const_spec(w_ukt.shape),
            pl.BlockSpec((tm, QK_ROPE), row),
            pl.BlockSpec((tm, QK_ROPE), row),
        ],
        out_specs=[
            pl.BlockSpec((tm, MLA_ROW), row),
            pl.BlockSpec((tm, MLA_ROW), row),
            pl.BlockSpec((MLA_HEADS, tm, MLA_ROW), lambda i: (0, i, 0)),
        ],
        out_shape=[
            jax.ShapeDtypeStruct((m, MLA_ROW), F32),
            jax.ShapeDtypeStruct((m, MLA_ROW), BF16),
            jax.ShapeDtypeStruct((MLA_HEADS, m, MLA_ROW), BF16),
        ],
        compiler_params=_params("parallel"),
        name="mla_project",
    )(x, nw, w_in, g_q, g_kv, w_uq, w_ukt, cs, sn)


def _softmax_update(s, v, m_ref, l_ref, acc_ref, v_transposed=False):
    m_prev = m_ref[...]
    m_new = jnp.maximum(m_prev, jnp.max(s, axis=1, keepdims=True))
    alpha = jnp.exp(m_prev - m_new)
    p = jnp.exp(s - m_new)
    p16 = p.astype(BF16)
    pv = _dot_nt(p16, v) if v_transposed else _dot(p16, v)
    l_ref[...] = alpha * l_ref[...] + jnp.sum(p, axis=1, keepdims=True)
    acc_ref[...] = alpha * acc_ref[...] + pv
    m_ref[...] = m_new


def _softmax_init(m_ref, l_ref, acc_ref):
    m_ref[...] = jnp.full(m_ref.shape, -jnp.inf, F32)
    l_ref[...] = jnp.zeros(l_ref.shape, F32)
    acc_ref[...] = jnp.zeros(acc_ref.shape, F32)


def _attn_prompt_body(q_ref, kv_ref, kvt_ref, o_ref, m_ref, l_ref, acc_ref, *, tq):
    i = pl.program_id(1)
    heads = q_ref.shape[0]
    cols = heads * tq
    q = q_ref[...].reshape(cols, MLA_ROW)
    _softmax_init(m_ref, l_ref, acc_ref)

    def step(j, masked):
        k = kv_ref[pl.ds(pl.multiple_of(j * tq, tq), tq), :]
        vt = kvt_ref[j]
        st = _dot_nt(k, q)
        if masked:
            key = lax.broadcasted_iota(jnp.int32, (tq, cols), 0)
            tok = lax.broadcasted_iota(jnp.int32, (tq, cols), 1) % tq
            st = jnp.where(key <= tok, st, -jnp.inf)
        m_prev = m_ref[...]
        m_new = jnp.maximum(m_prev, jnp.max(st, axis=0, keepdims=True))
        alpha = jnp.exp(m_prev - m_new)
        p = jnp.exp(st - m_new)
        l_ref[...] = alpha * l_ref[...] + jnp.sum(p, axis=0, keepdims=True)
        acc_ref[...] = alpha * acc_ref[...] + _dot(vt, p.astype(BF16))
        m_ref[...] = m_new

    def body(j, carry):
        step(j, False)
        return carry

    lax.fori_loop(0, i, body, 0)
    step(i, True)
    o = (acc_ref[...] / l_ref[...]).T
    o_ref[...] = o.reshape(heads, tq, KV_LORA).astype(BF16)


def _attn_prompt(q, kv, batch, seq, tq):
    heads, m, _ = q.shape
    nq = seq // tq
    kvt = kv[:, :KV_LORA].reshape(batch * nq, tq, KV_LORA).transpose(0, 2, 1)
    return pl.pallas_call(
        functools.partial(_attn_prompt_body, tq=tq),
        grid=(batch, nq),
        in_specs=[
            pl.BlockSpec((heads, tq, MLA_ROW), lambda b, i: (0, b * nq + i, 0)),
            pl.BlockSpec((seq, MLA_ROW), lambda b, i: (b, 0)),
            pl.BlockSpec((nq, KV_LORA, tq), lambda b, i: (b, 0, 0)),
        ],
        out_specs=pl.BlockSpec((heads, tq, KV_LORA), lambda b, i: (0, b * nq + i, 0)),
        out_shape=jax.ShapeDtypeStruct((heads, m, KV_LORA), BF16),
        scratch_shapes=[
            pltpu.VMEM((1, heads * tq), F32),
            pltpu.VMEM((1, heads * tq), F32),
            pltpu.VMEM((KV_LORA, heads * tq), F32),
        ],
        compiler_params=_params("parallel", "parallel"),
        name="mla_attn_prompt",
    )(q, kv, kvt)


def _attn_decode_body(pt_ref, q_ref, kn_ref, *rest, pages, ts):
    del pt_ref
    cache_refs = rest[:pages]
    o_ref, m_ref, l_ref, acc_ref = rest[pages:]
    c = pl.program_id(1)

    @pl.when(c == 0)
    def _():
        _softmax_init(m_ref, l_ref, acc_ref)

    q = q_ref[0]
    kt = jnp.concatenate([r[0, 0].astype(BF16) for r in cache_refs], axis=1)
    _softmax_update(_dot(q, kt), kt[:KV_LORA, :], m_ref, l_ref, acc_ref, v_transposed=True)

    @pl.when(c == pl.num_programs(1) - 1)
    def _():
        kn = kn_ref[0]
        s = _dot_nt(q, kn)
        tok = lax.broadcasted_iota(jnp.int32, s.shape, 0) % ts
        key = lax.broadcasted_iota(jnp.int32, s.shape, 1)
        s = jnp.where(key <= tok, s, -jnp.inf)
        _softmax_update(s, kn[:, :KV_LORA], m_ref, l_ref, acc_ref)
        o_ref[0] = (acc_ref[...] / l_ref[...]).astype(BF16)


def _attn_decode(q, k_new, cache_t, page_table, layer_slot, ts):
    bs, rows, _ = q.shape
    n_pages = page_table.shape[1]
    pages = DEC_PAGES_PER_STEP
    assert n_pages % pages == 0
    cache = cache_t
    cache_specs = [
        pl.BlockSpec((1, 1, MLA_ROW, PAGE_SIZE),
                     lambda b, c, pt, p=p: (layer_slot, pt[b, c * pages + p], 0, 0))
        for p in range(pages)
    ]
    grid_spec = pltpu.PrefetchScalarGridSpec(
        num_scalar_prefetch=1,
        grid=(bs, n_pages // pages),
        in_specs=[
            pl.BlockSpec((1, rows, MLA_ROW), lambda b, c, pt: (b, 0, 0)),
            pl.BlockSpec((1, k_new.shape[1], MLA_ROW), lambda b, c, pt: (b, 0, 0)),
        ] + cache_specs,
        out_specs=pl.BlockSpec((1, rows, KV_LORA), lambda b, c, pt: (b, 0, 0)),
        scratch_shapes=[
            pltpu.VMEM((rows, 1), F32),
            pltpu.VMEM((rows, 1), F32),
            pltpu.VMEM((rows, KV_LORA), F32),
        ],
    )
    return pl.pallas_call(
        functools.partial(_attn_decode_body, pages=pages, ts=ts),
        grid_spec=grid_spec,
        out_shape=jax.ShapeDtypeStruct((bs, rows, KV_LORA), BF16),
        compiler_params=_params("parallel", "arbitrary"),
        name="mla_attn_decode",
    )(page_table, q, k_new, *([cache] * pages))


def _mla_out_body(o_ref, wuv_ref, wo_ref, h_ref, nw_ref, out_ref):
    vs = [_dot(o_ref[h], wuv_ref[h]).astype(BF16) for h in range(MLA_HEADS)]
    mix = _dot(jnp.concatenate(vs, axis=1), wo_ref[...])
    out_ref[...] = h_ref[...] + _rms(mix, nw_ref[...])


def _mla_out(o_lat, w_uv, w_o, h, nw, tm):
    m = h.shape[0]
    row = lambda i: (i, 0)
    return pl.pallas_call(
        _mla_out_body,
        grid=(m // tm,),
        in_specs=[
            pl.BlockSpec((MLA_HEADS, tm, KV_LORA), lambda i: (0, i, 0)),
            _const_spec(w_uv.shape),
            _const_spec(w_o.shape),
            pl.BlockSpec((tm, D_MODEL), row),
            _const_spec((1, D_MODEL)),
        ],
        out_specs=pl.BlockSpec((tm, D_MODEL), row),
        out_shape=jax.ShapeDtypeStruct((m, D_MODEL), F32),
        compiler_params=_params("parallel"),
        name="mla_out",
    )(o_lat, w_uv, w_o, h, nw)


def _ffn_body(h_ref, nw_in_ref, win_ref, wout_ref, nw_out_ref, out_ref, *, d_ff, chunk):
    x = h_ref[...]
    u = _rms(x, nw_in_ref[...]).astype(BF16)
    acc = jnp.zeros(x.shape, F32)
    for c in range(d_ff // chunk):
        lo = c * chunk
        gate = _dot(u, win_ref[:, lo:lo + chunk])
        up = _dot(u, win_ref[:, d_ff + lo:d_ff + lo + chunk])
        act = (_silu(gate) * up).astype(BF16)
        acc = acc + _dot(act, wout_ref[lo:lo + chunk, :])
    out_ref[...] = x + _rms(acc, nw_out_ref[...])


def _ffn(h, nw_in, w_in, w_out, nw_out, tm):
    m = h.shape[0]
    d_ff = w_out.shape[0]
    assert d_ff % FFN_CHUNK == 0
    row = lambda i: (i, 0)
    return pl.pallas_call(
        functools.partial(_ffn_body, d_ff=d_ff, chunk=FFN_CHUNK),
        grid=(m // tm,),
        in_specs=[
            pl.BlockSpec((tm, D_MODEL), row),
            _const_spec((1, D_MODEL)),
            _const_spec(w_in.shape),
            _const_spec(w_out.shape),
            _const_spec((1, D_MODEL)),
        ],
        out_specs=pl.BlockSpec((tm, D_MODEL), row),
        out_shape=jax.ShapeDtypeStruct((m, D_MODEL), F32),
        compiler_params=_params("parallel"),
        name="ffn",
    )(h, nw_in, w_in, w_out, nw_out)


def _dn_proj_body(*refs, tm, seq_len, has_hist, tail_rows):
    if has_hist:
        (x_ref, nw_ref, wqkv_ref, wz_ref, wba_ref, cw_ref, alog_ref, dtb_ref, hist_ref,
         qkv_ref, z_ref, bg_ref, tail_ref, ext_ref) = refs
    else:
        (x_ref, nw_ref, wqkv_ref, wz_ref, wba_ref, cw_ref, alog_ref, dtb_ref,
         qkv_ref, z_ref, bg_ref, tail_ref, ext_ref) = refs
        hist_ref = None
    i = pl.program_id(0)
    carry = CONV_CARRY_ROWS

    @pl.when((i * tm) % seq_len == 0)
    def _():
        ext_ref[0:carry, :] = jnp.zeros((carry, DN_QKV), F32)

    u = _rms(x_ref[...], nw_ref[...]).astype(BF16)
    raw = _dot(u, wqkv_ref[...])
    ext_ref[carry:carry + tm, :] = raw
    cw = cw_ref[...]
    acc = raw * cw[CONV_W - 1:CONV_W, :]
    for k in range(1, CONV_W):
        shifted = ext_ref[carry - k:carry - k + tm, :]
        if has_hist:
            tpos = lax.broadcasted_iota(jnp.int32, (tm, 1), 0) % seq_len
            shifted = jnp.where(tpos >= k, shifted, hist_ref[k - 1])
        acc = acc + shifted * cw[CONV_W - 1 - k:CONV_W - k, :]
    ext_ref[0:carry, :] = raw[tm - carry:, :]
    tail_ref[0] = raw[tm - tail_rows:, :]
    act = _silu(acc)
    for h in range(DN_HEADS):
        qh = act[:, h * DN_DK:(h + 1) * DN_DK]
        kh = act[:, DN_HK + h * DN_DK:DN_HK + (h + 1) * DN_DK]
        qn = qh * lax.rsqrt(jnp.sum(qh * qh, axis=-1, keepdims=True) + L2_EPS) * (DN_DK ** -0.5)
        kn = kh * lax.rsqrt(jnp.sum(kh * kh, axis=-1, keepdims=True) + L2_EPS)
        qkv_ref[:, h * DN_DK:(h + 1) * DN_DK] = qn
        qkv_ref[:, DN_HK + h * DN_DK:DN_HK + (h + 1) * DN_DK] = kn
    qkv_ref[:, 2 * DN_HK:] = act[:, 2 * DN_HK:]
    z_ref[...] = _dot(u, wz_ref[...])
    ba = _dot(u, wba_ref[...])
    beta = _sigmoid(ba[:, :DN_HEADS])
    g = -jnp.exp(alog_ref[...]) * _softplus(ba[:, DN_HEADS:2 * DN_HEADS] + dtb_ref[...])
    bg_ref[:, :DN_HEADS] = beta
    bg_ref[:, DN_HEADS:] = g


def _dn_project(x, nw, w_qkv, w_z, w_ba, conv_w, a_log, dt_bias, hist, tm, seq_len, tail_rows):
    m = x.shape[0]
    n_tiles = m // tm
    has_hist = hist is not None
    assert (tm % seq_len == 0 and n_tiles == 1) if has_hist else seq_len % tm == 0
    tiles_per_tail = max(seq_len // tm, 1)
    row = lambda i: (i, 0)
    in_specs = [
        pl.BlockSpec((tm, D_MODEL), row),
        _const_spec((1, D_MODEL)),
        _const_spec(w_qkv.shape),
        _const_spec(w_z.shape),
        _const_spec(w_ba.shape),
        _const_spec(conv_w.shape),
        _const_spec((1, DN_HEADS)),
        _const_spec((1, DN_HEADS)),
    ]
    args = [x, nw, w_qkv, w_z, w_ba, conv_w, a_log, dt_bias]
    if has_hist:
        in_specs.append(pl.BlockSpec((CONV_W - 1, tm, DN_QKV), lambda i: (0, i, 0)))
        args.append(hist)
    return pl.pallas_call(
        functools.partial(_dn_proj_body, tm=tm, seq_len=seq_len, has_hist=has_hist,
                          tail_rows=tail_rows),
        grid=(n_tiles,),
        in_specs=in_specs,
        out_specs=[
            pl.BlockSpec((tm, DN_QKV), row),
            pl.BlockSpec((tm, DN_Z), row),
            pl.BlockSpec((tm, 2 * DN_HEADS), row),
            pl.BlockSpec((1, tail_rows, DN_QKV), lambda i: (i // tiles_per_tail, 0, 0)),
        ],
        out_shape=[
            jax.ShapeDtypeStruct((m, DN_QKV), F32),
            jax.ShapeDtypeStruct((m, DN_Z), F32),
            jax.ShapeDtypeStruct((m, 2 * DN_HEADS), F32),
            jax.ShapeDtypeStruct((n_tiles // tiles_per_tail, tail_rows, DN_QKV), F32),
        ],
        scratch_shapes=[pltpu.VMEM((CONV_CARRY_ROWS + tm, DN_QKV), F32)],
        compiler_params=_params("arbitrary"),
        name="dn_project",
    )(*args)


def _stack_heads(load, g):
    return jnp.concatenate(
        [load(slice((g * GDN_GROUP + hh) * DN_DK, (g * GDN_GROUP + hh + 1) * DN_DK))
         for hh in range(GDN_GROUP)], axis=0)


def _gdn_local_body(q_ref, k_ref, v_ref, bg_ref, bgt_ref,
                    u_ref, w_ref, qd_ref, kdt_ref, qk_ref, egl_ref, *, chunk, chunks_per_step):
    n_heads = DN_HEADS
    stack = GDN_GROUP * chunk
    ri = lax.broadcasted_iota(jnp.int32, (stack, stack), 0)
    ci = lax.broadcasted_iota(jnp.int32, (stack, stack), 1)
    same_head = (ri // chunk) == (ci // chunk)
    causal = same_head & (ri >= ci)
    strict = same_head & (ri > ci)
    r1 = lax.broadcasted_iota(jnp.int32, (chunk, chunk), 0)
    c1 = lax.broadcasted_iota(jnp.int32, (chunk, chunk), 1)
    tril = jnp.where(r1 >= c1, 1.0, 0.0).astype(BF16)
    triu = jnp.where(r1 <= c1, 1.0, 0.0).astype(BF16)
    n_double = int(math.log2(chunk)) - 1

    def store_heads(ref, rows, g, stacked):
        for hh in range(GDN_GROUP):
            h = g * GDN_GROUP + hh
            ref[rows, h * DN_DK:(h + 1) * DN_DK] = stacked[hh * chunk:(hh + 1) * chunk, :]

    chains = []
    for cp in range(chunks_per_step):
        rows = slice(cp * chunk, (cp + 1) * chunk)
        bg = bg_ref[rows, :]
        bgt = bgt_ref[cp]
        gc_col = sum(_dot(tril, part) for part in _split3(bg))
        gc_row = sum(_dot(part, triu) for part in _split3(bgt))
        egl_ref[cp] = jnp.broadcast_to(jnp.exp(gc_row[n_heads:, chunk - 1:chunk]),
                                       (n_heads, LANES))
        for g in range(n_heads // GDN_GROUP):
            heads = [g * GDN_GROUP + hh for hh in range(GDN_GROUP)]
            q = _stack_heads(lambda sl: q_ref[rows, sl], g)
            k = _stack_heads(lambda sl: k_ref[rows, sl], g)
            v = _stack_heads(lambda sl: v_ref[rows, sl], g)
            beta = jnp.concatenate([bg[:, h:h + 1] for h in heads], axis=0)
            gcc = jnp.concatenate([gc_col[:, n_heads + h:n_heads + h + 1] for h in heads], axis=0)
            gcr = jnp.concatenate([gc_row[n_heads + h:n_heads + h + 1, :] for h in heads], axis=1)
            g_last = jnp.concatenate(
                [jnp.broadcast_to(gc_col[chunk - 1:chunk, n_heads + h:n_heads + h + 1], (chunk, 1))
                 for h in heads], axis=0)
            eg = jnp.exp(gcc)
            kb = k * beta
            store_heads(qd_ref, rows, g, (q * eg).astype(BF16))
            kdt_ref[cp, :, g * stack:(g + 1) * stack] = (k * jnp.exp(g_last - gcc)).T.astype(BF16)
            chains.append(dict(
                rows=rows, g=g, q16=q.astype(BF16), k16=k.astype(BF16), kb16=kb.astype(BF16),
                decay=jnp.where(causal, jnp.exp(gcc - gcr), 0.0),
                rhs=jnp.concatenate([v * beta, kb * eg], axis=1)))

    for ch in chains:
        lower = jnp.where(strict, _dot_nt(ch["kb16"], ch["k16"]) * ch["decay"], 0.0)
        ch.update(lower=lower, power=lower, off=-lower)
    for ch in chains:
        qk = jnp.where(causal, _dot_nt(ch["q16"], ch["k16"]) * ch["decay"], 0.0)
        qk_packed = sum(qk[hh * chunk:(hh + 1) * chunk, :] for hh in range(GDN_GROUP))
        qk_ref[ch["rows"], ch["g"] * stack:(ch["g"] + 1) * stack] = qk_packed.astype(BF16)
    for _ in range(n_double):
        for ch in chains:
            p16 = ch["power"].astype(BF16)
            ch["power"] = _dot(p16, p16)
        for ch in chains:
            ch["off"] = ch["off"] + ch["power"] + _dot(ch["off"].astype(BF16),
                                                       ch["power"].astype(BF16))
    for ch in chains:
        ch["resid"] = (ch["lower"] + ch["off"]) + _dot_hl(ch["lower"], ch["off"])
    for ch in chains:
        ch["off"] = ch["off"] - ch["resid"] - _dot(ch["off"].astype(BF16),
                                                   ch["resid"].astype(BF16))
    for ch in chains:
        sol = ch["rhs"] + _dot(ch["off"].astype(BF16), ch["rhs"].astype(BF16))
        store_heads(u_ref, ch["rows"], ch["g"], sol[:, :DN_DV])
        store_heads(w_ref, ch["rows"], ch["g"], sol[:, DN_DV:].astype(BF16))


def _gdn_scan_body(u_ref, w_ref, qd_ref, kdt_ref, qk_ref, egl_ref, s0_ref, o_ref, s_ref, *,
                   chunk, seqs):
    c = pl.program_id(1)

    @pl.when(c == 0)
    def _():
        s_ref[...] = s0_ref[...]

    stack = GDN_GROUP * chunk
    lhs_mask = ((lax.broadcasted_iota(jnp.int32, (2 * stack, GDN_GROUP * DN_DK), 0) // chunk)
                % GDN_GROUP
                == lax.broadcasted_iota(jnp.int32, (2 * stack, GDN_GROUP * DN_DK), 1) // DN_DK)
    qk_mask = (lax.broadcasted_iota(jnp.int32, (stack, stack), 0) // chunk
               == lax.broadcasted_iota(jnp.int32, (stack, stack), 1) // chunk)
    kd_mask = (lax.broadcasted_iota(jnp.int32, (GDN_GROUP * DN_DK, stack), 0) // DN_DK
               == lax.broadcasted_iota(jnp.int32, (GDN_GROUP * DN_DK, stack), 1) // chunk)
    zero16 = jnp.zeros((), BF16)
    chains = [dict(sq=sq, g=g) for sq in range(seqs) for g in range(DN_HEADS // GDN_GROUP)]
    for ch in chains:
        sq, g = ch["sq"], ch["g"]
        s = s_ref[sq, g * GDN_GROUP:(g + 1) * GDN_GROUP].reshape(GDN_GROUP * DN_DK, DN_DV)
        lhs = jnp.concatenate([_stack_heads(lambda sl: w_ref[sq, :, sl], g),
                               _stack_heads(lambda sl: qd_ref[sq, :, sl], g)], axis=0)
        lhs_bd = jnp.where(lhs_mask, jnp.concatenate([lhs] * GDN_GROUP, axis=1), zero16)
        ch.update(s=s, res=_dot(lhs_bd, s.astype(BF16)))
    for ch in chains:
        sq, g = ch["sq"], ch["g"]
        v_new = _stack_heads(lambda sl: u_ref[sq, :, sl], g) - ch["res"][:stack]
        ch["v16"] = v_new.astype(BF16)
    for ch in chains:
        sq, g = ch["sq"], ch["g"]
        qk = qk_ref[sq, :, g * stack:(g + 1) * stack]
        qk_bd = jnp.where(qk_mask, jnp.concatenate([qk] * GDN_GROUP, axis=0), zero16)
        o = ch["res"][stack:] + _dot(qk_bd, ch["v16"])
        for hh in range(GDN_GROUP):
            h = g * GDN_GROUP + hh
            o_ref[sq, :, h * DN_DV:(h + 1) * DN_DV] = o[hh * chunk:(hh + 1) * chunk, :]
    for ch in chains:
        sq, g = ch["sq"], ch["g"]
        kdt = kdt_ref[sq, 0, :, g * stack:(g + 1) * stack]
        kdt_bd = jnp.where(kd_mask, jnp.concatenate([kdt] * GDN_GROUP, axis=0), zero16)
        gate = jnp.concatenate(
            [jnp.broadcast_to(egl_ref[sq, 0, g * GDN_GROUP + hh:g * GDN_GROUP + hh + 1, :],
                              (DN_DK, DN_DV)) for hh in range(GDN_GROUP)], axis=0)
        s_new = ch["s"] * gate + _dot(kdt_bd, ch["v16"])
        s_ref[sq, g * GDN_GROUP:(g + 1) * GDN_GROUP] = s_new.reshape(GDN_GROUP, DN_DK, DN_DV)


def _gdn(qkv, bg, s0, n_seq, seq_len):
    chunk = DN_CHUNK
    m = qkv.shape[0]
    n_chunks = m // chunk
    nc = seq_len // chunk
    cps = GDN_CHUNKS_PER_STEP
    sps = GDN_SEQS_PER_STEP
    assert n_chunks % cps == 0 and n_seq % sps == 0
    bgt = bg.reshape(n_chunks, chunk, 2 * DN_HEADS).transpose(0, 2, 1)
    half = DN_HEADS * chunk
    row = lambda i: (i, 0)
    blk = lambda col: pl.BlockSpec((cps * chunk, DN_HK), lambda i, col=col: (i, col))
    u, w16, qd16, kdt16, qk16, egl = pl.pallas_call(
        functools.partial(_gdn_local_body, chunk=chunk, chunks_per_step=cps),
        grid=(n_chunks // cps,),
        in_specs=[
            blk(0), blk(1), blk(2),
            pl.BlockSpec((cps * chunk, 2 * DN_HEADS), row),
            pl.BlockSpec((cps, 2 * DN_HEADS, chunk), lambda i: (i, 0, 0)),
        ],
        out_specs=[
            pl.BlockSpec((cps * chunk, DN_Z), row),
            pl.BlockSpec((cps * chunk, DN_HK), row),
            pl.BlockSpec((cps * chunk, DN_HK), row),
            pl.BlockSpec((cps, DN_DK, half), lambda i: (i, 0, 0)),
            pl.BlockSpec((cps * chunk, half), row),
            pl.BlockSpec((cps, DN_HEADS, LANES), lambda i: (i, 0, 0)),
        ],
        out_shape=[
            jax.ShapeDtypeStruct((m, DN_Z), F32),
            jax.ShapeDtypeStruct((m, DN_HK), BF16),
            jax.ShapeDtypeStruct((m, DN_HK), BF16),
            jax.ShapeDtypeStruct((n_chunks, DN_DK, half), BF16),
            jax.ShapeDtypeStruct((m, half), BF16),
            jax.ShapeDtypeStruct((n_chunks, DN_HEADS, LANES), F32),
        ],
        compiler_params=_params("parallel"),
        name="gdn_local",
    )(qkv, qkv, qkv, bg, bgt)

    per_seq = lambda a: a.reshape((n_seq, nc if a.ndim == 3 else seq_len) + a.shape[1:])
    tok_spec = lambda width: pl.BlockSpec((sps, chunk, width), lambda b, c: (b, c, 0))
    lead_spec = lambda d1, d2: pl.BlockSpec((sps, 1, d1, d2), lambda b, c: (b, c, 0, 0))
    state_spec = pl.BlockSpec((sps, DN_HEADS, DN_DK, DN_DV), lambda b, c: (b, 0, 0, 0))
    o, s = pl.pallas_call(
        functools.partial(_gdn_scan_body, chunk=chunk, seqs=sps),
        grid=(n_seq // sps, nc),
        in_specs=[
            tok_spec(DN_Z),
            tok_spec(DN_HK),
            tok_spec(DN_HK),
            lead_spec(DN_DK, half),
            tok_spec(half),
            lead_spec(DN_HEADS, LANES),
            state_spec,
        ],
        out_specs=[tok_spec(DN_Z), state_spec],
        out_shape=[
            jax.ShapeDtypeStruct((n_seq, seq_len, DN_Z), F32),
            jax.ShapeDtypeStruct(s0.shape, F32),
        ],
        compiler_params=_params("parallel", "arbitrary"),
        name="gdn_scan",
    )(per_seq(u), per_seq(w16), per_seq(qd16), per_seq(kdt16), per_seq(qk16), per_seq(egl), s0)
    return o.reshape(m, DN_Z), s


def _dn_out_body(o_ref, z_ref, gout_ref, wo_ref, h_ref, nw_ref, out_ref):
    gout = gout_ref[...]
    ys = []
    for h in range(DN_HEADS):
        sl = slice(h * DN_DV, (h + 1) * DN_DV)
        ys.append((_rms(o_ref[:, sl], gout) * _silu(z_ref[:, sl])).astype(BF16))
    mix = _dot(jnp.concatenate(ys, axis=1), wo_ref[...])
    out_ref[...] = h_ref[...] + _rms(mix, nw_ref[...])


def _dn_out(o, z, g_out, w_o, h, nw, tm):
    m = h.shape[0]
    row = lambda i: (i, 0)
    return pl.pallas_call(
        _dn_out_body,
        grid=(m // tm,),
        in_specs=[
            pl.BlockSpec((tm, DN_Z), row),
            pl.BlockSpec((tm, DN_Z), row),
            _const_spec((1, DN_DV)),
            _const_spec(w_o.shape),
            pl.BlockSpec((tm, D_MODEL), row),
            _const_spec((1, D_MODEL)),
        ],
        out_specs=pl.BlockSpec((tm, D_MODEL), row),
        out_shape=jax.ShapeDtypeStruct((m, D_MODEL), F32),
        compiler_params=_params("parallel"),
        name="dn_out",
    )(o, z, g_out, w_o, h, nw)


def _mla_layer(hp, hs, cache_mla, page_table, slot, nw, w_in, g_q, g_kv, w_uq, w_uk, w_uv, w_o,
               bp, tp, bs, ts):
    past = page_table.shape[1] * PAGE_SIZE
    w_in16 = w_in.astype(BF16)
    w_uq16 = w_uq.astype(BF16)
    w_ukt16 = jnp.swapaxes(w_uk, 1, 2).astype(BF16)
    w_uv16 = w_uv.astype(BF16)
    w_o16 = w_o.astype(BF16)
    g_q = g_q.reshape(1, Q_LORA)
    g_kv = g_kv.reshape(1, KV_LORA)
    nw0 = nw[0].reshape(1, D_MODEL)
    nw1 = nw[1].reshape(1, D_MODEL)

    cs_p, sn_p = _rope_tables(jnp.tile(jnp.arange(tp), bp))
    cs_s, sn_s = _rope_tables(past + jnp.tile(jnp.arange(ts), bs))
    ms = bs * ts
    rows_p, kv_p, q_p = _mla_project(hp, nw0, w_in16, g_q, g_kv, w_uq16, w_ukt16, cs_p, sn_p,
                                     TM_TOKENS)
    rows_s, kv_s, q_s = _mla_project(hs, nw0, w_in16, g_q, g_kv, w_uq16, w_ukt16, cs_s, sn_s, ms)

    o_p = _attn_prompt(q_p, kv_p, bp, tp, TQ_ATTN)

    q_sb = q_s.reshape(MLA_HEADS, bs, ts, MLA_ROW).transpose(1, 0, 2, 3).reshape(
        bs, MLA_HEADS * ts, MLA_ROW)
    new_rows = 2 * SUBLANES
    k_new = jnp.pad(kv_s.reshape(bs, ts, MLA_ROW), ((0, 0), (0, new_rows - ts), (0, 0)))
    o_sb = _attn_decode(q_sb, k_new, jnp.swapaxes(cache_mla, 2, 3), page_table, slot, ts)
    o_s = o_sb.reshape(bs, MLA_HEADS, ts, KV_LORA).transpose(1, 0, 2, 3).reshape(
        MLA_HEADS, ms, KV_LORA)

    hp = _mla_out(o_p, w_uv16, w_o16, hp, nw1, TM_TOKENS)
    hs = _mla_out(o_s, w_uv16, w_o16, hs, nw1, ms)
    return hp, hs, rows_p.reshape(bp, tp, MLA_ROW), rows_s.reshape(bs, ts, MLA_ROW)


def _dn_layer(hp, hs, s0_s, conv0_s, nw, w_in, conv_w, a_log, dt_bias, g_out, w_o, bp, tp, bs, ts):
    w_qkv16 = w_in[:, :DN_QKV].astype(BF16)
    w_z16 = w_in[:, DN_QKV:DN_QKV + DN_Z].astype(BF16)
    w_ba16 = jnp.pad(w_in[:, DN_QKV + DN_Z:], ((0, 0), (0, LANES - 2 * DN_HEADS))).astype(BF16)
    w_o16 = w_o.astype(BF16)
    a_log = a_log.reshape(1, DN_HEADS).astype(F32)
    dt_bias = dt_bias.reshape(1, DN_HEADS).astype(F32)
    g_out = g_out.reshape(1, DN_DV)
    nw0 = nw[0].reshape(1, D_MODEL)
    nw1 = nw[1].reshape(1, D_MODEL)
    hist_rows = CONV_W - 1
    ms = bs * ts

    qkv_p, z_p, bg_p, tail_p = _dn_project(hp, nw0, w_qkv16, w_z16, w_ba16, conv_w, a_log, dt_bias,
                                           None, TM_DN_PROJ, tp, SUBLANES)
    conv_p = tail_p[:, SUBLANES - hist_rows:, :]
    s0_p = jnp.zeros((bp, DN_HEADS, DN_DK, DN_DV), F32)
    o_p, s_p = _gdn(qkv_p, bg_p, s0_p, bp, tp)
    hp = _dn_out(o_p, z_p, g_out, w_o16, hp, nw1, TM_TOKENS)

    tok = jnp.arange(ts)
    hist = jnp.stack([
        conv0_s[:, jnp.clip(hist_rows - k + tok, 0, hist_rows - 1), :].reshape(ms, DN_QKV)
        for k in range(1, CONV_W)
    ]).astype(F32)
    qkv_s, z_s, bg_s, tail_s = _dn_project(hs, nw0, w_qkv16, w_z16, w_ba16, conv_w, a_log, dt_bias,
                                           hist, ms, ts, ms)
    raw_s = tail_s.reshape(bs, ts, DN_QKV)
    conv_s = jnp.concatenate([conv0_s.astype(F32), raw_s], axis=1)[:, ts:, :]
    cs = DN_CHUNK
    pad = lambda a: jnp.pad(a.reshape(bs, ts, -1), ((0, 0), (0, cs - ts), (0, 0))).reshape(
        bs * cs, -1)
    o_s_pad, s_s = _gdn(pad(qkv_s), pad(bg_s), s0_s.astype(F32), bs, cs)
    o_s = o_s_pad.reshape(bs, cs, DN_Z)[:, :ts, :].reshape(ms, DN_Z)
    hs = _dn_out(o_s, z_s, g_out, w_o16, hs, nw1, ms)
    return hp, hs, s_p, s_s, conv_p, conv_s


def kernel(x_prompt, x_sample, cache_mla, state_dn, state_dn_conv, page_table, norm_w, mla_w_in,
           mla_g_q, mla_g_kv, mla_w_uq, mla_w_uk, mla_w_uv, mla_w_o, dn_w_in, dn_conv_w, dn_a_log,
           dn_dt_bias, dn_g_out, dn_w_o, ffn_w_in, ffn_w_out):
    bp, tp, _ = x_prompt.shape
    bs, ts, _ = x_sample.shape
    depth = norm_w.shape[0]
    hp = x_prompt.reshape(bp * tp, D_MODEL)
    hs = x_sample.reshape(bs * ts, D_MODEL)
    rows_p_l, rows_s_l, sp_l, ss_l, cp_l, cs_l = [], [], [], [], [], []
    for layer in range(depth):
        j = layer // N_MIXERS
        nw = norm_w[layer]
        if layer % N_MIXERS == 0:
            hp, hs, rows_p, rows_s = _mla_layer(
                hp, hs, cache_mla, page_table, j, nw, mla_w_in[j], mla_g_q[j], mla_g_kv[j],
                mla_w_uq[j], mla_w_uk[j], mla_w_uv[j], mla_w_o[j], bp, tp, bs, ts)
            rows_p_l.append(rows_p)
            rows_s_l.append(rows_s)
        else:
            hp, hs, s_p, s_s, c_p, c_s = _dn_layer(
                hp, hs, state_dn[j], state_dn_conv[j], nw, dn_w_in[j], dn_conv_w[j], dn_a_log[j],
                dn_dt_bias[j], dn_g_out[j], dn_w_o[j], bp, tp, bs, ts)
            sp_l.append(s_p.astype(state_dn.dtype))
            ss_l.append(s_s.astype(state_dn.dtype))
            cp_l.append(c_p.astype(state_dn_conv.dtype))
            cs_l.append(c_s.astype(state_dn_conv.dtype))
        w_in16 = ffn_w_in[layer].astype(BF16)
        w_out16 = ffn_w_out[layer].astype(BF16)
        nw2 = nw[2].reshape(1, D_MODEL)
        nw3 = nw[3].reshape(1, D_MODEL)
        hp = _ffn(hp, nw2, w_in16, w_out16, nw3, TM_TOKENS)
        hs = _ffn(hs, nw2, w_in16, w_out16, nw3, bs * ts)
    return (hp.reshape(bp, tp, D_MODEL), hs.reshape(bs, ts, D_MODEL),
            jnp.stack(rows_p_l), jnp.stack(rows_s_l), jnp.stack(sp_l), jnp.stack(ss_l),
            jnp.stack(cp_l), jnp.stack(cs_l))
```

```python
import functools
import math

import jax
import jax.numpy as jnp
from jax import lax
from jax.experimental import pallas as pl
from jax.experimental.pallas import tpu as pltpu

F32 = jnp.float32
BF16 = jnp.bfloat16

D_MODEL = 1024
PAGE_SIZE = 128
N_MIXERS = 2

MLA_HEADS = 8
QK_NOPE = 128
QK_ROPE = 64
V_HEAD = 128
KV_LORA = 256
Q_LORA = 384
MLA_ROW = KV_LORA + QK_ROPE
MLA_SCALE = (QK_NOPE + QK_ROPE) ** -0.5
ROPE_THETA = 10000.0

DN_HEADS = 8
DN_DK = 128
DN_DV = 128
CONV_W = 4
DN_HK = DN_HEADS * DN_DK
DN_QKV = DN_HEADS * (2 * DN_DK + DN_DV)
DN_Z = DN_HEADS * DN_DV

RMS_EPS = 1e-6
L2_EPS = 1e-6

LANES = 128
SUBLANES = 8
VMEM_LIMIT_BYTES = 56 * 1024 * 1024

TM_TOKENS = 512
TM_DN_PROJ = 256
TQ_ATTN = 256
DN_CHUNK = 64
GDN_GROUP = 4
GDN_CHUNKS_PER_STEP = 4
GDN_SEQS_PER_STEP = 2
DEC_PAGES_PER_STEP = 64
FFN_CHUNK = 256
CONV_CARRY_ROWS = SUBLANES


def _params(*sem):
    return pltpu.CompilerParams(dimension_semantics=sem, vmem_limit_bytes=VMEM_LIMIT_BYTES)


def _const_spec(shape):
    nd = len(shape)
    return pl.BlockSpec(shape, lambda *_: (0,) * nd)


def _dot(a, b):
    return jnp.dot(a, b, preferred_element_type=F32)


def _dot_nt(a, b):
    return lax.dot_general(a, b, (((1,), (1,)), ((), ())), preferred_element_type=F32)


def _dot_tn(a, b):
    return lax.dot_general(a, b, (((0,), (0,)), ((), ())), preferred_element_type=F32)


def _split2(x):
    hi = x.astype(BF16)
    lo = (x - hi.astype(F32)).astype(BF16)
    return hi, lo


def _split3(x):
    hi = x.astype(BF16)
    r = x - hi.astype(F32)
    mid = r.astype(BF16)
    lo = (r - mid.astype(F32)).astype(BF16)
    return hi, mid, lo


def _dot_hl(a, b):
    ah, al = _split2(a)
    bh, bl = _split2(b)
    return _dot(ah, bh) + (_dot(ah, bl) + _dot(al, bh))


def _rms(x, w):
    return x * lax.rsqrt(jnp.mean(x * x, axis=-1, keepdims=True) + RMS_EPS) * w


def _sigmoid(x):
    return 1.0 / (1.0 + jnp.exp(-x))


def _silu(x):
    return x * _sigmoid(x)


def _softplus(x):
    return jnp.maximum(x, 0.0) + jnp.log1p(jnp.exp(-jnp.abs(x)))


def _rope(x, cs, sn):
    half = x.shape[-1] // 2
    swapped = jnp.concatenate([x[:, half:], x[:, :half]], axis=1)
    return x * cs + swapped * sn


def _rope_tables(pos):
    half = QK_ROPE // 2
    freq = ROPE_THETA ** (-jnp.arange(half, dtype=F32) / half)
    ang = pos.astype(F32)[:, None] * freq[None, :]
    cos, sin = jnp.cos(ang), jnp.sin(ang)
    return jnp.concatenate([cos, cos], axis=1), jnp.concatenate([-sin, sin], axis=1)


def _mla_proj_body(x_ref, nw_ref, win_ref, gq_ref, gkv_ref, wuq_ref, wukt_ref, cs_ref, sn_ref,
                   rows_ref, kv_ref, q_ref):
    u = _rms(x_ref[...], nw_ref[...]).astype(BF16)
    a = _dot(u, win_ref[...])
    c_q = _rms(a[:, :Q_LORA], gq_ref[...]).astype(BF16)
    c_kv = _rms(a[:, Q_LORA:Q_LORA + KV_LORA], gkv_ref[...])
    cs = cs_ref[...]
    sn = sn_ref[...]
    k_r = _rope(a[:, Q_LORA + KV_LORA:], cs, sn)
    rows_ref[:, :KV_LORA] = c_kv
    rows_ref[:, KV_LORA:] = k_r
    kv_ref[:, :KV_LORA] = c_kv.astype(BF16)
    kv_ref[:, KV_LORA:] = k_r.astype(BF16)
    q = _dot(c_q, wuq_ref[...])
    for h in range(MLA_HEADS):
        base = h * (QK_NOPE + QK_ROPE)
        q_lat = _dot(q[:, base:base + QK_NOPE].astype(BF16), wukt_ref[h])
        q_rope = _rope(q[:, base + QK_NOPE:base + QK_NOPE + QK_ROPE], cs, sn)
        q_ref[h, :, :KV_LORA] = (q_lat * MLA_SCALE).astype(BF16)
        q_ref[h, :, KV_LORA:] = (q_rope * MLA_SCALE).astype(BF16)


def _mla_project(x, nw, w_in, g_q, g_kv, w_uq, w_ukt, cs, sn, tm):
    m = x.shape[0]
    row = lambda i: (i, 0)
    return pl.pallas_call(
        _mla_proj_body,
        grid=(m // tm,),
        in_specs=[
            pl.BlockSpec((tm, D_MODEL), row),
            _const_spec((1, D_MODEL)),
            _const_spec(w_in.shape),
            _const_spec((1, Q_LORA)),
            _const_spec((1, KV_LORA)),
            _const_spec(w_uq.shape),
            _const_spec(w_ukt.shape),
            pl.BlockSpec((tm, QK_ROPE), row),
            pl.BlockSpec((tm, QK_ROPE), row),
        ],
        out_specs=[
            pl.BlockSpec((tm, MLA_ROW), row),
            pl.BlockSpec((tm, MLA_ROW), row),
            pl.BlockSpec((MLA_HEADS, tm, MLA_ROW), lambda i: (0, i, 0)),
        ],
        out_shape=[
            jax.ShapeDtypeStruct((m, MLA_ROW), F32),
            jax.ShapeDtypeStruct((m, MLA_ROW), BF16),
            jax.ShapeDtypeStruct((MLA_HEADS, m, MLA_ROW), BF16),
        ],
        compiler_params=_params("parallel"),
        name="mla_project",
    )(x, nw, w_in, g_q, g_kv, w_uq, w_ukt, cs, sn)


def _softmax_update(s, v, m_ref, l_ref, acc_ref, v_transposed=False):
    m_prev = m_ref[...]
    m_new = jnp.maximum(m_prev, jnp.max(s, axis=1, keepdims=True))
    alpha = jnp.exp(m_prev - m_new)
    p = jnp.exp(s - m_new)
    p16 = p.astype(BF16)
    pv = _dot_nt(p16, v) if v_transposed else _dot(p16, v)
    l_ref[...] = alpha * l_ref[...] + jnp.sum(p, axis=1, keepdims=True)
    acc_ref[...] = alpha * acc_ref[...] + pv
    m_ref[...] = m_new


def _softmax_init(m_ref, l_ref, acc_ref):
    m_ref[...] = jnp.full(m_ref.shape, -jnp.inf, F32)
    l_ref[...] = jnp.zeros(l_ref.shape, F32)
    acc_ref[...] = jnp.zeros(acc_ref.shape, F32)


def _attn_prompt_body(q_ref, kv_ref, kvt_ref, o_ref, m_ref, l_ref, acc_ref, sa_ref, sb_ref, *, tq):
    i = pl.program_id(1)
    heads = q_ref.shape[0]
    cols = heads * tq
    q = q_ref[...].reshape(cols, MLA_ROW)
    _softmax_init(m_ref, l_ref, acc_ref)

    def scores(j, buf):
        k = kv_ref[pl.ds(pl.multiple_of(j * tq, tq), tq), :]
        buf[...] = _dot_nt(k, q)

    def absorb(j, buf, masked):
        st = buf[...]
        if masked:
            key = lax.broadcasted_iota(jnp.int32, (tq, cols), 0)
            tok = lax.broadcasted_iota(jnp.int32, (tq, cols), 1) % tq
            st = jnp.where(key <= tok, st, -jnp.inf)
        m_prev = m_ref[...]
        m_new = jnp.maximum(m_prev, jnp.max(st, axis=0, keepdims=True))
        alpha = jnp.exp(m_prev - m_new)
        p = jnp.exp(st - m_new)
        l_ref[...] = alpha * l_ref[...] + jnp.sum(p, axis=0, keepdims=True)
        acc_ref[...] = alpha * acc_ref[...] + _dot(kvt_ref[j], p.astype(BF16))
        m_ref[...] = m_new

    scores(0, sa_ref)

    def body(p, carry):
        scores(2 * p + 1, sb_ref)
        absorb(2 * p, sa_ref, False)
        scores(2 * p + 2, sa_ref)
        absorb(2 * p + 1, sb_ref, False)
        return carry

    lax.fori_loop(0, i // 2, body, 0)

    @pl.when(i % 2 == 0)
    def _():
        absorb(i, sa_ref, True)

    @pl.when(i % 2 == 1)
    def _():
        scores(i, sb_ref)
        absorb(i - 1, sa_ref, False)
        absorb(i, sb_ref, True)

    o = (acc_ref[...] / l_ref[...]).T
    o_ref[...] = o.reshape(heads, tq, KV_LORA).astype(BF16)


def _attn_prompt(q, kv, batch, seq, tq):
    heads, m, _ = q.shape
    nq = seq // tq
    kvt = kv[:, :KV_LORA].reshape(batch * nq, tq, KV_LORA).transpose(0, 2, 1)
    return pl.pallas_call(
        functools.partial(_attn_prompt_body, tq=tq),
        grid=(batch, nq),
        in_specs=[
            pl.BlockSpec((heads, tq, MLA_ROW), lambda b, i: (0, b * nq + i, 0)),
            pl.BlockSpec((seq, MLA_ROW), lambda b, i: (b, 0)),
            pl.BlockSpec((nq, KV_LORA, tq), lambda b, i: (b, 0, 0)),
        ],
        out_specs=pl.BlockSpec((heads, tq, KV_LORA), lambda b, i: (0, b * nq + i, 0)),
        out_shape=jax.ShapeDtypeStruct((heads, m, KV_LORA), BF16),
        scratch_shapes=[
            pltpu.VMEM((1, heads * tq), F32),
            pltpu.VMEM((1, heads * tq), F32),
            pltpu.VMEM((KV_LORA, heads * tq), F32),
            pltpu.VMEM((tq, heads * tq), F32),
            pltpu.VMEM((tq, heads * tq), F32),
        ],
        compiler_params=_params("parallel", "parallel"),
        name="mla_attn_prompt",
    )(q, kv, kvt)


def _attn_decode_body(pt_ref, q_ref, kn_ref, *rest, pages, ts):
    del pt_ref
    cache_refs = rest[:pages]
    o_ref, m_ref, l_ref, acc_ref = rest[pages:]
    c = pl.program_id(1)

    @pl.when(c == 0)
    def _():
        _softmax_init(m_ref, l_ref, acc_ref)

    q = q_ref[0]
    kt = jnp.concatenate([r[0, 0].astype(BF16) for r in cache_refs], axis=1)
    _softmax_update(_dot(q, kt), kt[:KV_LORA, :], m_ref, l_ref, acc_ref, v_transposed=True)

    @pl.when(c == pl.num_programs(1) - 1)
    def _():
        kn = kn_ref[0]
        s = _dot_nt(q, kn)
        tok = lax.broadcasted_iota(jnp.int32, s.shape, 0) % ts
        key = lax.broadcasted_iota(jnp.int32, s.shape, 1)
        s = jnp.where(key <= tok, s, -jnp.inf)
        _softmax_update(s, kn[:, :KV_LORA], m_ref, l_ref, acc_ref)
        o_ref[0] = (acc_ref[...] / l_ref[...]).astype(BF16)


def _attn_decode(q, k_new, cache_t, page_table, layer_slot, ts):
    bs, rows, _ = q.shape
    n_pages = page_table.shape[1]
    pages = DEC_PAGES_PER_STEP
    assert n_pages % pages == 0
    cache = cache_t
    cache_specs = [
        pl.BlockSpec((1, 1, MLA_ROW, PAGE_SIZE),
                     lambda b, c, pt, p=p: (layer_slot, pt[b, c * pages + p], 0, 0))
        for p in range(pages)
    ]
    grid_spec = pltpu.PrefetchScalarGridSpec(
        num_scalar_prefetch=1,
        grid=(bs, n_pages // pages),
        in_specs=[
            pl.BlockSpec((1, rows, MLA_ROW), lambda b, c, pt: (b, 0, 0)),
            pl.BlockSpec((1, k_new.shape[1], MLA_ROW), lambda b, c, pt: (b, 0, 0)),
        ] + cache_specs,
        out_specs=pl.BlockSpec((1, rows, KV_LORA), lambda b, c, pt: (b, 0, 0)),
        scratch_shapes=[
            pltpu.VMEM((rows, 1), F32),
            pltpu.VMEM((rows, 1), F32),
            pltpu.VMEM((rows, KV_LORA), F32),
        ],
    )
    return pl.pallas_call(
        functools.partial(_attn_decode_body, pages=pages, ts=ts),
        grid_spec=grid_spec,
        out_shape=jax.ShapeDtypeStruct((bs, rows, KV_LORA), BF16),
        compiler_params=_params("parallel", "arbitrary"),
        name="mla_attn_decode",
    )(page_table, q, k_new, *([cache] * pages))


def _mla_out_body(o_ref, wuv_ref, wo_ref, h_ref, nw_ref, out_ref):
    vs = [_dot(o_ref[h], wuv_ref[h]).astype(BF16) for h in range(MLA_HEADS)]
    mix = _dot(jnp.concatenate(vs, axis=1), wo_ref[...])
    out_ref[...] = h_ref[...] + _rms(mix, nw_ref[...])


def _mla_out(o_lat, w_uv, w_o, h, nw, tm):
    m = h.shape[0]
    row = lambda i: (i, 0)
    return pl.pallas_call(
        _mla_out_body,
        grid=(m // tm,),
        in_specs=[
            pl.BlockSpec((MLA_HEADS, tm, KV_LORA), lambda i: (0, i, 0)),
            _const_spec(w_uv.shape),
            _const_spec(w_o.shape),
            pl.BlockSpec((tm, D_MODEL), row),
            _const_spec((1, D_MODEL)),
        ],
        out_specs=pl.BlockSpec((tm, D_MODEL), row),
        out_shape=jax.ShapeDtypeStruct((m, D_MODEL), F32),
        compiler_params=_params("parallel"),
        name="mla_out",
    )(o_lat, w_uv, w_o, h, nw)


def _ffn_body(h_ref, nw_in_ref, win_ref, wout_ref, nw_out_ref, out_ref, *, d_ff, chunk):
    x = h_ref[...]
    u = _rms(x, nw_in_ref[...]).astype(BF16)
    acc = jnp.zeros(x.shape, F32)
    for c in range(d_ff // chunk):
        lo = c * chunk
        gate = _dot(u, win_ref[:, lo:lo + chunk])
        up = _dot(u, win_ref[:, d_ff + lo:d_ff + lo + chunk])
        act = (_silu(gate) * up).astype(BF16)
        acc = acc + _dot(act, wout_ref[lo:lo + chunk, :])
    out_ref[...] = x + _rms(acc, nw_out_ref[...])


def _ffn(h, nw_in, w_in, w_out, nw_out, tm):
    m = h.shape[0]
    d_ff = w_out.shape[0]
    assert d_ff % FFN_CHUNK == 0
    row = lambda i: (i, 0)
    return pl.pallas_call(
        functools.partial(_ffn_body, d_ff=d_ff, chunk=FFN_CHUNK),
        grid=(m // tm,),
        in_specs=[
            pl.BlockSpec((tm, D_MODEL), row),
            _const_spec((1, D_MODEL)),
            _const_spec(w_in.shape),
            _const_spec(w_out.shape),
            _const_spec((1, D_MODEL)),
        ],
        out_specs=pl.BlockSpec((tm, D_MODEL), row),
        out_shape=jax.ShapeDtypeStruct((m, D_MODEL), F32),
        compiler_params=_params("parallel"),
        name="ffn",
    )(h, nw_in, w_in, w_out, nw_out)


def _dn_proj_body(*refs, tm, seq_len, has_hist, tail_rows):
    if has_hist:
        (x_ref, nw_ref, wqkv_ref, wz_ref, wba_ref, cw_ref, alog_ref, dtb_ref, hist_ref,
         qkv_ref, z_ref, bg_ref, tail_ref, ext_ref) = refs
    else:
        (x_ref, nw_ref, wqkv_ref, wz_ref, wba_ref, cw_ref, alog_ref, dtb_ref,
         qkv_ref, z_ref, bg_ref, tail_ref, ext_ref) = refs
        hist_ref = None
    i = pl.program_id(0)
    carry = CONV_CARRY_ROWS

    @pl.when((i * tm) % seq_len == 0)
    def _():
        ext_ref[0:carry, :] = jnp.zeros((carry, DN_QKV), F32)

    u = _rms(x_ref[...], nw_ref[...]).astype(BF16)
    raw = _dot(u, wqkv_ref[...])
    ext_ref[carry:carry + tm, :] = raw
    cw = cw_ref[...]
    acc = raw * cw[CONV_W - 1:CONV_W, :]
    for k in range(1, CONV_W):
        shifted = ext_ref[carry - k:carry - k + tm, :]
        if has_hist:
            tpos = lax.broadcasted_iota(jnp.int32, (tm, 1), 0) % seq_len
            shifted = jnp.where(tpos >= k, shifted, hist_ref[k - 1])
        acc = acc + shifted * cw[CONV_W - 1 - k:CONV_W - k, :]
    ext_ref[0:carry, :] = raw[tm - carry:, :]
    tail_ref[0] = raw[tm - tail_rows:, :]
    act = _silu(acc)
    for h in range(DN_HEADS):
        qh = act[:, h * DN_DK:(h + 1) * DN_DK]
        kh = act[:, DN_HK + h * DN_DK:DN_HK + (h + 1) * DN_DK]
        qn = qh * lax.rsqrt(jnp.sum(qh * qh, axis=-1, keepdims=True) + L2_EPS) * (DN_DK ** -0.5)
        kn = kh * lax.rsqrt(jnp.sum(kh * kh, axis=-1, keepdims=True) + L2_EPS)
        qkv_ref[:, h * DN_DK:(h + 1) * DN_DK] = qn
        qkv_ref[:, DN_HK + h * DN_DK:DN_HK + (h + 1) * DN_DK] = kn
    qkv_ref[:, 2 * DN_HK:] = act[:, 2 * DN_HK:]
    z_ref[...] = _dot(u, wz_ref[...])
    ba = _dot(u, wba_ref[...])
    beta = _sigmoid(ba[:, :DN_HEADS])
    g = -jnp.exp(alog_ref[...]) * _softplus(ba[:, DN_HEADS:2 * DN_HEADS] + dtb_ref[...])
    bg_ref[:, :DN_HEADS] = beta
    bg_ref[:, DN_HEADS:] = g


def _dn_project(x, nw, w_qkv, w_z, w_ba, conv_w, a_log, dt_bias, hist, tm, seq_len, tail_rows):
    m = x.shape[0]
    n_tiles = m // tm
    has_hist = hist is not None
    assert (tm % seq_len == 0 and n_tiles == 1) if has_hist else seq_len % tm == 0
    tiles_per_tail = max(seq_len // tm, 1)
    row = lambda i: (i, 0)
    in_specs = [
        pl.BlockSpec((tm, D_MODEL), row),
        _const_spec((1, D_MODEL)),
        _const_spec(w_qkv.shape),
        _const_spec(w_z.shape),
        _const_spec(w_ba.shape),
        _const_spec(conv_w.shape),
        _const_spec((1, DN_HEADS)),
        _const_spec((1, DN_HEADS)),
    ]
    args = [x, nw, w_qkv, w_z, w_ba, conv_w, a_log, dt_bias]
    if has_hist:
        in_specs.append(pl.BlockSpec((CONV_W - 1, tm, DN_QKV), lambda i: (0, i, 0)))
        args.append(hist)
    return pl.pallas_call(
        functools.partial(_dn_proj_body, tm=tm, seq_len=seq_len, has_hist=has_hist,
                          tail_rows=tail_rows),
        grid=(n_tiles,),
        in_specs=in_specs,
        out_specs=[
            pl.BlockSpec((tm, DN_QKV), row),
            pl.BlockSpec((tm, DN_Z), row),
            pl.BlockSpec((tm, 2 * DN_HEADS), row),
            pl.BlockSpec((1, tail_rows, DN_QKV), lambda i: (i // tiles_per_tail, 0, 0)),
        ],
        out_shape=[
            jax.ShapeDtypeStruct((m, DN_QKV), F32),
            jax.ShapeDtypeStruct((m, DN_Z), F32),
            jax.ShapeDtypeStruct((m, 2 * DN_HEADS), F32),
            jax.ShapeDtypeStruct((n_tiles // tiles_per_tail, tail_rows, DN_QKV), F32),
        ],
        scratch_shapes=[pltpu.VMEM((CONV_CARRY_ROWS + tm, DN_QKV), F32)],
        compiler_params=_params("arbitrary"),
        name="dn_project",
    )(*args)


def _stack_heads(load, g):
    return jnp.concatenate(
        [load(slice((g * GDN_GROUP + hh) * DN_DK, (g * GDN_GROUP + hh + 1) * DN_DK))
         for hh in range(GDN_GROUP)], axis=0)


def _gdn_local_body(q_ref, k_ref, v_ref, bg_ref, bgt_ref,
                    u_ref, w_ref, qd_ref, kdt_ref, qk_ref, egl_ref, *, chunk, chunks_per_step):
    n_heads = DN_HEADS
    stack = GDN_GROUP * chunk
    ri = lax.broadcasted_iota(jnp.int32, (stack, stack), 0)
    ci = lax.broadcasted_iota(jnp.int32, (stack, stack), 1)
    same_head = (ri // chunk) == (ci // chunk)
    causal = same_head & (ri >= ci)
    strict = same_head & (ri > ci)
    r1 = lax.broadcasted_iota(jnp.int32, (chunk, chunk), 0)
    c1 = lax.broadcasted_iota(jnp.int32, (chunk, chunk), 1)
    tril = jnp.where(r1 >= c1, 1.0, 0.0).astype(BF16)
    triu = jnp.where(r1 <= c1, 1.0, 0.0).astype(BF16)
    n_double = int(math.log2(chunk)) - 1

    def store_heads(ref, rows, g, stacked):
        for hh in range(GDN_GROUP):
            h = g * GDN_GROUP + hh
            ref[rows, h * DN_DK:(h + 1) * DN_DK] = stacked[hh * chunk:(hh + 1) * chunk, :]

    chains = []
    for cp in range(chunks_per_step):
        rows = slice(cp * chunk, (cp + 1) * chunk)
        bg = bg_ref[rows, :]
        bgt = bgt_ref[cp]
        gc_col = sum(_dot(tril, part) for part in _split3(bg))
        gc_row = sum(_dot(part, triu) for part in _split3(bgt))
        egl_ref[cp] = jnp.broadcast_to(jnp.exp(gc_row[n_heads:, chunk - 1:chunk]),
                                       (n_heads, LANES))
        for g in range(n_heads // GDN_GROUP):
            heads = [g * GDN_GROUP + hh for hh in range(GDN_GROUP)]
            q = _stack_heads(lambda sl: q_ref[rows, sl], g)
            k = _stack_heads(lambda sl: k_ref[rows, sl], g)
            v = _stack_heads(lambda sl: v_ref[rows, sl], g)
            beta = jnp.concatenate([bg[:, h:h + 1] for h in heads], axis=0)
            gcc = jnp.concatenate([gc_col[:, n_heads + h:n_heads + h + 1] for h in heads], axis=0)
            gcr = jnp.concatenate([gc_row[n_heads + h:n_heads + h + 1, :] for h in heads], axis=1)
            g_last = jnp.concatenate(
                [jnp.broadcast_to(gc_col[chunk - 1:chunk, n_heads + h:n_heads + h + 1], (chunk, 1))
                 for h in heads], axis=0)
            eg = jnp.exp(gcc)
            kb = k * beta
            store_heads(qd_ref, rows, g, (q * eg).astype(BF16))
            kdt_ref[cp, :, g * stack:(g + 1) * stack] = (k * jnp.exp(g_last - gcc)).T.astype(BF16)
            chains.append(dict(
                rows=rows, g=g, q16=q.astype(BF16), k16=k.astype(BF16), kb16=kb.astype(BF16),
                decay=jnp.where(causal, jnp.exp(gcc - gcr), 0.0),
                rhs=jnp.concatenate([v * beta, kb * eg], axis=1)))

    for ch in chains:
        lower = jnp.where(strict, _dot_nt(ch["kb16"], ch["k16"]) * ch["decay"], 0.0)
        ch.update(lower=lower, power=lower, off=-lower)
    for ch in chains:
        qk = jnp.where(causal, _dot_nt(ch["q16"], ch["k16"]) * ch["decay"], 0.0)
        qk_packed = sum(qk[hh * chunk:(hh + 1) * chunk, :] for hh in range(GDN_GROUP))
        qk_ref[ch["rows"], ch["g"] * stack:(ch["g"] + 1) * stack] = qk_packed.astype(BF16)
    for _ in range(n_double):
        for ch in chains:
            p16 = ch["power"].astype(BF16)
            ch["power"] = _dot(p16, p16)
        for ch in chains:
            ch["off"] = ch["off"] + ch["power"] + _dot(ch["off"].astype(BF16),
                                                       ch["power"].astype(BF16))
    for ch in chains:
        ch["resid"] = (ch["lower"] + ch["off"]) + _dot_hl(ch["lower"], ch["off"])
    for ch in chains:
        ch["off"] = ch["off"] - ch["resid"] - _dot(ch["off"].astype(BF16),
                                                   ch["resid"].astype(BF16))
    for ch in chains:
        sol = ch["rhs"] + _dot(ch["off"].astype(BF16), ch["rhs"].astype(BF16))
        store_heads(u_ref, ch["rows"], ch["g"], sol[:, :DN_DV])
        store_heads(w_ref, ch["rows"], ch["g"], sol[:, DN_DV:].astype(BF16))


def _gdn_scan_body(u_ref, w_ref, qd_ref, kdt_ref, qk_ref, egl_ref, z_ref, gout_ref, s0_ref,
                   y_ref, s_ref, *, chunk, seqs):
    c = pl.program_id(1)

    @pl.when(c == 0)
    def _():
        s_ref[...] = s0_ref[...]

    stack = GDN_GROUP * chunk
    pair = 2 * DN_DK
    lhs_mask = ((lax.broadcasted_iota(jnp.int32, (4 * chunk, pair), 0) // chunk) % 2
                == lax.broadcasted_iota(jnp.int32, (4 * chunk, pair), 1) // DN_DK)
    qk_mask = (lax.broadcasted_iota(jnp.int32, (stack, stack), 0) // chunk
               == lax.broadcasted_iota(jnp.int32, (stack, stack), 1) // chunk)
    kd_mask = (lax.broadcasted_iota(jnp.int32, (GDN_GROUP * DN_DK, stack), 0) // DN_DK
               == lax.broadcasted_iota(jnp.int32, (GDN_GROUP * DN_DK, stack), 1) // chunk)
    zero16 = jnp.zeros((), BF16)
    chains = [dict(sq=sq, g=g) for sq in range(seqs) for g in range(DN_HEADS // GDN_GROUP)]
    for ch in chains:
        sq, g = ch["sq"], ch["g"]
        s = s_ref[sq, g * GDN_GROUP:(g + 1) * GDN_GROUP].reshape(GDN_GROUP * DN_DK, DN_DV)
        s16 = s.astype(BF16)
        ws, qs = [], []
        for p in range(GDN_GROUP // 2):
            cols = [slice((g * GDN_GROUP + 2 * p + e) * DN_DK, (g * GDN_GROUP + 2 * p + e + 1) * DN_DK)
                    for e in range(2)]
            lhs = jnp.concatenate([w_ref[sq, :, cols[0]], w_ref[sq, :, cols[1]],
                                   qd_ref[sq, :, cols[0]], qd_ref[sq, :, cols[1]]], axis=0)
            lhs_bd = jnp.where(lhs_mask, jnp.concatenate([lhs, lhs], axis=1), zero16)
            res = _dot(lhs_bd, s16[p * pair:(p + 1) * pair, :])
            ws.append(res[:2 * chunk])
            qs.append(res[2 * chunk:])
        ch.update(s=s, ws=jnp.concatenate(ws, axis=0), qs=jnp.concatenate(qs, axis=0))
    for ch in chains:
        sq, g = ch["sq"], ch["g"]
        v_new = _stack_heads(lambda sl: u_ref[sq, :, sl], g) - ch["ws"]
        ch["v16"] = v_new.astype(BF16)
    gout = gout_ref[...]
    for ch in chains:
        sq, g = ch["sq"], ch["g"]
        qk = qk_ref[sq, :, g * stack:(g + 1) * stack]
        qk_bd = jnp.where(qk_mask, jnp.concatenate([qk] * GDN_GROUP, axis=0), zero16)
        o = ch["qs"] + _dot(qk_bd, ch["v16"])
        y = _rms(o, gout) * _silu(_stack_heads(lambda sl: z_ref[sq, :, sl], g))
        y16 = y.astype(BF16)
        for hh in range(GDN_GROUP):
            h = g * GDN_GROUP + hh
            y_ref[sq, :, h * DN_DV:(h + 1) * DN_DV] = y16[hh * chunk:(hh + 1) * chunk, :]
    for ch in chains:
        sq, g = ch["sq"], ch["g"]
        kdt = kdt_ref[sq, 0, :, g * stack:(g + 1) * stack]
        kdt_bd = jnp.where(kd_mask, jnp.concatenate([kdt] * GDN_GROUP, axis=0), zero16)
        gate = jnp.concatenate(
            [jnp.broadcast_to(egl_ref[sq, 0, g * GDN_GROUP + hh:g * GDN_GROUP + hh + 1, :],
                              (DN_DK, DN_DV)) for hh in range(GDN_GROUP)], axis=0)
        s_new = ch["s"] * gate + _dot(kdt_bd, ch["v16"])
        s_ref[sq, g * GDN_GROUP:(g + 1) * GDN_GROUP] = s_new.reshape(GDN_GROUP, DN_DK, DN_DV)


def _gdn(qkv, bg, z, g_out, s0, n_seq, seq_len):
    chunk = DN_CHUNK
    m = qkv.shape[0]
    n_chunks = m // chunk
    nc = seq_len // chunk
    cps = GDN_CHUNKS_PER_STEP
    sps = GDN_SEQS_PER_STEP
    assert n_chunks % cps == 0 and n_seq % sps == 0
    bgt = bg.reshape(n_chunks, chunk, 2 * DN_HEADS).transpose(0, 2, 1)
    half = DN_HEADS * chunk
    row = lambda i: (i, 0)
    blk = lambda col: pl.BlockSpec((cps * chunk, DN_HK), lambda i, col=col: (i, col))
    u, w16, qd16, kdt16, qk16, egl = pl.pallas_call(
        functools.partial(_gdn_local_body, chunk=chunk, chunks_per_step=cps),
        grid=(n_chunks // cps,),
        in_specs=[
            blk(0), blk(1), blk(2),
            pl.BlockSpec((cps * chunk, 2 * DN_HEADS), row),
            pl.BlockSpec((cps, 2 * DN_HEADS, chunk), lambda i: (i, 0, 0)),
        ],
        out_specs=[
            pl.BlockSpec((cps * chunk, DN_Z), row),
            pl.BlockSpec((cps * chunk, DN_HK), row),
            pl.BlockSpec((cps * chunk, DN_HK), row),
            pl.BlockSpec((cps, DN_DK, half), lambda i: (i, 0, 0)),
            pl.BlockSpec((cps * chunk, half), row),
            pl.BlockSpec((cps, DN_HEADS, LANES), lambda i: (i, 0, 0)),
        ],
        out_shape=[
            jax.ShapeDtypeStruct((m, DN_Z), F32),
            jax.ShapeDtypeStruct((m, DN_HK), BF16),
            jax.ShapeDtypeStruct((m, DN_HK), BF16),
            jax.ShapeDtypeStruct((n_chunks, DN_DK, half), BF16),
            jax.ShapeDtypeStruct((m, half), BF16),
            jax.ShapeDtypeStruct((n_chunks, DN_HEADS, LANES), F32),
        ],
        compiler_params=_params("parallel"),
        name="gdn_local",
    )(qkv, qkv, qkv, bg, bgt)

    per_seq = lambda a: a.reshape((n_seq, nc if a.ndim == 3 else seq_len) + a.shape[1:])
    tok_spec = lambda width: pl.BlockSpec((sps, chunk, width), lambda b, c: (b, c, 0))
    lead_spec = lambda d1, d2: pl.BlockSpec((sps, 1, d1, d2), lambda b, c: (b, c, 0, 0))
    state_spec = pl.BlockSpec((sps, DN_HEADS, DN_DK, DN_DV), lambda b, c: (b, 0, 0, 0))
    y16, s = pl.pallas_call(
        functools.partial(_gdn_scan_body, chunk=chunk, seqs=sps),
        grid=(n_seq // sps, nc),
        in_specs=[
            tok_spec(DN_Z),
            tok_spec(DN_HK),
            tok_spec(DN_HK),
            lead_spec(DN_DK, half),
            tok_spec(half),
            lead_spec(DN_HEADS, LANES),
            tok_spec(DN_Z),
            pl.BlockSpec((1, DN_DV), lambda b, c: (0, 0)),
            state_spec,
        ],
        out_specs=[tok_spec(DN_Z), state_spec],
        out_shape=[
            jax.ShapeDtypeStruct((n_seq, seq_len, DN_Z), BF16),
            jax.ShapeDtypeStruct(s0.shape, F32),
        ],
        compiler_params=_params("parallel", "arbitrary"),
        name="gdn_scan",
    )(per_seq(u), per_seq(w16), per_seq(qd16), per_seq(kdt16), per_seq(qk16), per_seq(egl),
      per_seq(z), g_out, s0)
    return y16.reshape(m, DN_Z), s


def _dn_out_body(y_ref, wo_ref, h_ref, nw_ref, out_ref):
    mix = _dot(y_ref[...], wo_ref[...])
    out_ref[...] = h_ref[...] + _rms(mix, nw_ref[...])


def _dn_out(y16, w_o, h, nw, tm):
    m = h.shape[0]
    row = lambda i: (i, 0)
    return pl.pallas_call(
        _dn_out_body,
        grid=(m // tm,),
        in_specs=[
            pl.BlockSpec((tm, DN_Z), row),
            _const_spec(w_o.shape),
            pl.BlockSpec((tm, D_MODEL), row),
            _const_spec((1, D_MODEL)),
        ],
        out_specs=pl.BlockSpec((tm, D_MODEL), row),
        out_shape=jax.ShapeDtypeStruct((m, D_MODEL), F32),
        compiler_params=_params("parallel"),
        name="dn_out",
    )(y16, w_o, h, nw)


def _mla_layer(hp, hs, cache_mla, page_table, slot, nw, w_in, g_q, g_kv, w_uq, w_uk, w_uv, w_o,
               bp, tp, bs, ts):
    past = page_table.shape[1] * PAGE_SIZE
    w_in16 = w_in.astype(BF16)
    w_uq16 = w_uq.astype(BF16)
    w_ukt16 = jnp.swapaxes(w_uk, 1, 2).astype(BF16)
    w_uv16 = w_uv.astype(BF16)
    w_o16 = w_o.astype(BF16)
    g_q = g_q.reshape(1, Q_LORA)
    g_kv = g_kv.reshape(1, KV_LORA)
    nw0 = nw[0].reshape(1, D_MODEL)
    nw1 = nw[1].reshape(1, D_MODEL)

    cs_p, sn_p = _rope_tables(jnp.tile(jnp.arange(tp), bp))
    cs_s, sn_s = _rope_tables(past + jnp.tile(jnp.arange(ts), bs))
    ms = bs * ts
    rows_p, kv_p, q_p = _mla_project(hp, nw0, w_in16, g_q, g_kv, w_uq16, w_ukt16, cs_p, sn_p,
                                     TM_TOKENS)
    rows_s, kv_s, q_s = _mla_project(hs, nw0, w_in16, g_q, g_kv, w_uq16, w_ukt16, cs_s, sn_s, ms)

    o_p = _attn_prompt(q_p, kv_p, bp, tp, TQ_ATTN)

    q_sb = q_s.reshape(MLA_HEADS, bs, ts, MLA_ROW).transpose(1, 0, 2, 3).reshape(
        bs, MLA_HEADS * ts, MLA_ROW)
    new_rows = 2 * SUBLANES
    k_new = jnp.pad(kv_s.reshape(bs, ts, MLA_ROW), ((0, 0), (0, new_rows - ts), (0, 0)))
    o_sb = _attn_decode(q_sb, k_new, jnp.swapaxes(cache_mla, 2, 3), page_table, slot, ts)
    o_s = o_sb.reshape(bs, MLA_HEADS, ts, KV_LORA).transpose(1, 0, 2, 3).reshape(
        MLA_HEADS, ms, KV_LORA)

    hp = _mla_out(o_p, w_uv16, w_o16, hp, nw1, TM_TOKENS)
    hs = _mla_out(o_s, w_uv16, w_o16, hs, nw1, ms)
    return hp, hs, rows_p.reshape(bp, tp, MLA_ROW), rows_s.reshape(bs, ts, MLA_ROW)


def _dn_layer(hp, hs, s0_s, conv0_s, nw, w_in, conv_w, a_log, dt_bias, g_out, w_o, bp, tp, bs, ts):
    w_qkv16 = w_in[:, :DN_QKV].astype(BF16)
    w_z16 = w_in[:, DN_QKV:DN_QKV + DN_Z].astype(BF16)
    w_ba16 = jnp.pad(w_in[:, DN_QKV + DN_Z:], ((0, 0), (0, LANES - 2 * DN_HEADS))).astype(BF16)
    w_o16 = w_o.astype(BF16)
    a_log = a_log.reshape(1, DN_HEADS).astype(F32)
    dt_bias = dt_bias.reshape(1, DN_HEADS).astype(F32)
    g_out = g_out.reshape(1, DN_DV)
    nw0 = nw[0].reshape(1, D_MODEL)
    nw1 = nw[1].reshape(1, D_MODEL)
    hist_rows = CONV_W - 1
    ms = bs * ts

    qkv_p, z_p, bg_p, tail_p = _dn_project(hp, nw0, w_qkv16, w_z16, w_ba16, conv_w, a_log, dt_bias,
                                           None, TM_DN_PROJ, tp, SUBLANES)
    conv_p = tail_p[:, SUBLANES - hist_rows:, :]
    s0_p = jnp.zeros((bp, DN_HEADS, DN_DK, DN_DV), F32)
    y_p, s_p = _gdn(qkv_p, bg_p, z_p, g_out, s0_p, bp, tp)
    hp = _dn_out(y_p, w_o16, hp, nw1, TM_TOKENS)

    tok = jnp.arange(ts)
    hist = jnp.stack([
        conv0_s[:, jnp.clip(hist_rows - k + tok, 0, hist_rows - 1), :].reshape(ms, DN_QKV)
        for k in range(1, CONV_W)
    ]).astype(F32)
    qkv_s, z_s, bg_s, tail_s = _dn_project(hs, nw0, w_qkv16, w_z16, w_ba16, conv_w, a_log, dt_bias,
                                           hist, ms, ts, ms)
    raw_s = tail_s.reshape(bs, ts, DN_QKV)
    conv_s = jnp.concatenate([conv0_s.astype(F32), raw_s], axis=1)[:, ts:, :]
    cs = DN_CHUNK
    pad = lambda a: jnp.pad(a.reshape(bs, ts, -1), ((0, 0), (0, cs - ts), (0, 0))).reshape(
        bs * cs, -1)
    y_s_pad, s_s = _gdn(pad(qkv_s), pad(bg_s), pad(z_s), g_out, s0_s.astype(F32), bs, cs)
    y_s = y_s_pad.reshape(bs, cs, DN_Z)[:, :ts, :].reshape(ms, DN_Z)
    hs = _dn_out(y_s, w_o16, hs, nw1, ms)
    return hp, hs, s_p, s_s, conv_p, conv_s


def kernel(x_prompt, x_sample, cache_mla, state_dn, state_dn_conv, page_table, norm_w, mla_w_in,
           mla_g_q, mla_g_kv, mla_w_uq, mla_w_uk, mla_w_uv, mla_w_o, dn_w_in, dn_conv_w, dn_a_log,
           dn_dt_bias, dn_g_out, dn_w_o, ffn_w_in, ffn_w_out):
    bp, tp, _ = x_prompt.shape
    bs, ts, _ = x_sample.shape
    depth = norm_w.shape[0]
    hp = x_prompt.reshape(bp * tp, D_MODEL)
    hs = x_sample.reshape(bs * ts, D_MODEL)
    rows_p_l, rows_s_l, sp_l, ss_l, cp_l, cs_l = [], [], [], [], [], []
    for layer in range(depth):
        j = layer // N_MIXERS
        nw = norm_w[layer]
        if layer % N_MIXERS == 0:
            hp, hs, rows_p, rows_s = _mla_layer(
                hp, hs, cache_mla, page_table, j, nw, mla_w_in[j], mla_g_q[j], mla_g_kv[j],
                mla_w_uq[j], mla_w_uk[j], mla_w_uv[j], mla_w_o[j], bp, tp, bs, ts)
            rows_p_l.append(rows_p)
            rows_s_l.append(rows_s)
        else:
            hp, hs, s_p, s_s, c_p, c_s = _dn_layer(
                hp, hs, state_dn[j], state_dn_conv[j], nw, dn_w_in[j], dn_conv_w[j], dn_a_log[j],
                dn_dt_bias[j], dn_g_out[j], dn_w_o[j], bp, tp, bs, ts)
            sp_l.append(s_p.astype(state_dn.dtype))
            ss_l.append(s_s.astype(state_dn.dtype))
            cp_l.append(c_p.astype(state_dn_conv.dtype))
            cs_l.append(c_s.astype(state_dn_conv.dtype))
        w_in16 = ffn_w_in[layer].astype(BF16)
        w_out16 = ffn_w_out[layer].astype(BF16)
        nw2 = nw[2].reshape(1, D_MODEL)
        nw3 = nw[3].reshape(1, D_MODEL)
        hp = _ffn(hp, nw2, w_in16, w_out16, nw3, TM_TOKENS)
        hs = _ffn(hs, nw2, w_in16, w_out16, nw3, bs * ts)
    return (hp.reshape(bp, tp, D_MODEL), hs.reshape(bs, ts, D_MODEL),
            jnp.stack(rows_p_l), jnp.stack(rows_s_l), jnp.stack(sp_l), jnp.stack(ss_l),
            jnp.stack(cp_l), jnp.stack(cs_l))
```

```python
import functools
import math

import jax
import jax.numpy as jnp
from jax import lax
from jax.experimental import pallas as pl
from jax.experimental.pallas import tpu as pltpu

F32 = jnp.float32
BF16 = jnp.bfloat16

D_MODEL = 1024
PAGE_SIZE = 128
N_MIXERS = 2

MLA_HEADS = 8
QK_NOPE = 128
QK_ROPE = 64
V_HEAD = 128
KV_LORA = 256
Q_LORA = 384
MLA_ROW = KV_LORA + QK_ROPE
MLA_SCALE = (QK_NOPE + QK_ROPE) ** -0.5
ROPE_THETA = 10000.0

DN_HEADS = 8
DN_DK = 128
DN_DV = 128
CONV_W = 4
DN_HK = DN_HEADS * DN_DK
DN_QKV = DN_HEADS * (2 * DN_DK + DN_DV)
DN_Z = DN_HEADS * DN_DV

RMS_EPS = 1e-6
L2_EPS = 1e-6

LANES = 128
SUBLANES = 8
VMEM_LIMIT_BYTES = 56 * 1024 * 1024

TM_TOKENS = 512
TM_DN_PROJ = 512
TQ_ATTN = 256
DN_CHUNK = 64
DN_CHUNK_SAMPLE = 16
GDN_GROUP = 4
GDN_CHUNKS_PER_STEP = 4
GDN_SEQS_PER_STEP = 4
DEC_PAGES_PER_STEP = 64
FFN_CHUNK = 256
CONV_CARRY_ROWS = SUBLANES


def _params(*sem):
    return pltpu.CompilerParams(dimension_semantics=sem, vmem_limit_bytes=VMEM_LIMIT_BYTES)


def _fixed_spec(block_shape, block_index):
    return pl.BlockSpec(block_shape, lambda *_: block_index, pipeline_mode=pl.Buffered(1))


def _const_spec(shape):
    return _fixed_spec(shape, (0,) * len(shape))


def _dot(a, b):
    return jnp.dot(a, b, preferred_element_type=F32)


def _dot_nt(a, b):
    return lax.dot_general(a, b, (((1,), (1,)), ((), ())), preferred_element_type=F32)


def _dot_tn(a, b):
    return lax.dot_general(a, b, (((0,), (0,)), ((), ())), preferred_element_type=F32)


def _split2(x):
    hi = x.astype(BF16)
    lo = (x - hi.astype(F32)).astype(BF16)
    return hi, lo


def _split3(x):
    hi = x.astype(BF16)
    r = x - hi.astype(F32)
    mid = r.astype(BF16)
    lo = (r - mid.astype(F32)).astype(BF16)
    return hi, mid, lo


def _dot_hl(a, b):
    ah, al = _split2(a)
    bh, bl = _split2(b)
    return _dot(ah, bh) + (_dot(ah, bl) + _dot(al, bh))


def _rms(x, w):
    return x * lax.rsqrt(jnp.mean(x * x, axis=-1, keepdims=True) + RMS_EPS) * w


def _sigmoid(x):
    return 1.0 / (1.0 + jnp.exp(-x))


def _silu(x):
    return x * _sigmoid(x)


def _softplus(x):
    return jnp.maximum(x, 0.0) + jnp.log1p(jnp.exp(-jnp.abs(x)))


def _rope(x, cs, sn):
    half = x.shape[-1] // 2
    swapped = jnp.concatenate([x[:, half:], x[:, :half]], axis=1)
    return x * cs + swapped * sn


def _rope_tables(pos):
    half = QK_ROPE // 2
    freq = ROPE_THETA ** (-jnp.arange(half, dtype=F32) / half)
    ang = pos.astype(F32)[:, None] * freq[None, :]
    cos, sin = jnp.cos(ang), jnp.sin(ang)
    return jnp.concatenate([cos, cos], axis=1), jnp.concatenate([-sin, sin], axis=1)


def _mla_proj_body(x_ref, nw_ref, win_ref, gq_ref, gkv_ref, wuq_ref, wukt_ref, cs_ref, sn_ref,
                   rows_ref, kv_ref, q_ref):
    u = _rms(x_ref[...], nw_ref[...]).astype(BF16)
    a = _dot(u, win_ref[...])
    c_q = _rms(a[:, :Q_LORA], gq_ref[...]).astype(BF16)
    c_kv = _rms(a[:, Q_LORA:Q_LORA + KV_LORA], gkv_ref[...])
    cs = cs_ref[...]
    sn = sn_ref[...]
    k_r = _rope(a[:, Q_LORA + KV_LORA:], cs, sn)
    rows_ref[:, :KV_LORA] = c_kv
    rows_ref[:, KV_LORA:] = k_r
    kv_ref[:, :KV_LORA] = c_kv.astype(BF16)
    kv_ref[:, KV_LORA:] = k_r.astype(BF16)
    q = _dot(c_q, wuq_ref[...])
    rope_base = MLA_HEADS * QK_NOPE
    for h in range(MLA_HEADS):
        q_lat = _dot(q[:, h * QK_NOPE:(h + 1) * QK_NOPE].astype(BF16), wukt_ref[h])
        q_rope = _rope(q[:, rope_base + h * QK_ROPE:rope_base + (h + 1) * QK_ROPE], cs, sn)
        q_ref[h, :, :KV_LORA] = (q_lat * MLA_SCALE).astype(BF16)
        q_ref[h, :, KV_LORA:] = (q_rope * MLA_SCALE).astype(BF16)


def _mla_project(x, nw, w_in, g_q, g_kv, w_uq, w_ukt, cs, sn, tm):
    m = x.shape[0]
    period_tiles = cs.shape[0] // tm
    assert cs.shape[0] % tm == 0 and m % cs.shape[0] == 0
    row = lambda i: (i, 0)
    pos_row = lambda i: (i % period_tiles, 0)
    return pl.pallas_call(
        _mla_proj_body,
        grid=(m // tm,),
        in_specs=[
            pl.BlockSpec((tm, D_MODEL), row),
            _const_spec((1, D_MODEL)),
            _const_spec(w_in.shape),
            _const_spec((1, Q_LORA)),
            _const_spec((1, KV_LORA)),
            _const_spec(w_uq.shape),
            _const_spec(w_ukt.shape),
            pl.BlockSpec((tm, QK_ROPE), pos_row),
            pl.BlockSpec((tm, QK_ROPE), pos_row),
        ],
        out_specs=[
            pl.BlockSpec((tm, MLA_ROW), row),
            pl.BlockSpec((tm, MLA_ROW), row),
            pl.BlockSpec((MLA_HEADS, tm, MLA_ROW), lambda i: (0, i, 0)),
        ],
        out_shape=[
            jax.ShapeDtypeStruct((m, MLA_ROW), F32),
            jax.ShapeDtypeStruct((m, MLA_ROW), BF16),
            jax.ShapeDtypeStruct((MLA_HEADS, m, MLA_ROW), BF16),
        ],
        compiler_params=_params("parallel"),
        name="mla_project",
    )(x, nw, w_in, g_q, g_kv, w_uq, w_ukt, cs, sn)


def _softmax_update(s, v, m_ref, l_ref, acc_ref, v_transposed=False):
    m_prev = m_ref[...]
    m_new = jnp.maximum(m_prev, jnp.max(s, axis=1, keepdims=True))
    alpha = jnp.exp(m_prev - m_new)
    p = jnp.exp(s - m_new)
    p16 = p.astype(BF16)
    pv = _dot_nt(p16, v) if v_transposed else _dot(p16, v)
    l_ref[...] = alpha * l_ref[...] + jnp.sum(p, axis=1, keepdims=True)
    acc_ref[...] = alpha * acc_ref[...] + pv
    m_ref[...] = m_new


def _softmax_init(m_ref, l_ref, acc_ref):
    m_ref[...] = jnp.full(m_ref.shape, -jnp.inf, F32)
    l_ref[...] = jnp.zeros(l_ref.shape, F32)
    acc_ref[...] = jnp.zeros(acc_ref.shape, F32)


def _attn_prompt_body(q_ref, kv_ref, kvt_ref, o_ref, m_ref, l_ref, acc_ref, sa_ref, sb_ref, *, tq):
    i = pl.program_id(1)
    heads = q_ref.shape[0]
    cols = heads * tq
    q = q_ref[...].reshape(cols, MLA_ROW)
    _softmax_init(m_ref, l_ref, acc_ref)

    def scores(j, buf):
        k = kv_ref[pl.ds(pl.multiple_of(j * tq, tq), tq), :]
        buf[...] = _dot_nt(k, q)

    def absorb(j, buf, masked):
        st = buf[...]
        if masked:
            key = lax.broadcasted_iota(jnp.int32, (tq, cols), 0)
            tok = lax.broadcasted_iota(jnp.int32, (tq, cols), 1) % tq
            st = jnp.where(key <= tok, st, -jnp.inf)
        m_prev = m_ref[...]
        m_new = jnp.maximum(m_prev, jnp.max(st, axis=0, keepdims=True))
        alpha = jnp.exp(m_prev - m_new)
        p = jnp.exp(st - m_new)
        l_ref[...] = alpha * l_ref[...] + jnp.sum(p, axis=0, keepdims=True)
        acc_ref[...] = alpha * acc_ref[...] + _dot(kvt_ref[j], p.astype(BF16))
        m_ref[...] = m_new

    scores(0, sa_ref)

    def body(p, carry):
        scores(2 * p + 1, sb_ref)
        absorb(2 * p, sa_ref, False)
        scores(2 * p + 2, sa_ref)
        absorb(2 * p + 1, sb_ref, False)
        return carry

    lax.fori_loop(0, i // 2, body, 0)

    @pl.when(i % 2 == 0)
    def _():
        absorb(i, sa_ref, True)

    @pl.when(i % 2 == 1)
    def _():
        scores(i, sb_ref)
        absorb(i - 1, sa_ref, False)
        absorb(i, sb_ref, True)

    o = (acc_ref[...] / l_ref[...]).T
    o_ref[...] = o.reshape(heads, tq, KV_LORA).astype(BF16)


def _attn_prompt(q, kv, batch, seq, tq):
    heads, m, _ = q.shape
    nq = seq // tq
    kvt = kv[:, :KV_LORA].reshape(batch * nq, tq, KV_LORA).transpose(0, 2, 1)
    return pl.pallas_call(
        functools.partial(_attn_prompt_body, tq=tq),
        grid=(batch, nq),
        in_specs=[
            pl.BlockSpec((heads, tq, MLA_ROW), lambda b, i: (0, b * nq + i, 0)),
            pl.BlockSpec((seq, MLA_ROW), lambda b, i: (b, 0)),
            pl.BlockSpec((nq, KV_LORA, tq), lambda b, i: (b, 0, 0)),
        ],
        out_specs=pl.BlockSpec((heads, tq, KV_LORA), lambda b, i: (0, b * nq + i, 0)),
        out_shape=jax.ShapeDtypeStruct((heads, m, KV_LORA), BF16),
        scratch_shapes=[
            pltpu.VMEM((1, heads * tq), F32),
            pltpu.VMEM((1, heads * tq), F32),
            pltpu.VMEM((KV_LORA, heads * tq), F32),
            pltpu.VMEM((tq, heads * tq), F32),
            pltpu.VMEM((tq, heads * tq), F32),
        ],
        compiler_params=_params("parallel", "parallel"),
        name="mla_attn_prompt",
    )(q, kv, kvt)


def _attn_decode_body(pt_ref, q_ref, kn_ref, *rest, pages, ts):
    del pt_ref
    cache_refs = rest[:pages]
    o_ref, m_ref, l_ref, acc_ref = rest[pages:]
    c = pl.program_id(1)

    @pl.when(c == 0)
    def _():
        _softmax_init(m_ref, l_ref, acc_ref)

    q = q_ref[0]
    kt = jnp.concatenate([r[0, 0].astype(BF16) for r in cache_refs], axis=1)
    _softmax_update(_dot(q, kt), kt[:KV_LORA, :], m_ref, l_ref, acc_ref, v_transposed=True)

    @pl.when(c == pl.num_programs(1) - 1)
    def _():
        kn = kn_ref[0]
        s = _dot_nt(q, kn)
        tok = lax.broadcasted_iota(jnp.int32, s.shape, 0) % ts
        key = lax.broadcasted_iota(jnp.int32, s.shape, 1)
        s = jnp.where(key <= tok, s, -jnp.inf)
        _softmax_update(s, kn[:, :KV_LORA], m_ref, l_ref, acc_ref)
        o_ref[0] = (acc_ref[...] / l_ref[...]).astype(BF16)


def _attn_decode(q, k_new, cache_t, page_table, layer_slot, ts):
    bs, rows, _ = q.shape
    n_pages = page_table.shape[1]
    pages = DEC_PAGES_PER_STEP
    assert n_pages % pages == 0
    cache = cache_t
    cache_specs = [
        pl.BlockSpec((1, 1, MLA_ROW, PAGE_SIZE),
                     lambda b, c, pt, p=p: (layer_slot, pt[b, c * pages + p], 0, 0))
        for p in range(pages)
    ]
    grid_spec = pltpu.PrefetchScalarGridSpec(
        num_scalar_prefetch=1,
        grid=(bs, n_pages // pages),
        in_specs=[
            pl.BlockSpec((1, rows, MLA_ROW), lambda b, c, pt: (b, 0, 0)),
            pl.BlockSpec((1, k_new.shape[1], MLA_ROW), lambda b, c, pt: (b, 0, 0)),
        ] + cache_specs,
        out_specs=pl.BlockSpec((1, rows, KV_LORA), lambda b, c, pt: (b, 0, 0)),
        scratch_shapes=[
            pltpu.VMEM((rows, 1), F32),
            pltpu.VMEM((rows, 1), F32),
            pltpu.VMEM((rows, KV_LORA), F32),
        ],
    )
    return pl.pallas_call(
        functools.partial(_attn_decode_body, pages=pages, ts=ts),
        grid_spec=grid_spec,
        out_shape=jax.ShapeDtypeStruct((bs, rows, KV_LORA), BF16),
        compiler_params=_params("parallel", "arbitrary"),
        name="mla_attn_decode",
    )(page_table, q, k_new, *([cache] * pages))


def _mla_out_body(o_ref, wuv_ref, wo_ref, h_ref, nw_ref, out_ref):
    vs = [_dot(o_ref[h], wuv_ref[h]).astype(BF16) for h in range(MLA_HEADS)]
    mix = _dot(jnp.concatenate(vs, axis=1), wo_ref[...])
    out_ref[...] = h_ref[...] + _rms(mix, nw_ref[...])


def _mla_out(o_lat, w_uv, w_o, h, nw, tm):
    m = h.shape[0]
    row = lambda i: (i, 0)
    return pl.pallas_call(
        _mla_out_body,
        grid=(m // tm,),
        in_specs=[
            pl.BlockSpec((MLA_HEADS, tm, KV_LORA), lambda i: (0, i, 0)),
            _const_spec(w_uv.shape),
            _const_spec(w_o.shape),
            pl.BlockSpec((tm, D_MODEL), row),
            _const_spec((1, D_MODEL)),
        ],
        out_specs=pl.BlockSpec((tm, D_MODEL), row),
        out_shape=jax.ShapeDtypeStruct((m, D_MODEL), F32),
        compiler_params=_params("parallel"),
        name="mla_out",
    )(o_lat, w_uv, w_o, h, nw)


def _ffn_body(h_ref, nw_in_ref, win_ref, wout_ref, nw_out_ref, out_ref, *, d_ff, chunk):
    x = h_ref[...]
    u = _rms(x, nw_in_ref[...]).astype(BF16)
    acc = jnp.zeros(x.shape, F32)
    for c in range(d_ff // chunk):
        lo = c * chunk
        gate = _dot(u, win_ref[0, :, lo:lo + chunk])
        up = _dot(u, win_ref[0, :, d_ff + lo:d_ff + lo + chunk])
        act = (_silu(gate) * up).astype(BF16)
        acc = acc + _dot(act, wout_ref[0, lo:lo + chunk, :])
    out_ref[...] = x + _rms(acc, nw_out_ref[...])


def _ffn(h, nw_in, w_in, w_out, layer, nw_out, tm):
    m = h.shape[0]
    d_ff = w_out.shape[1]
    assert d_ff % FFN_CHUNK == 0
    row = lambda i: (i, 0)
    return pl.pallas_call(
        functools.partial(_ffn_body, d_ff=d_ff, chunk=FFN_CHUNK),
        grid=(m // tm,),
        in_specs=[
            pl.BlockSpec((tm, D_MODEL), row),
            _const_spec((1, D_MODEL)),
            _fixed_spec((1,) + w_in.shape[1:], (layer, 0, 0)),
            _fixed_spec((1,) + w_out.shape[1:], (layer, 0, 0)),
            _const_spec((1, D_MODEL)),
        ],
        out_specs=pl.BlockSpec((tm, D_MODEL), row),
        out_shape=jax.ShapeDtypeStruct((m, D_MODEL), F32),
        compiler_params=_params("parallel"),
        name="ffn",
    )(h, nw_in, w_in, w_out, nw_out)


def _dn_proj_body(*refs, tm, seq_len, has_hist, tail_rows):
    if has_hist:
        (x_ref, nw_ref, wqkv_ref, wz_ref, wba_ref, cw_ref, alog_ref, dtb_ref, hist_ref,
         qkv_ref, z_ref, bg_ref, tail_ref, ext_ref) = refs
    else:
        (x_ref, nw_ref, wqkv_ref, wz_ref, wba_ref, cw_ref, alog_ref, dtb_ref,
         qkv_ref, z_ref, bg_ref, tail_ref, ext_ref) = refs
        hist_ref = None
    i = pl.program_id(0)
    carry = CONV_CARRY_ROWS

    @pl.when((i * tm) % seq_len == 0)
    def _():
        ext_ref[0:carry, :] = jnp.zeros((carry, DN_QKV), F32)

    u = _rms(x_ref[...], nw_ref[...]).astype(BF16)
    raw = _dot(u, wqkv_ref[...])
    ext_ref[carry:carry + tm, :] = raw
    cw = cw_ref[...]
    acc = raw * cw[CONV_W - 1:CONV_W, :]
    for k in range(1, CONV_W):
        shifted = ext_ref[carry - k:carry - k + tm, :]
        if has_hist:
            tpos = lax.broadcasted_iota(jnp.int32, (tm, 1), 0) % seq_len
            shifted = jnp.where(tpos >= k, shifted, hist_ref[k - 1])
        acc = acc + shifted * cw[CONV_W - 1 - k:CONV_W - k, :]
    ext_ref[0:carry, :] = raw[tm - carry:, :]
    tail_ref[0] = raw[tm - tail_rows:, :]
    act = _silu(acc)
    for h in range(DN_HEADS):
        qh = act[:, h * DN_DK:(h + 1) * DN_DK]
        kh = act[:, DN_HK + h * DN_DK:DN_HK + (h + 1) * DN_DK]
        qn = qh * lax.rsqrt(jnp.sum(qh * qh, axis=-1, keepdims=True) + L2_EPS) * (DN_DK ** -0.5)
        kn = kh * lax.rsqrt(jnp.sum(kh * kh, axis=-1, keepdims=True) + L2_EPS)
        qkv_ref[:, h * DN_DK:(h + 1) * DN_DK] = qn
        qkv_ref[:, DN_HK + h * DN_DK:DN_HK + (h + 1) * DN_DK] = kn
    qkv_ref[:, 2 * DN_HK:] = act[:, 2 * DN_HK:]
    z_ref[...] = _dot(u, wz_ref[...])
    ba = _dot(u, wba_ref[...])
    beta = _sigmoid(ba[:, :DN_HEADS])
    g = -jnp.exp(alog_ref[...]) * _softplus(ba[:, DN_HEADS:2 * DN_HEADS] + dtb_ref[...])
    bg_ref[:, :DN_HEADS] = beta
    bg_ref[:, DN_HEADS:] = g


def _dn_project(x, nw, w_all, w_ba, conv_w, a_log, dt_bias, hist, tm, seq_len, tail_rows):
    assert DN_QKV % DN_Z == 0
    m = x.shape[0]
    n_tiles = m // tm
    has_hist = hist is not None
    assert (tm % seq_len == 0 and n_tiles == 1) if has_hist else seq_len % tm == 0
    tiles_per_tail = max(seq_len // tm, 1)
    row = lambda i: (i, 0)
    in_specs = [
        pl.BlockSpec((tm, D_MODEL), row),
        _const_spec((1, D_MODEL)),
        _fixed_spec((D_MODEL, DN_QKV), (0, 0)),
        _fixed_spec((D_MODEL, DN_Z), (0, DN_QKV // DN_Z)),
        _const_spec(w_ba.shape),
        _const_spec(conv_w.shape),
        _const_spec((1, DN_HEADS)),
        _const_spec((1, DN_HEADS)),
    ]
    args = [x, nw, w_all, w_all, w_ba, conv_w, a_log, dt_bias]
    if has_hist:
        in_specs.append(pl.BlockSpec((CONV_W - 1, tm, DN_QKV), lambda i: (0, i, 0)))
        args.append(hist)
    return pl.pallas_call(
        functools.partial(_dn_proj_body, tm=tm, seq_len=seq_len, has_hist=has_hist,
                          tail_rows=tail_rows),
        grid=(n_tiles,),
        in_specs=in_specs,
        out_specs=[
            pl.BlockSpec((tm, DN_QKV), row),
            pl.BlockSpec((tm, DN_Z), row),
            pl.BlockSpec((tm, 2 * DN_HEADS), row),
            pl.BlockSpec((1, tail_rows, DN_QKV), lambda i: (i // tiles_per_tail, 0, 0)),
        ],
        out_shape=[
            jax.ShapeDtypeStruct((m, DN_QKV), F32),
            jax.ShapeDtypeStruct((m, DN_Z), F32),
            jax.ShapeDtypeStruct((m, 2 * DN_HEADS), F32),
            jax.ShapeDtypeStruct((n_tiles // tiles_per_tail, tail_rows, DN_QKV), F32),
        ],
        scratch_shapes=[pltpu.VMEM((CONV_CARRY_ROWS + tm, DN_QKV), F32)],
        compiler_params=_params("arbitrary"),
        name="dn_project",
    )(*args)


def _stack_heads(load, g):
    return jnp.concatenate(
        [load(slice((g * GDN_GROUP + hh) * DN_DK, (g * GDN_GROUP + hh + 1) * DN_DK))
         for hh in range(GDN_GROUP)], axis=0)


def _gdn_local_body(q_ref, k_ref, v_ref, bg_ref, bgt_ref,
                    u_ref, w_ref, qd_ref, kdt_ref, qk_ref, egl_ref, *, chunk, chunks_per_step):
    n_heads = DN_HEADS
    stack = GDN_GROUP * chunk
    ri = lax.broadcasted_iota(jnp.int32, (stack, stack), 0)
    ci = lax.broadcasted_iota(jnp.int32, (stack, stack), 1)
    same_head = (ri // chunk) == (ci // chunk)
    causal = same_head & (ri >= ci)
    strict = same_head & (ri > ci)
    r1 = lax.broadcasted_iota(jnp.int32, (chunk, chunk), 0)
    c1 = lax.broadcasted_iota(jnp.int32, (chunk, chunk), 1)
    tril = jnp.where(r1 >= c1, 1.0, 0.0).astype(BF16)
    triu = jnp.where(r1 <= c1, 1.0, 0.0).astype(BF16)
    n_double = int(math.log2(chunk)) - 1

    def store_heads(ref, rows, g, stacked):
        for hh in range(GDN_GROUP):
            h = g * GDN_GROUP + hh
            ref[rows, h * DN_DK:(h + 1) * DN_DK] = stacked[hh * chunk:(hh + 1) * chunk, :]

    chains = []
    for cp in range(chunks_per_step):
        rows = slice(cp * chunk, (cp + 1) * chunk)
        bg = bg_ref[rows, :]
        bgt = bgt_ref[cp]
        gc_col = sum(_dot(tril, part) for part in _split3(bg))
        gc_row = sum(_dot(part, triu) for part in _split3(bgt))
        egl_ref[cp] = jnp.broadcast_to(jnp.exp(gc_row[n_heads:, chunk - 1:chunk]),
                                       (n_heads, LANES))
        for g in range(n_heads // GDN_GROUP):
            heads = [g * GDN_GROUP + hh for hh in range(GDN_GROUP)]
            q = _stack_heads(lambda sl: q_ref[rows, sl], g)
            k = _stack_heads(lambda sl: k_ref[rows, sl], g)
            v = _stack_heads(lambda sl: v_ref[rows, sl], g)
            beta = jnp.concatenate([bg[:, h:h + 1] for h in heads], axis=0)
            gcc = jnp.concatenate([gc_col[:, n_heads + h:n_heads + h + 1] for h in heads], axis=0)
            gcr = jnp.concatenate([gc_row[n_heads + h:n_heads + h + 1, :] for h in heads], axis=1)
            g_last = jnp.concatenate(
                [jnp.broadcast_to(gc_col[chunk - 1:chunk, n_heads + h:n_heads + h + 1], (chunk, 1))
                 for h in heads], axis=0)
            eg = jnp.exp(gcc)
            kb = k * beta
            store_heads(qd_ref, rows, g, (q * eg).astype(BF16))
            kdt_ref[cp, :, g * stack:(g + 1) * stack] = (k * jnp.exp(g_last - gcc)).T.astype(BF16)
            chains.append(dict(
                rows=rows, g=g, q16=q.astype(BF16), k16=k.astype(BF16), kb16=kb.astype(BF16),
                decay=jnp.where(causal, jnp.exp(gcc - gcr), 0.0),
                rhs=jnp.concatenate([v * beta, kb * eg], axis=1)))

    for ch in chains:
        lower = jnp.where(strict, _dot_nt(ch["kb16"], ch["k16"]) * ch["decay"], 0.0)
        ch.update(lower=lower, power=lower, off=-lower)
    for ch in chains:
        qk = jnp.where(causal, _dot_nt(ch["q16"], ch["k16"]) * ch["decay"], 0.0)
        qk_packed = sum(qk[hh * chunk:(hh + 1) * chunk, :] for hh in range(GDN_GROUP))
        qk_ref[ch["rows"], ch["g"] * stack:(ch["g"] + 1) * stack] = qk_packed.astype(BF16)
    for _ in range(n_double):
        for ch in chains:
            p16 = ch["power"].astype(BF16)
            ch["power"] = _dot(p16, p16)
        for ch in chains:
            ch["off"] = ch["off"] + ch["power"] + _dot(ch["off"].astype(BF16),
                                                       ch["power"].astype(BF16))
    for ch in chains:
        ch["resid"] = (ch["lower"] + ch["off"]) + _dot_hl(ch["lower"], ch["off"])
    for ch in chains:
        ch["off"] = ch["off"] - ch["resid"] - _dot(ch["off"].astype(BF16),
                                                   ch["resid"].astype(BF16))
    for ch in chains:
        sol = ch["rhs"] + _dot(ch["off"].astype(BF16), ch["rhs"].astype(BF16))
        store_heads(u_ref, ch["rows"], ch["g"], sol[:, :DN_DV])
        store_heads(w_ref, ch["rows"], ch["g"], sol[:, DN_DV:].astype(BF16))


def _gdn_scan_body(u_ref, w_ref, qd_ref, kdt_ref, qk_ref, egl_ref, z_ref, gout_ref, s0_ref,
                   y_ref, s_ref, *, chunk, seqs):
    c = pl.program_id(1)

    @pl.when(c == 0)
    def _():
        s_ref[...] = s0_ref[...]

    stack = GDN_GROUP * chunk
    pair = 2 * DN_DK
    lhs_mask = ((lax.broadcasted_iota(jnp.int32, (4 * chunk, pair), 0) // chunk) % 2
                == lax.broadcasted_iota(jnp.int32, (4 * chunk, pair), 1) // DN_DK)
    qk_mask = (lax.broadcasted_iota(jnp.int32, (stack, stack), 0) // chunk
               == lax.broadcasted_iota(jnp.int32, (stack, stack), 1) // chunk)
    kd_mask = (lax.broadcasted_iota(jnp.int32, (GDN_GROUP * DN_DK, stack), 0) // DN_DK
               == lax.broadcasted_iota(jnp.int32, (GDN_GROUP * DN_DK, stack), 1) // chunk)
    zero16 = jnp.zeros((), BF16)
    chains = [dict(sq=sq, g=g) for sq in range(seqs) for g in range(DN_HEADS // GDN_GROUP)]
    for ch in chains:
        sq, g = ch["sq"], ch["g"]
        s = s_ref[sq, g * GDN_GROUP:(g + 1) * GDN_GROUP].reshape(GDN_GROUP * DN_DK, DN_DV)
        s16 = s.astype(BF16)
        ws, qs = [], []
        for p in range(GDN_GROUP // 2):
            cols = [slice((g * GDN_GROUP + 2 * p + e) * DN_DK, (g * GDN_GROUP + 2 * p + e + 1) * DN_DK)
                    for e in range(2)]
            lhs = jnp.concatenate([w_ref[sq, :, cols[0]], w_ref[sq, :, cols[1]],
                                   qd_ref[sq, :, cols[0]], qd_ref[sq, :, cols[1]]], axis=0)
            lhs_bd = jnp.where(lhs_mask, jnp.concatenate([lhs, lhs], axis=1), zero16)
            res = _dot(lhs_bd, s16[p * pair:(p + 1) * pair, :])
            ws.append(res[:2 * chunk])
            qs.append(res[2 * chunk:])
        ch.update(s=s, ws=jnp.concatenate(ws, axis=0), qs=jnp.concatenate(qs, axis=0))
    for ch in chains:
        sq, g = ch["sq"], ch["g"]
        v_new = _stack_heads(lambda sl: u_ref[sq, :, sl], g) - ch["ws"]
        ch["v16"] = v_new.astype(BF16)
    gout = gout_ref[...]
    for ch in chains:
        sq, g = ch["sq"], ch["g"]
        qk = qk_ref[sq, :, g * stack:(g + 1) * stack]
        qk_bd = jnp.where(qk_mask, jnp.concatenate([qk] * GDN_GROUP, axis=0), zero16)
        o = ch["qs"] + _dot(qk_bd, ch["v16"])
        y = _rms(o, gout) * _silu(_stack_heads(lambda sl: z_ref[sq, :, sl], g))
        y16 = y.astype(BF16)
        for hh in range(GDN_GROUP):
            h = g * GDN_GROUP + hh
            y_ref[sq, :, h * DN_DV:(h + 1) * DN_DV] = y16[hh * chunk:(hh + 1) * chunk, :]
    for ch in chains:
        sq, g = ch["sq"], ch["g"]
        kdt = kdt_ref[sq, 0, :, g * stack:(g + 1) * stack]
        kdt_bd = jnp.where(kd_mask, jnp.concatenate([kdt] * GDN_GROUP, axis=0), zero16)
        gate = jnp.concatenate(
            [jnp.broadcast_to(egl_ref[sq, 0, g * GDN_GROUP + hh:g * GDN_GROUP + hh + 1, :],
                              (DN_DK, DN_DV)) for hh in range(GDN_GROUP)], axis=0)
        s_new = ch["s"] * gate + _dot(kdt_bd, ch["v16"])
        s_ref[sq, g * GDN_GROUP:(g + 1) * GDN_GROUP] = s_new.reshape(GDN_GROUP, DN_DK, DN_DV)


def _gdn(qkv, bg, z, g_out, s0, n_seq, seq_len, chunk):
    m = qkv.shape[0]
    n_chunks = m // chunk
    nc = seq_len // chunk
    cps = GDN_CHUNKS_PER_STEP
    sps = GDN_SEQS_PER_STEP
    assert n_chunks % cps == 0 and n_seq % sps == 0
    bgt = bg.reshape(n_chunks, chunk, 2 * DN_HEADS).transpose(0, 2, 1)
    half = DN_HEADS * chunk
    row = lambda i: (i, 0)
    blk = lambda col: pl.BlockSpec((cps * chunk, DN_HK), lambda i, col=col: (i, col))
    u, w16, qd16, kdt16, qk16, egl = pl.pallas_call(
        functools.partial(_gdn_local_body, chunk=chunk, chunks_per_step=cps),
        grid=(n_chunks // cps,),
        in_specs=[
            blk(0), blk(1), blk(2),
            pl.BlockSpec((cps * chunk, 2 * DN_HEADS), row),
            pl.BlockSpec((cps, 2 * DN_HEADS, chunk), lambda i: (i, 0, 0)),
        ],
        out_specs=[
            pl.BlockSpec((cps * chunk, DN_Z), row),
            pl.BlockSpec((cps * chunk, DN_HK), row),
            pl.BlockSpec((cps * chunk, DN_HK), row),
            pl.BlockSpec((cps, DN_DK, half), lambda i: (i, 0, 0)),
            pl.BlockSpec((cps * chunk, half), row),
            pl.BlockSpec((cps, DN_HEADS, LANES), lambda i: (i, 0, 0)),
        ],
        out_shape=[
            jax.ShapeDtypeStruct((m, DN_Z), F32),
            jax.ShapeDtypeStruct((m, DN_HK), BF16),
            jax.ShapeDtypeStruct((m, DN_HK), BF16),
            jax.ShapeDtypeStruct((n_chunks, DN_DK, half), BF16),
            jax.ShapeDtypeStruct((m, half), BF16),
            jax.ShapeDtypeStruct((n_chunks, DN_HEADS, LANES), F32),
        ],
        compiler_params=_params("parallel"),
        name="gdn_local",
    )(qkv, qkv, qkv, bg, bgt)

    per_seq = lambda a: a.reshape((n_seq, nc if a.ndim == 3 else seq_len) + a.shape[1:])
    tok_spec = lambda width: pl.BlockSpec((sps, chunk, width), lambda b, c: (b, c, 0))
    lead_spec = lambda d1, d2: pl.BlockSpec((sps, 1, d1, d2), lambda b, c: (b, c, 0, 0))
    state_spec = pl.BlockSpec((sps, DN_HEADS, DN_DK, DN_DV), lambda b, c: (b, 0, 0, 0))
    y16, s = pl.pallas_call(
        functools.partial(_gdn_scan_body, chunk=chunk, seqs=sps),
        grid=(n_seq // sps, nc),
        in_specs=[
            tok_spec(DN_Z),
            tok_spec(DN_HK),
            tok_spec(DN_HK),
            lead_spec(DN_DK, half),
            tok_spec(half),
            lead_spec(DN_HEADS, LANES),
            tok_spec(DN_Z),
            pl.BlockSpec((1, DN_DV), lambda b, c: (0, 0)),
            state_spec,
        ],
        out_specs=[tok_spec(DN_Z), state_spec],
        out_shape=[
            jax.ShapeDtypeStruct((n_seq, seq_len, DN_Z), BF16),
            jax.ShapeDtypeStruct(s0.shape, F32),
        ],
        compiler_params=_params("parallel", "arbitrary"),
        name="gdn_scan",
    )(per_seq(u), per_seq(w16), per_seq(qd16), per_seq(kdt16), per_seq(qk16), per_seq(egl),
      per_seq(z), g_out, s0)
    return y16.reshape(m, DN_Z), s


def _dn_out_body(y_ref, wo_ref, h_ref, nw_ref, out_ref):
    mix = _dot(y_ref[...], wo_ref[...])
    out_ref[...] = h_ref[...] + _rms(mix, nw_ref[...])


def _dn_out(y16, w_o, h, nw, tm):
    m = h.shape[0]
    row = lambda i: (i, 0)
    return pl.pallas_call(
        _dn_out_body,
        grid=(m // tm,),
        in_specs=[
            pl.BlockSpec((tm, DN_Z), row),
            _const_spec(w_o.shape),
            pl.BlockSpec((tm, D_MODEL), row),
            _const_spec((1, D_MODEL)),
        ],
        out_specs=pl.BlockSpec((tm, D_MODEL), row),
        out_shape=jax.ShapeDtypeStruct((m, D_MODEL), F32),
        compiler_params=_params("parallel"),
        name="dn_out",
    )(y16, w_o, h, nw)


def _mla_layer(hp, hs, cache_mla, page_table, slot, nw, w_in, g_q, g_kv, w_uq, w_uk, w_uv, w_o,
               bp, tp, bs, ts):
    past = page_table.shape[1] * PAGE_SIZE
    w_in16 = w_in.astype(BF16)
    w_uq_heads = w_uq.reshape(Q_LORA, MLA_HEADS, QK_NOPE + QK_ROPE)
    w_uq16 = jnp.concatenate(
        [w_uq_heads[:, :, :QK_NOPE].reshape(Q_LORA, MLA_HEADS * QK_NOPE),
         w_uq_heads[:, :, QK_NOPE:].reshape(Q_LORA, MLA_HEADS * QK_ROPE)], axis=1).astype(BF16)
    w_ukt16 = jnp.swapaxes(w_uk, 1, 2).astype(BF16)
    w_uv16 = w_uv.astype(BF16)
    w_o16 = w_o.astype(BF16)
    g_q = g_q.reshape(1, Q_LORA)
    g_kv = g_kv.reshape(1, KV_LORA)
    nw0 = nw[0].reshape(1, D_MODEL)
    nw1 = nw[1].reshape(1, D_MODEL)

    cs_p, sn_p = _rope_tables(jnp.arange(tp))
    cs_s, sn_s = _rope_tables(past + jnp.tile(jnp.arange(ts), bs))
    ms = bs * ts
    rows_p, kv_p, q_p = _mla_project(hp, nw0, w_in16, g_q, g_kv, w_uq16, w_ukt16, cs_p, sn_p,
                                     TM_TOKENS)
    rows_s, kv_s, q_s = _mla_project(hs, nw0, w_in16, g_q, g_kv, w_uq16, w_ukt16, cs_s, sn_s, ms)

    o_p = _attn_prompt(q_p, kv_p, bp, tp, TQ_ATTN)

    q_sb = q_s.reshape(MLA_HEADS, bs, ts, MLA_ROW).transpose(1, 0, 2, 3).reshape(
        bs, MLA_HEADS * ts, MLA_ROW)
    new_rows = 2 * SUBLANES
    k_new = jnp.pad(kv_s.reshape(bs, ts, MLA_ROW), ((0, 0), (0, new_rows - ts), (0, 0)))
    o_sb = _attn_decode(q_sb, k_new, jnp.swapaxes(cache_mla, 2, 3), page_table, slot, ts)
    o_s = o_sb.reshape(bs, MLA_HEADS, ts, KV_LORA).transpose(1, 0, 2, 3).reshape(
        MLA_HEADS, ms, KV_LORA)

    hp = _mla_out(o_p, w_uv16, w_o16, hp, nw1, TM_TOKENS)
    hs = _mla_out(o_s, w_uv16, w_o16, hs, nw1, ms)
    return hp, hs, rows_p.reshape(bp, tp, MLA_ROW), rows_s.reshape(bs, ts, MLA_ROW)


def _dn_layer(hp, hs, s0_s, conv0_s, nw, w_in, conv_w, a_log, dt_bias, g_out, w_o, bp, tp, bs, ts):
    w_in16 = w_in.astype(BF16)
    w_ba16 = jnp.pad(w_in[:, DN_QKV + DN_Z:], ((0, 0), (0, LANES - 2 * DN_HEADS))).astype(BF16)
    w_o16 = w_o.astype(BF16)
    a_log = a_log.reshape(1, DN_HEADS).astype(F32)
    dt_bias = dt_bias.reshape(1, DN_HEADS).astype(F32)
    g_out = g_out.reshape(1, DN_DV)
    nw0 = nw[0].reshape(1, D_MODEL)
    nw1 = nw[1].reshape(1, D_MODEL)
    hist_rows = CONV_W - 1
    ms = bs * ts

    qkv_p, z_p, bg_p, tail_p = _dn_project(hp, nw0, w_in16, w_ba16, conv_w, a_log, dt_bias,
                                           None, TM_DN_PROJ, tp, SUBLANES)
    conv_p = tail_p[:, SUBLANES - hist_rows:, :]
    s0_p = jnp.zeros((bp, DN_HEADS, DN_DK, DN_DV), F32)
    y_p, s_p = _gdn(qkv_p, bg_p, z_p, g_out, s0_p, bp, tp, DN_CHUNK)
    hp = _dn_out(y_p, w_o16, hp, nw1, TM_TOKENS)

    tok = jnp.arange(ts)
    hist = jnp.stack([
        conv0_s[:, jnp.clip(hist_rows - k + tok, 0, hist_rows - 1), :].reshape(ms, DN_QKV)
        for k in range(1, CONV_W)
    ]).astype(F32)
    qkv_s, z_s, bg_s, tail_s = _dn_project(hs, nw0, w_in16, w_ba16, conv_w, a_log, dt_bias,
                                           hist, ms, ts, ms)
    raw_s = tail_s.reshape(bs, ts, DN_QKV)
    conv_s = jnp.concatenate([conv0_s.astype(F32), raw_s], axis=1)[:, ts:, :]
    cs = DN_CHUNK_SAMPLE
    assert ts <= cs
    pad = lambda a: jnp.pad(a.reshape(bs, ts, -1), ((0, 0), (0, cs - ts), (0, 0))).reshape(
        bs * cs, -1)
    y_s_pad, s_s = _gdn(pad(qkv_s), pad(bg_s), pad(z_s), g_out, s0_s.astype(F32), bs, cs, cs)
    y_s = y_s_pad.reshape(bs, cs, DN_Z)[:, :ts, :].reshape(ms, DN_Z)
    hs = _dn_out(y_s, w_o16, hs, nw1, ms)
    return hp, hs, s_p, s_s, conv_p, conv_s


def kernel(x_prompt, x_sample, cache_mla, state_dn, state_dn_conv, page_table, norm_w, mla_w_in,
           mla_g_q, mla_g_kv, mla_w_uq, mla_w_uk, mla_w_uv, mla_w_o, dn_w_in, dn_conv_w, dn_a_log,
           dn_dt_bias, dn_g_out, dn_w_o, ffn_w_in, ffn_w_out):
    bp, tp, _ = x_prompt.shape
    bs, ts, _ = x_sample.shape
    depth = norm_w.shape[0]
    hp = x_prompt.reshape(bp * tp, D_MODEL)
    hs = x_sample.reshape(bs * ts, D_MODEL)
    rows_p_l, rows_s_l, sp_l, ss_l, cp_l, cs_l = [], [], [], [], [], []
    ffn_w_in16 = ffn_w_in.astype(BF16)
    ffn_w_out16 = ffn_w_out.astype(BF16)
    for layer in range(depth):
        j = layer // N_MIXERS
        nw = norm_w[layer]
        if layer % N_MIXERS == 0:
            hp, hs, rows_p, rows_s = _mla_layer(
                hp, hs, cache_mla, page_table, j, nw, mla_w_in[j], mla_g_q[j], mla_g_kv[j],
                mla_w_uq[j], mla_w_uk[j], mla_w_uv[j], mla_w_o[j], bp, tp, bs, ts)
            rows_p_l.append(rows_p)
            rows_s_l.append(rows_s)
        else:
            hp, hs, s_p, s_s, c_p, c_s = _dn_layer(
                hp, hs, state_dn[j], state_dn_conv[j], nw, dn_w_in[j], dn_conv_w[j], dn_a_log[j],
                dn_dt_bias[j], dn_g_out[j], dn_w_o[j], bp, tp, bs, ts)
            sp_l.append(s_p.astype(state_dn.dtype))
            ss_l.append(s_s.astype(state_dn.dtype))
            cp_l.append(c_p.astype(state_dn_conv.dtype))
            cs_l.append(c_s.astype(state_dn_conv.dtype))
        nw2 = nw[2].reshape(1, D_MODEL)
        nw3 = nw[3].reshape(1, D_MODEL)
        hp = _ffn(hp, nw2, ffn_w_in16, ffn_w_out16, layer, nw3, TM_TOKENS)
        hs = _ffn(hs, nw2, ffn_w_in16, ffn_w_out16, layer, nw3, bs * ts)
    return (hp.reshape(bp, tp, D_MODEL), hs.reshape(bs, ts, D_MODEL),
            jnp.stack(rows_p_l), jnp.stack(rows_s_l), jnp.stack(sp_l), jnp.stack(ss_l),
            jnp.stack(cp_l), jnp.stack(cs_l))
```

```python
import functools
import math

import jax
import jax.numpy as jnp
from jax import lax
from jax.experimental import pallas as pl
from jax.experimental.pallas import tpu as pltpu

F32 = jnp.float32
BF16 = jnp.bfloat16

D_MODEL = 1024
PAGE_SIZE = 128
N_MIXERS = 2

MLA_HEADS = 8
QK_NOPE = 128
QK_ROPE = 64
V_HEAD = 128
KV_LORA = 256
Q_LORA = 384
MLA_ROW = KV_LORA + QK_ROPE
MLA_SCALE = (QK_NOPE + QK_ROPE) ** -0.5
ROPE_THETA = 10000.0

DN_HEADS = 8
DN_DK = 128
DN_DV = 128
CONV_W = 4
DN_HK = DN_HEADS * DN_DK
DN_QKV = DN_HEADS * (2 * DN_DK + DN_DV)
DN_Z = DN_HEADS * DN_DV

RMS_EPS = 1e-6
L2_EPS = 1e-6

LANES = 128
SUBLANES = 8
VMEM_LIMIT_BYTES = 56 * 1024 * 1024

TM_TOKENS = 512
TM_DN_PROJ = 512
TQ_ATTN = 256
DN_CHUNK = 64
DN_CHUNK_SAMPLE = 16
GDN_GROUP = 4
GDN_CHUNKS_PER_STEP = 4
GDN_SEQS_PER_STEP = 4
DEC_PAGES_PER_STEP = 64
DEC_PAGES_PER_GROUP = 8
FFN_CHUNK = 256
CONV_CARRY_ROWS = SUBLANES


def _params(*sem):
    return pltpu.CompilerParams(dimension_semantics=sem, vmem_limit_bytes=VMEM_LIMIT_BYTES)


def _fixed_spec(block_shape, block_index):
    return pl.BlockSpec(block_shape, lambda *_: block_index, pipeline_mode=pl.Buffered(1))


def _const_spec(shape):
    return _fixed_spec(shape, (0,) * len(shape))


def _dot(a, b):
    return jnp.dot(a, b, preferred_element_type=F32)


def _dot_nt(a, b):
    return lax.dot_general(a, b, (((1,), (1,)), ((), ())), preferred_element_type=F32)


def _dot_tn(a, b):
    return lax.dot_general(a, b, (((0,), (0,)), ((), ())), preferred_element_type=F32)


def _split2(x):
    hi = x.astype(BF16)
    lo = (x - hi.astype(F32)).astype(BF16)
    return hi, lo


def _split3(x):
    hi = x.astype(BF16)
    r = x - hi.astype(F32)
    mid = r.astype(BF16)
    lo = (r - mid.astype(F32)).astype(BF16)
    return hi, mid, lo


def _dot_hl(a, b):
    ah, al = _split2(a)
    bh, bl = _split2(b)
    return _dot(ah, bh) + (_dot(ah, bl) + _dot(al, bh))


def _rms(x, w):
    return x * lax.rsqrt(jnp.mean(x * x, axis=-1, keepdims=True) + RMS_EPS) * w


def _sigmoid(x):
    return 1.0 / (1.0 + jnp.exp(-x))


def _silu(x):
    return x * _sigmoid(x)


def _softplus(x):
    return jnp.maximum(x, 0.0) + jnp.log1p(jnp.exp(-jnp.abs(x)))


def _rope(x, cs, sn):
    half = x.shape[-1] // 2
    swapped = jnp.concatenate([x[:, half:], x[:, :half]], axis=1)
    return x * cs + swapped * sn


def _rope_tables(pos):
    half = QK_ROPE // 2
    freq = ROPE_THETA ** (-jnp.arange(half, dtype=F32) / half)
    ang = pos.astype(F32)[:, None] * freq[None, :]
    cos, sin = jnp.cos(ang), jnp.sin(ang)
    return jnp.concatenate([cos, cos], axis=1), jnp.concatenate([-sin, sin], axis=1)


def _mla_proj_body(x_ref, nw_ref, win_ref, gq_ref, gkv_ref, wuq_ref, wukt_ref, cs_ref, sn_ref,
                   rows_ref, kv_ref, q_ref):
    u = _rms(x_ref[...], nw_ref[...]).astype(BF16)
    a = _dot(u, win_ref[...])
    c_q = _rms(a[:, :Q_LORA], gq_ref[...]).astype(BF16)
    c_kv = _rms(a[:, Q_LORA:Q_LORA + KV_LORA], gkv_ref[...])
    cs = cs_ref[...]
    sn = sn_ref[...]
    k_r = _rope(a[:, Q_LORA + KV_LORA:], cs, sn)
    rows_ref[:, :KV_LORA] = c_kv
    rows_ref[:, KV_LORA:] = k_r
    kv_ref[:, :KV_LORA] = c_kv.astype(BF16)
    kv_ref[:, KV_LORA:] = k_r.astype(BF16)
    q = _dot(c_q, wuq_ref[...])
    rope_base = MLA_HEADS * QK_NOPE
    for h in range(MLA_HEADS):
        q_lat = _dot(q[:, h * QK_NOPE:(h + 1) * QK_NOPE].astype(BF16), wukt_ref[h])
        q_rope = _rope(q[:, rope_base + h * QK_ROPE:rope_base + (h + 1) * QK_ROPE], cs, sn)
        q_ref[h, :, :KV_LORA] = (q_lat * MLA_SCALE).astype(BF16)
        q_ref[h, :, KV_LORA:] = (q_rope * MLA_SCALE).astype(BF16)


def _mla_project(x, nw, w_in, g_q, g_kv, w_uq, w_ukt, cs, sn, tm):
    m = x.shape[0]
    period_tiles = cs.shape[0] // tm
    assert cs.shape[0] % tm == 0 and m % cs.shape[0] == 0
    row = lambda i: (i, 0)
    pos_row = lambda i: (i % period_tiles, 0)
    return pl.pallas_call(
        _mla_proj_body,
        grid=(m // tm,),
        in_specs=[
            pl.BlockSpec((tm, D_MODEL), row),
            _const_spec((1, D_MODEL)),
            _const_spec(w_in.shape),
            _const_spec((1, Q_LORA)),
            _const_spec((1, KV_LORA)),
            _const_spec(w_uq.shape),
            _const_spec(w_ukt.shape),
            pl.BlockSpec((tm, QK_ROPE), pos_row),
            pl.BlockSpec((tm, QK_ROPE), pos_row),
        ],
        out_specs=[
            pl.BlockSpec((tm, MLA_ROW), row),
            pl.BlockSpec((tm, MLA_ROW), row),
            pl.BlockSpec((MLA_HEADS, tm, MLA_ROW), lambda i: (0, i, 0)),
        ],
        out_shape=[
            jax.ShapeDtypeStruct((m, MLA_ROW), F32),
            jax.ShapeDtypeStruct((m, MLA_ROW), BF16),
            jax.ShapeDtypeStruct((MLA_HEADS, m, MLA_ROW), BF16),
        ],
        compiler_params=_params("parallel"),
        name="mla_project",
    )(x, nw, w_in, g_q, g_kv, w_uq, w_ukt, cs, sn)


def _softmax_update(s, v, m_ref, l_ref, acc_ref, v_transposed=False):
    m_prev = m_ref[...]
    m_new = jnp.maximum(m_prev, jnp.max(s, axis=1, keepdims=True))
    alpha = jnp.exp(m_prev - m_new)
    p = jnp.exp(s - m_new)
    p16 = p.astype(BF16)
    pv = _dot_nt(p16, v) if v_transposed else _dot(p16, v)
    l_ref[...] = alpha * l_ref[...] + jnp.sum(p, axis=1, keepdims=True)
    acc_ref[...] = alpha * acc_ref[...] + pv
    m_ref[...] = m_new


def _softmax_init(m_ref, l_ref, acc_ref):
    m_ref[...] = jnp.full(m_ref.shape, -jnp.inf, F32)
    l_ref[...] = jnp.zeros(l_ref.shape, F32)
    acc_ref[...] = jnp.zeros(acc_ref.shape, F32)


def _attn_prompt_body(q_ref, kv_ref, kvt_ref, o_ref, m_ref, l_ref, acc_ref, sa_ref, sb_ref, *, tq):
    i = pl.program_id(1)
    heads = q_ref.shape[0]
    cols = heads * tq
    q = q_ref[...].reshape(cols, MLA_ROW)
    _softmax_init(m_ref, l_ref, acc_ref)

    def scores(j, buf):
        k = kv_ref[pl.ds(pl.multiple_of(j * tq, tq), tq), :]
        buf[...] = _dot_nt(k, q)

    def absorb(j, buf, masked):
        st = buf[...]
        if masked:
            key = lax.broadcasted_iota(jnp.int32, (tq, cols), 0)
            tok = lax.broadcasted_iota(jnp.int32, (tq, cols), 1) % tq
            st = jnp.where(key <= tok, st, -jnp.inf)
        m_prev = m_ref[...]
        m_new = jnp.maximum(m_prev, jnp.max(st, axis=0, keepdims=True))
        alpha = jnp.exp(m_prev - m_new)
        p = jnp.exp(st - m_new)
        l_ref[...] = alpha * l_ref[...] + jnp.sum(p, axis=0, keepdims=True)
        acc_ref[...] = alpha * acc_ref[...] + _dot(kvt_ref[j], p.astype(BF16))
        m_ref[...] = m_new

    scores(0, sa_ref)

    def body(p, carry):
        scores(2 * p + 1, sb_ref)
        absorb(2 * p, sa_ref, False)
        scores(2 * p + 2, sa_ref)
        absorb(2 * p + 1, sb_ref, False)
        return carry

    lax.fori_loop(0, i // 2, body, 0)

    @pl.when(i % 2 == 0)
    def _():
        absorb(i, sa_ref, True)

    @pl.when(i % 2 == 1)
    def _():
        scores(i, sb_ref)
        absorb(i - 1, sa_ref, False)
        absorb(i, sb_ref, True)

    o = (acc_ref[...] / l_ref[...]).T
    o_ref[...] = o.reshape(heads, tq, KV_LORA).astype(BF16)


def _attn_prompt(q, kv, batch, seq, tq):
    heads, m, _ = q.shape
    nq = seq // tq
    kvt = kv[:, :KV_LORA].reshape(batch * nq, tq, KV_LORA).transpose(0, 2, 1)
    return pl.pallas_call(
        functools.partial(_attn_prompt_body, tq=tq),
        grid=(batch, nq),
        in_specs=[
            pl.BlockSpec((heads, tq, MLA_ROW), lambda b, i: (0, b * nq + i, 0)),
            pl.BlockSpec((seq, MLA_ROW), lambda b, i: (b, 0)),
            pl.BlockSpec((nq, KV_LORA, tq), lambda b, i: (b, 0, 0)),
        ],
        out_specs=pl.BlockSpec((heads, tq, KV_LORA), lambda b, i: (0, b * nq + i, 0)),
        out_shape=jax.ShapeDtypeStruct((heads, m, KV_LORA), BF16),
        scratch_shapes=[
            pltpu.VMEM((1, heads * tq), F32),
            pltpu.VMEM((1, heads * tq), F32),
            pltpu.VMEM((KV_LORA, heads * tq), F32),
            pltpu.VMEM((tq, heads * tq), F32),
            pltpu.VMEM((tq, heads * tq), F32),
        ],
        compiler_params=_params("parallel", "parallel"),
        name="mla_attn_prompt",
    )(q, kv, kvt)


def _attn_decode_body(pt_ref, q_ref, kn_ref, cache_ref, o_ref, kbuf_ref, sem_ref,
                      m_ref, l_ref, acc_ref, *, pages, ts, layer_slot):
    b = pl.program_id(0)
    c = pl.program_id(1)
    n_chunks = pl.num_programs(1)
    step = b * n_chunks + c
    slot = step % 2

    def page_copy(page, p, sl):
        return pltpu.make_async_copy(cache_ref.at[layer_slot, page], kbuf_ref.at[sl, p],
                                     sem_ref.at[sl])

    def start_pages(bb, cc, sl):
        def body(p, carry):
            page_copy(pt_ref[bb, cc * pages + p], p, sl).start()
            return carry
        lax.fori_loop(0, pages, body, 0)

    @pl.when(step == 0)
    def _():
        start_pages(0, 0, 0)

    @pl.when(step + 1 < pl.num_programs(0) * n_chunks)
    def _():
        nxt = step + 1
        start_pages(nxt // n_chunks, nxt % n_chunks, 1 - slot)

    def wait_body(p, carry):
        page_copy(0, p, slot).wait()
        return carry

    lax.fori_loop(0, pages, wait_body, 0)

    @pl.when(c == 0)
    def _():
        _softmax_init(m_ref, l_ref, acc_ref)

    q = q_ref[0]
    group = DEC_PAGES_PER_GROUP
    kts, scores = [], []
    for g in range(pages // group):
        kt = jnp.concatenate([kbuf_ref[slot, p].astype(BF16)
                              for p in range(g * group, (g + 1) * group)],
                             axis=1)
        kts.append(kt)
        scores.append(_dot(q, kt))
    s = jnp.concatenate(scores, axis=1)
    m_prev = m_ref[...]
    m_new = jnp.maximum(m_prev, jnp.max(s, axis=1, keepdims=True))
    alpha = jnp.exp(m_prev - m_new)
    p = jnp.exp(s - m_new)
    l_ref[...] = alpha * l_ref[...] + jnp.sum(p, axis=1, keepdims=True)
    p16 = p.astype(BF16)
    width = group * PAGE_SIZE
    pv = sum(_dot_nt(p16[:, g * width:(g + 1) * width], kt[:KV_LORA, :])
             for g, kt in enumerate(kts))
    acc_ref[...] = alpha * acc_ref[...] + pv
    m_ref[...] = m_new

    @pl.when(c == pl.num_programs(1) - 1)
    def _():
        kn = kn_ref[0]
        s = _dot_nt(q, kn)
        tok = lax.broadcasted_iota(jnp.int32, s.shape, 0) % ts
        key = lax.broadcasted_iota(jnp.int32, s.shape, 1)
        s = jnp.where(key <= tok, s, -jnp.inf)
        _softmax_update(s, kn[:, :KV_LORA], m_ref, l_ref, acc_ref)
        o_ref[0] = (acc_ref[...] / l_ref[...]).astype(BF16)


def _attn_decode(q, k_new, cache_t, page_table, layer_slot, ts):
    bs, rows, _ = q.shape
    n_pages = page_table.shape[1]
    pages = DEC_PAGES_PER_STEP
    assert n_pages % pages == 0
    grid_spec = pltpu.PrefetchScalarGridSpec(
        num_scalar_prefetch=1,
        grid=(bs, n_pages // pages),
        in_specs=[
            pl.BlockSpec((1, rows, MLA_ROW), lambda b, c, pt: (b, 0, 0)),
            pl.BlockSpec((1, k_new.shape[1], MLA_ROW), lambda b, c, pt: (b, 0, 0)),
            pl.BlockSpec(memory_space=pl.ANY),
        ],
        out_specs=pl.BlockSpec((1, rows, KV_LORA), lambda b, c, pt: (b, 0, 0)),
        scratch_shapes=[
            pltpu.VMEM((2, pages, MLA_ROW, PAGE_SIZE), cache_t.dtype),
            pltpu.SemaphoreType.DMA((2,)),
            pltpu.VMEM((rows, 1), F32),
            pltpu.VMEM((rows, 1), F32),
            pltpu.VMEM((rows, KV_LORA), F32),
        ],
    )
    return pl.pallas_call(
        functools.partial(_attn_decode_body, pages=pages, ts=ts, layer_slot=layer_slot),
        grid_spec=grid_spec,
        out_shape=jax.ShapeDtypeStruct((bs, rows, KV_LORA), BF16),
        compiler_params=_params("arbitrary", "arbitrary"),
        name="mla_attn_decode",
    )(page_table, q, k_new, cache_t)


def _layer_tail_body(*refs, from_latent, d_ff, chunk):
    if from_latent:
        (a_ref, wuv_ref, wo_ref, h_ref, nw_mix_ref, nw_in_ref, win_ref, wout_ref, nw_out_ref,
         out_ref) = refs
        a16 = jnp.concatenate(
            [_dot(a_ref[h], wuv_ref[h]).astype(BF16) for h in range(MLA_HEADS)], axis=1)
    else:
        (a_ref, wo_ref, h_ref, nw_mix_ref, nw_in_ref, win_ref, wout_ref, nw_out_ref,
         out_ref) = refs
        a16 = a_ref[...]
    x = h_ref[...] + _rms(_dot(a16, wo_ref[...]), nw_mix_ref[...])
    u = _rms(x, nw_in_ref[...]).astype(BF16)
    acc = jnp.zeros(x.shape, F32)
    for c in range(d_ff // chunk):
        lo = c * chunk
        gate = _dot(u, win_ref[0, :, lo:lo + chunk])
        up = _dot(u, win_ref[0, :, d_ff + lo:d_ff + lo + chunk])
        act = (_silu(gate) * up).astype(BF16)
        acc = acc + _dot(act, wout_ref[0, lo:lo + chunk, :])
    out_ref[...] = x + _rms(acc, nw_out_ref[...])


def _layer_tail(a16, w_uv, w_o, h, nw_mix, ffn, tm):
    nw_in, w_in, w_out, layer, nw_out = ffn
    m = h.shape[0]
    d_ff = w_out.shape[1]
    assert d_ff % FFN_CHUNK == 0
    from_latent = w_uv is not None
    row = lambda i: (i, 0)
    if from_latent:
        in_specs = [pl.BlockSpec((MLA_HEADS, tm, KV_LORA), lambda i: (0, i, 0)),
                    _const_spec(w_uv.shape)]
        args = [a16, w_uv]
    else:
        in_specs = [pl.BlockSpec((tm, a16.shape[1]), row)]
        args = [a16]
    in_specs += [
        _const_spec(w_o.shape),
        pl.BlockSpec((tm, D_MODEL), row),
        _const_spec((1, D_MODEL)),
        _const_spec((1, D_MODEL)),
        _fixed_spec((1,) + w_in.shape[1:], (layer, 0, 0)),
        _fixed_spec((1,) + w_out.shape[1:], (layer, 0, 0)),
        _const_spec((1, D_MODEL)),
    ]
    args += [w_o, h, nw_mix, nw_in, w_in, w_out, nw_out]
    return pl.pallas_call(
        functools.partial(_layer_tail_body, from_latent=from_latent, d_ff=d_ff, chunk=FFN_CHUNK),
        grid=(m // tm,),
        in_specs=in_specs,
        out_specs=pl.BlockSpec((tm, D_MODEL), row),
        out_shape=jax.ShapeDtypeStruct((m, D_MODEL), F32),
        compiler_params=_params("parallel"),
        name="layer_tail",
    )(*args)


def _dn_proj_body(*refs, tm, seq_len, has_hist, tail_rows):
    if has_hist:
        (x_ref, nw_ref, wqkv_ref, wz_ref, wba_ref, cw_ref, alog_ref, dtb_ref, hist_ref,
         qkv_ref, z_ref, bg_ref, tail_ref, ext_ref) = refs
    else:
        (x_ref, nw_ref, wqkv_ref, wz_ref, wba_ref, cw_ref, alog_ref, dtb_ref,
         qkv_ref, z_ref, bg_ref, tail_ref, ext_ref) = refs
        hist_ref = None
    i = pl.program_id(0)
    carry = CONV_CARRY_ROWS

    @pl.when((i * tm) % seq_len == 0)
    def _():
        ext_ref[0:carry, :] = jnp.zeros((carry, DN_QKV), F32)

    u = _rms(x_ref[...], nw_ref[...]).astype(BF16)
    raw = _dot(u, wqkv_ref[...])
    ext_ref[carry:carry + tm, :] = raw
    cw = cw_ref[...]
    acc = raw * cw[CONV_W - 1:CONV_W, :]
    for k in range(1, CONV_W):
        shifted = ext_ref[carry - k:carry - k + tm, :]
        if has_hist:
            tpos = lax.broadcasted_iota(jnp.int32, (tm, 1), 0) % seq_len
            shifted = jnp.where(tpos >= k, shifted, hist_ref[k - 1])
        acc = acc + shifted * cw[CONV_W - 1 - k:CONV_W - k, :]
    ext_ref[0:carry, :] = raw[tm - carry:, :]
    tail_ref[0] = raw[tm - tail_rows:, :]
    act = _silu(acc)
    for h in range(DN_HEADS):
        qh = act[:, h * DN_DK:(h + 1) * DN_DK]
        kh = act[:, DN_HK + h * DN_DK:DN_HK + (h + 1) * DN_DK]
        qn = qh * lax.rsqrt(jnp.sum(qh * qh, axis=-1, keepdims=True) + L2_EPS) * (DN_DK ** -0.5)
        kn = kh * lax.rsqrt(jnp.sum(kh * kh, axis=-1, keepdims=True) + L2_EPS)
        qkv_ref[:, h * DN_DK:(h + 1) * DN_DK] = qn
        qkv_ref[:, DN_HK + h * DN_DK:DN_HK + (h + 1) * DN_DK] = kn
    qkv_ref[:, 2 * DN_HK:] = act[:, 2 * DN_HK:]
    z_ref[...] = _dot(u, wz_ref[...])
    ba = _dot(u, wba_ref[...])
    beta = _sigmoid(ba[:, :DN_HEADS])
    g = -jnp.exp(alog_ref[...]) * _softplus(ba[:, DN_HEADS:2 * DN_HEADS] + dtb_ref[...])
    bg_ref[:, :DN_HEADS] = beta
    bg_ref[:, DN_HEADS:] = g


def _dn_project(x, nw, w_all, w_ba, conv_w, a_log, dt_bias, hist, tm, seq_len, tail_rows):
    assert DN_QKV % DN_Z == 0
    m = x.shape[0]
    n_tiles = m // tm
    has_hist = hist is not None
    assert (tm % seq_len == 0 and n_tiles == 1) if has_hist else seq_len % tm == 0
    tiles_per_tail = max(seq_len // tm, 1)
    row = lambda i: (i, 0)
    in_specs = [
        pl.BlockSpec((tm, D_MODEL), row),
        _const_spec((1, D_MODEL)),
        _fixed_spec((D_MODEL, DN_QKV), (0, 0)),
        _fixed_spec((D_MODEL, DN_Z), (0, DN_QKV // DN_Z)),
        _const_spec(w_ba.shape),
        _const_spec(conv_w.shape),
        _const_spec((1, DN_HEADS)),
        _const_spec((1, DN_HEADS)),
    ]
    args = [x, nw, w_all, w_all, w_ba, conv_w, a_log, dt_bias]
    if has_hist:
        in_specs.append(pl.BlockSpec((CONV_W - 1, tm, DN_QKV), lambda i: (0, i, 0)))
        args.append(hist)
    return pl.pallas_call(
        functools.partial(_dn_proj_body, tm=tm, seq_len=seq_len, has_hist=has_hist,
                          tail_rows=tail_rows),
        grid=(n_tiles,),
        in_specs=in_specs,
        out_specs=[
            pl.BlockSpec((tm, DN_QKV), row),
            pl.BlockSpec((tm, DN_Z), row),
            pl.BlockSpec((tm, 2 * DN_HEADS), row),
            pl.BlockSpec((1, tail_rows, DN_QKV), lambda i: (i // tiles_per_tail, 0, 0)),
        ],
        out_shape=[
            jax.ShapeDtypeStruct((m, DN_QKV), F32),
            jax.ShapeDtypeStruct((m, DN_Z), F32),
            jax.ShapeDtypeStruct((m, 2 * DN_HEADS), F32),
            jax.ShapeDtypeStruct((n_tiles // tiles_per_tail, tail_rows, DN_QKV), F32),
        ],
        scratch_shapes=[pltpu.VMEM((CONV_CARRY_ROWS + tm, DN_QKV), F32)],
        compiler_params=_params("arbitrary"),
        name="dn_project",
    )(*args)


def _stack_heads(load, g):
    return jnp.concatenate(
        [load(slice((g * GDN_GROUP + hh) * DN_DK, (g * GDN_GROUP + hh + 1) * DN_DK))
         for hh in range(GDN_GROUP)], axis=0)


def _gdn_local_body(q_ref, k_ref, v_ref, bg_ref, bgt_ref,
                    u_ref, w_ref, qd_ref, kdt_ref, qk_ref, egl_ref, *, chunk, chunks_per_step):
    n_heads = DN_HEADS
    stack = GDN_GROUP * chunk
    ri = lax.broadcasted_iota(jnp.int32, (stack, stack), 0)
    ci = lax.broadcasted_iota(jnp.int32, (stack, stack), 1)
    same_head = (ri // chunk) == (ci // chunk)
    causal = same_head & (ri >= ci)
    strict = same_head & (ri > ci)
    r1 = lax.broadcasted_iota(jnp.int32, (chunk, chunk), 0)
    c1 = lax.broadcasted_iota(jnp.int32, (chunk, chunk), 1)
    tril = jnp.where(r1 >= c1, 1.0, 0.0).astype(BF16)
    triu = jnp.where(r1 <= c1, 1.0, 0.0).astype(BF16)
    n_double = int(math.log2(chunk)) - 1

    def store_heads(ref, rows, g, stacked):
        for hh in range(GDN_GROUP):
            h = g * GDN_GROUP + hh
            ref[rows, h * DN_DK:(h + 1) * DN_DK] = stacked[hh * chunk:(hh + 1) * chunk, :]

    chains = []
    for cp in range(chunks_per_step):
        rows = slice(cp * chunk, (cp + 1) * chunk)
        bg = bg_ref[rows, :]
        bgt = bgt_ref[cp]
        gc_col = sum(_dot(tril, part) for part in _split3(bg))
        gc_row = sum(_dot(part, triu) for part in _split3(bgt))
        egl_ref[cp] = jnp.broadcast_to(jnp.exp(gc_row[n_heads:, chunk - 1:chunk]),
                                       (n_heads, LANES))
        for g in range(n_heads // GDN_GROUP):
            heads = [g * GDN_GROUP + hh for hh in range(GDN_GROUP)]
            q = _stack_heads(lambda sl: q_ref[rows, sl], g)
            k = _stack_heads(lambda sl: k_ref[rows, sl], g)
            v = _stack_heads(lambda sl: v_ref[rows, sl], g)
            beta = jnp.concatenate([bg[:, h:h + 1] for h in heads], axis=0)
            gcc = jnp.concatenate([gc_col[:, n_heads + h:n_heads + h + 1] for h in heads], axis=0)
            gcr = jnp.concatenate([gc_row[n_heads + h:n_heads + h + 1, :] for h in heads], axis=1)
            g_last = jnp.concatenate(
                [jnp.broadcast_to(gc_col[chunk - 1:chunk, n_heads + h:n_heads + h + 1], (chunk, 1))
                 for h in heads], axis=0)
            eg = jnp.exp(gcc)
            kb = k * beta
            store_heads(qd_ref, rows, g, (q * eg).astype(BF16))
            kdt_ref[cp, :, g * stack:(g + 1) * stack] = (k * jnp.exp(g_last - gcc)).T.astype(BF16)
            chains.append(dict(
                rows=rows, g=g, q16=q.astype(BF16), k16=k.astype(BF16), kb16=kb.astype(BF16),
                decay=jnp.where(causal, jnp.exp(gcc - gcr), 0.0),
                rhs=jnp.concatenate([v * beta, kb * eg], axis=1)))

    for ch in chains:
        lower = jnp.where(strict, _dot_nt(ch["kb16"], ch["k16"]) * ch["decay"], 0.0)
        ch.update(lower=lower, power=lower, off=-lower)
    for ch in chains:
        qk = jnp.where(causal, _dot_nt(ch["q16"], ch["k16"]) * ch["decay"], 0.0)
        qk_packed = sum(qk[hh * chunk:(hh + 1) * chunk, :] for hh in range(GDN_GROUP))
        qk_ref[ch["rows"], ch["g"] * stack:(ch["g"] + 1) * stack] = qk_packed.astype(BF16)
    for _ in range(n_double):
        for ch in chains:
            p16 = ch["power"].astype(BF16)
            ch["power"] = _dot(p16, p16)
        for ch in chains:
            ch["off"] = ch["off"] + ch["power"] + _dot(ch["off"].astype(BF16),
                                                       ch["power"].astype(BF16))
    for ch in chains:
        ch["resid"] = (ch["lower"] + ch["off"]) + _dot_hl(ch["lower"], ch["off"])
    for ch in chains:
        ch["off"] = ch["off"] - ch["resid"] - _dot(ch["off"].astype(BF16),
                                                   ch["resid"].astype(BF16))
    for ch in chains:
        sol = ch["rhs"] + _dot(ch["off"].astype(BF16), ch["rhs"].astype(BF16))
        store_heads(u_ref, ch["rows"], ch["g"], sol[:, :DN_DV])
        store_heads(w_ref, ch["rows"], ch["g"], sol[:, DN_DV:].astype(BF16))


def _gdn_scan_body(u_ref, w_ref, qd_ref, kdt_ref, qk_ref, egl_ref, z_ref, gout_ref, s0_ref,
                   y_ref, s_ref, *, chunk, seqs):
    c = pl.program_id(1)

    @pl.when(c == 0)
    def _():
        s_ref[...] = s0_ref[...]

    stack = GDN_GROUP * chunk
    pair = 2 * DN_DK
    lhs_mask = ((lax.broadcasted_iota(jnp.int32, (4 * chunk, pair), 0) // chunk) % 2
                == lax.broadcasted_iota(jnp.int32, (4 * chunk, pair), 1) // DN_DK)
    qk_mask = (lax.broadcasted_iota(jnp.int32, (stack, stack), 0) // chunk
               == lax.broadcasted_iota(jnp.int32, (stack, stack), 1) // chunk)
    kd_mask = (lax.broadcasted_iota(jnp.int32, (GDN_GROUP * DN_DK, stack), 0) // DN_DK
               == lax.broadcasted_iota(jnp.int32, (GDN_GROUP * DN_DK, stack), 1) // chunk)
    zero16 = jnp.zeros((), BF16)
    chains = [dict(sq=sq, g=g) for sq in range(seqs) for g in range(DN_HEADS // GDN_GROUP)]
    for ch in chains:
        sq, g = ch["sq"], ch["g"]
        s = s_ref[sq, g * GDN_GROUP:(g + 1) * GDN_GROUP].reshape(GDN_GROUP * DN_DK, DN_DV)
        s16 = s.astype(BF16)
        ws, qs = [], []
        for p in range(GDN_GROUP // 2):
            cols = [slice((g * GDN_GROUP + 2 * p + e) * DN_DK, (g * GDN_GROUP + 2 * p + e + 1) * DN_DK)
                    for e in range(2)]
            lhs = jnp.concatenate([w_ref[sq, :, cols[0]], w_ref[sq, :, cols[1]],
                                   qd_ref[sq, :, cols[0]], qd_ref[sq, :, cols[1]]], axis=0)
            lhs_bd = jnp.where(lhs_mask, jnp.concatenate([lhs, lhs], axis=1), zero16)
            res = _dot(lhs_bd, s16[p * pair:(p + 1) * pair, :])
            ws.append(res[:2 * chunk])
            qs.append(res[2 * chunk:])
        ch.update(s=s, ws=jnp.concatenate(ws, axis=0), qs=jnp.concatenate(qs, axis=0))
    for ch in chains:
        sq, g = ch["sq"], ch["g"]
        v_new = _stack_heads(lambda sl: u_ref[sq, :, sl], g) - ch["ws"]
        ch["v16"] = v_new.astype(BF16)
    gout = gout_ref[...]
    for ch in chains:
        sq, g = ch["sq"], ch["g"]
        qk = qk_ref[sq, :, g * stack:(g + 1) * stack]
        qk_bd = jnp.where(qk_mask, jnp.concatenate([qk] * GDN_GROUP, axis=0), zero16)
        o = ch["qs"] + _dot(qk_bd, ch["v16"])
        y = _rms(o, gout) * _silu(_stack_heads(lambda sl: z_ref[sq, :, sl], g))
        y16 = y.astype(BF16)
        for hh in range(GDN_GROUP):
            h = g * GDN_GROUP + hh
            y_ref[sq, :, h * DN_DV:(h + 1) * DN_DV] = y16[hh * chunk:(hh + 1) * chunk, :]
    for ch in chains:
        sq, g = ch["sq"], ch["g"]
        kdt = kdt_ref[sq, 0, :, g * stack:(g + 1) * stack]
        kdt_bd = jnp.where(kd_mask, jnp.concatenate([kdt] * GDN_GROUP, axis=0), zero16)
        gate = jnp.concatenate(
            [jnp.broadcast_to(egl_ref[sq, 0, g * GDN_GROUP + hh:g * GDN_GROUP + hh + 1, :],
                              (DN_DK, DN_DV)) for hh in range(GDN_GROUP)], axis=0)
        s_new = ch["s"] * gate + _dot(kdt_bd, ch["v16"])
        s_ref[sq, g * GDN_GROUP:(g + 1) * GDN_GROUP] = s_new.reshape(GDN_GROUP, DN_DK, DN_DV)


def _gdn(qkv, bg, z, g_out, s0, n_seq, seq_len, chunk):
    m = qkv.shape[0]
    n_chunks = m // chunk
    nc = seq_len // chunk
    cps = GDN_CHUNKS_PER_STEP
    sps = GDN_SEQS_PER_STEP
    assert n_chunks % cps == 0 and n_seq % sps == 0
    bgt = bg.reshape(n_chunks, chunk, 2 * DN_HEADS).transpose(0, 2, 1)
    half = DN_HEADS * chunk
    row = lambda i: (i, 0)
    blk = lambda col: pl.BlockSpec((cps * chunk, DN_HK), lambda i, col=col: (i, col))
    u, w16, qd16, kdt16, qk16, egl = pl.pallas_call(
        functools.partial(_gdn_local_body, chunk=chunk, chunks_per_step=cps),
        grid=(n_chunks // cps,),
        in_specs=[
            blk(0), blk(1), blk(2),
            pl.BlockSpec((cps * chunk, 2 * DN_HEADS), row),
            pl.BlockSpec((cps, 2 * DN_HEADS, chunk), lambda i: (i, 0, 0)),
        ],
        out_specs=[
            pl.BlockSpec((cps * chunk, DN_Z), row),
            pl.BlockSpec((cps * chunk, DN_HK), row),
            pl.BlockSpec((cps * chunk, DN_HK), row),
            pl.BlockSpec((cps, DN_DK, half), lambda i: (i, 0, 0)),
            pl.BlockSpec((cps * chunk, half), row),
            pl.BlockSpec((cps, DN_HEADS, LANES), lambda i: (i, 0, 0)),
        ],
        out_shape=[
            jax.ShapeDtypeStruct((m, DN_Z), F32),
            jax.ShapeDtypeStruct((m, DN_HK), BF16),
            jax.ShapeDtypeStruct((m, DN_HK), BF16),
            jax.ShapeDtypeStruct((n_chunks, DN_DK, half), BF16),
            jax.ShapeDtypeStruct((m, half), BF16),
            jax.ShapeDtypeStruct((n_chunks, DN_HEADS, LANES), F32),
        ],
        compiler_params=_params("parallel"),
        name="gdn_local",
    )(qkv, qkv, qkv, bg, bgt)

    per_seq = lambda a: a.reshape((n_seq, nc if a.ndim == 3 else seq_len) + a.shape[1:])
    tok_spec = lambda width: pl.BlockSpec((sps, chunk, width), lambda b, c: (b, c, 0))
    lead_spec = lambda d1, d2: pl.BlockSpec((sps, 1, d1, d2), lambda b, c: (b, c, 0, 0))
    state_spec = pl.BlockSpec((sps, DN_HEADS, DN_DK, DN_DV), lambda b, c: (b, 0, 0, 0))
    y16, s = pl.pallas_call(
        functools.partial(_gdn_scan_body, chunk=chunk, seqs=sps),
        grid=(n_seq // sps, nc),
        in_specs=[
            tok_spec(DN_Z),
            tok_spec(DN_HK),
            tok_spec(DN_HK),
            lead_spec(DN_DK, half),
            tok_spec(half),
            lead_spec(DN_HEADS, LANES),
            tok_spec(DN_Z),
            pl.BlockSpec((1, DN_DV), lambda b, c: (0, 0)),
            state_spec,
        ],
        out_specs=[tok_spec(DN_Z), state_spec],
        out_shape=[
            jax.ShapeDtypeStruct((n_seq, seq_len, DN_Z), BF16),
            jax.ShapeDtypeStruct(s0.shape, F32),
        ],
        compiler_params=_params("parallel", "arbitrary"),
        name="gdn_scan",
    )(per_seq(u), per_seq(w16), per_seq(qd16), per_seq(kdt16), per_seq(qk16), per_seq(egl),
      per_seq(z), g_out, s0)
    return y16.reshape(m, DN_Z), s


def _mla_layer(hp, hs, cache_mla, page_table, slot, nw, w_in, g_q, g_kv, w_uq, w_uk, w_uv, w_o,
               ffn, bp, tp, bs, ts):
    past = page_table.shape[1] * PAGE_SIZE
    w_in16 = w_in.astype(BF16)
    w_uq_heads = w_uq.reshape(Q_LORA, MLA_HEADS, QK_NOPE + QK_ROPE)
    w_uq16 = jnp.concatenate(
        [w_uq_heads[:, :, :QK_NOPE].reshape(Q_LORA, MLA_HEADS * QK_NOPE),
         w_uq_heads[:, :, QK_NOPE:].reshape(Q_LORA, MLA_HEADS * QK_ROPE)], axis=1).astype(BF16)
    w_ukt16 = jnp.swapaxes(w_uk, 1, 2).astype(BF16)
    w_uv16 = w_uv.astype(BF16)
    w_o16 = w_o.astype(BF16)
    g_q = g_q.reshape(1, Q_LORA)
    g_kv = g_kv.reshape(1, KV_LORA)
    nw0 = nw[0].reshape(1, D_MODEL)
    nw1 = nw[1].reshape(1, D_MODEL)

    cs_p, sn_p = _rope_tables(jnp.arange(tp))
    cs_s, sn_s = _rope_tables(past + jnp.tile(jnp.arange(ts), bs))
    ms = bs * ts
    rows_p, kv_p, q_p = _mla_project(hp, nw0, w_in16, g_q, g_kv, w_uq16, w_ukt16, cs_p, sn_p,
                                     TM_TOKENS)
    rows_s, kv_s, q_s = _mla_project(hs, nw0, w_in16, g_q, g_kv, w_uq16, w_ukt16, cs_s, sn_s, ms)

    o_p = _attn_prompt(q_p, kv_p, bp, tp, TQ_ATTN)

    q_sb = q_s.reshape(MLA_HEADS, bs, ts, MLA_ROW).transpose(1, 0, 2, 3).reshape(
        bs, MLA_HEADS * ts, MLA_ROW)
    new_rows = 2 * SUBLANES
    k_new = jnp.pad(kv_s.reshape(bs, ts, MLA_ROW), ((0, 0), (0, new_rows - ts), (0, 0)))
    o_sb = _attn_decode(q_sb, k_new, jnp.swapaxes(cache_mla, 2, 3), page_table, slot, ts)
    o_s = o_sb.reshape(bs, MLA_HEADS, ts, KV_LORA).transpose(1, 0, 2, 3).reshape(
        MLA_HEADS, ms, KV_LORA)

    hp = _layer_tail(o_p, w_uv16, w_o16, hp, nw1, ffn, TM_TOKENS)
    hs = _layer_tail(o_s, w_uv16, w_o16, hs, nw1, ffn, ms)
    return hp, hs, rows_p.reshape(bp, tp, MLA_ROW), rows_s.reshape(bs, ts, MLA_ROW)


def _dn_layer(hp, hs, s0_s, conv0_s, nw, w_in, conv_w, a_log, dt_bias, g_out, w_o, ffn,
              bp, tp, bs, ts):
    w_in16 = w_in.astype(BF16)
    w_ba16 = jnp.pad(w_in[:, DN_QKV + DN_Z:], ((0, 0), (0, LANES - 2 * DN_HEADS))).astype(BF16)
    w_o16 = w_o.astype(BF16)
    a_log = a_log.reshape(1, DN_HEADS).astype(F32)
    dt_bias = dt_bias.reshape(1, DN_HEADS).astype(F32)
    g_out = g_out.reshape(1, DN_DV)
    nw0 = nw[0].reshape(1, D_MODEL)
    nw1 = nw[1].reshape(1, D_MODEL)
    hist_rows = CONV_W - 1
    ms = bs * ts

    qkv_p, z_p, bg_p, tail_p = _dn_project(hp, nw0, w_in16, w_ba16, conv_w, a_log, dt_bias,
                                           None, TM_DN_PROJ, tp, SUBLANES)
    conv_p = tail_p[:, SUBLANES - hist_rows:, :]
    s0_p = jnp.zeros((bp, DN_HEADS, DN_DK, DN_DV), F32)
    y_p, s_p = _gdn(qkv_p, bg_p, z_p, g_out, s0_p, bp, tp, DN_CHUNK)
    hp = _layer_tail(y_p, None, w_o16, hp, nw1, ffn, TM_TOKENS)

    tok = jnp.arange(ts)
    hist = jnp.stack([
        conv0_s[:, jnp.clip(hist_rows - k + tok, 0, hist_rows - 1), :].reshape(ms, DN_QKV)
        for k in range(1, CONV_W)
    ]).astype(F32)
    qkv_s, z_s, bg_s, tail_s = _dn_project(hs, nw0, w_in16, w_ba16, conv_w, a_log, dt_bias,
                                           hist, ms, ts, ms)
    raw_s = tail_s.reshape(bs, ts, DN_QKV)
    conv_s = jnp.concatenate([conv0_s.astype(F32), raw_s], axis=1)[:, ts:, :]
    cs = DN_CHUNK_SAMPLE
    assert ts <= cs
    pad = lambda a: jnp.pad(a.reshape(bs, ts, -1), ((0, 0), (0, cs - ts), (0, 0))).reshape(
        bs * cs, -1)
    y_s_pad, s_s = _gdn(pad(qkv_s), pad(bg_s), pad(z_s), g_out, s0_s.astype(F32), bs, cs, cs)
    y_s = y_s_pad.reshape(bs, cs, DN_Z)[:, :ts, :].reshape(ms, DN_Z)
    hs = _layer_tail(y_s, None, w_o16, hs, nw1, ffn, ms)
    return hp, hs, s_p, s_s, conv_p, conv_s


def kernel(x_prompt, x_sample, cache_mla, state_dn, state_dn_conv, page_table, norm_w, mla_w_in,
           mla_g_q, mla_g_kv, mla_w_uq, mla_w_uk, mla_w_uv, mla_w_o, dn_w_in, dn_conv_w, dn_a_log,
           dn_dt_bias, dn_g_out, dn_w_o, ffn_w_in, ffn_w_out):
    bp, tp, _ = x_prompt.shape
    bs, ts, _ = x_sample.shape
    depth = norm_w.shape[0]
    hp = x_prompt.reshape(bp * tp, D_MODEL)
    hs = x_sample.reshape(bs * ts, D_MODEL)
    rows_p_l, rows_s_l, sp_l, ss_l, cp_l, cs_l = [], [], [], [], [], []
    ffn_w_in16 = ffn_w_in.astype(BF16)
    ffn_w_out16 = ffn_w_out.astype(BF16)
    for layer in range(depth):
        j = layer // N_MIXERS
        nw = norm_w[layer]
        ffn = (nw[2].reshape(1, D_MODEL), ffn_w_in16, ffn_w_out16, layer, nw[3].reshape(1, D_MODEL))
        if layer % N_MIXERS == 0:
            hp, hs, rows_p, rows_s = _mla_layer(
                hp, hs, cache_mla, page_table, j, nw, mla_w_in[j], mla_g_q[j], mla_g_kv[j],
                mla_w_uq[j], mla_w_uk[j], mla_w_uv[j], mla_w_o[j], ffn, bp, tp, bs, ts)
            rows_p_l.append(rows_p)
            rows_s_l.append(rows_s)
        else:
            hp, hs, s_p, s_s, c_p, c_s = _dn_layer(
                hp, hs, state_dn[j], state_dn_conv[j], nw, dn_w_in[j], dn_conv_w[j], dn_a_log[j],
                dn_dt_bias[j], dn_g_out[j], dn_w_o[j], ffn, bp, tp, bs, ts)
            sp_l.append(s_p.astype(state_dn.dtype))
            ss_l.append(s_s.astype(state_dn.dtype))
            cp_l.append(c_p.astype(state_dn_conv.dtype))
            cs_l.append(c_s.astype(state_dn_conv.dtype))
    return (hp.reshape(bp, tp, D_MODEL), hs.reshape(bs, ts, D_MODEL),
            jnp.stack(rows_p_l), jnp.stack(rows_s_l), jnp.stack(sp_l), jnp.stack(ss_l),
            jnp.stack(cp_l), jnp.stack(cs_l))
```

```python
import functools
import math

import jax
import jax.numpy as jnp
from jax import lax
from jax.experimental import pallas as pl
from jax.experimental.pallas import tpu as pltpu

F32 = jnp.float32
BF16 = jnp.bfloat16

D_MODEL = 1024
PAGE_SIZE = 128
N_MIXERS = 2

MLA_HEADS = 8
QK_NOPE = 128
QK_ROPE = 64
V_HEAD = 128
KV_LORA = 256
Q_LORA = 384
MLA_ROW = KV_LORA + QK_ROPE
MLA_SCALE = (QK_NOPE + QK_ROPE) ** -0.5
ROPE_THETA = 10000.0

DN_HEADS = 8
DN_DK = 128
DN_DV = 128
CONV_W = 4
DN_HK = DN_HEADS * DN_DK
DN_QKV = DN_HEADS * (2 * DN_DK + DN_DV)
DN_Z = DN_HEADS * DN_DV

RMS_EPS = 1e-6
L2_EPS = 1e-6

LANES = 128
SUBLANES = 8
VMEM_LIMIT_BYTES = 56 * 1024 * 1024

TM_TOKENS = 512
TM_DN_PROJ = 512
TQ_ATTN = 256
DN_CHUNK = 64
DN_CHUNK_SAMPLE = 16
GDN_GROUP = 4
GDN_CHUNKS_PER_STEP = 4
GDN_SEQS_PER_STEP = 4
DEC_PAGES_PER_STEP = 64
DEC_PAGES_PER_GROUP = 8
FFN_CHUNK = 256
CONV_CARRY_ROWS = SUBLANES


def _params(*sem):
    return pltpu.CompilerParams(dimension_semantics=sem, vmem_limit_bytes=VMEM_LIMIT_BYTES)


def _fixed_spec(block_shape, block_index):
    return pl.BlockSpec(block_shape, lambda *_: block_index, pipeline_mode=pl.Buffered(1))


def _const_spec(shape):
    return _fixed_spec(shape, (0,) * len(shape))


def _dot(a, b):
    return jnp.dot(a, b, preferred_element_type=F32)


def _dot_nt(a, b):
    return lax.dot_general(a, b, (((1,), (1,)), ((), ())), preferred_element_type=F32)


def _dot_tn(a, b):
    return lax.dot_general(a, b, (((0,), (0,)), ((), ())), preferred_element_type=F32)


def _split2(x):
    hi = x.astype(BF16)
    lo = (x - hi.astype(F32)).astype(BF16)
    return hi, lo


def _split3(x):
    hi = x.astype(BF16)
    r = x - hi.astype(F32)
    mid = r.astype(BF16)
    lo = (r - mid.astype(F32)).astype(BF16)
    return hi, mid, lo


def _dot_hl(a, b):
    ah, al = _split2(a)
    bh, bl = _split2(b)
    return _dot(ah, bh) + (_dot(ah, bl) + _dot(al, bh))


def _rms(x, w):
    return x * lax.rsqrt(jnp.mean(x * x, axis=-1, keepdims=True) + RMS_EPS) * w


def _sigmoid(x):
    return 1.0 / (1.0 + jnp.exp(-x))


def _silu(x):
    return x * _sigmoid(x)


def _softplus(x):
    return jnp.maximum(x, 0.0) + jnp.log1p(jnp.exp(-jnp.abs(x)))


def _rope(x, cs, sn):
    half = x.shape[-1] // 2
    swapped = jnp.concatenate([x[:, half:], x[:, :half]], axis=1)
    return x * cs + swapped * sn


def _rope_tables(pos):
    half = QK_ROPE // 2
    freq = ROPE_THETA ** (-jnp.arange(half, dtype=F32) / half)
    ang = pos.astype(F32)[:, None] * freq[None, :]
    cos, sin = jnp.cos(ang), jnp.sin(ang)
    return jnp.concatenate([cos, cos], axis=1), jnp.concatenate([-sin, sin], axis=1)


def _mla_proj_body(x_ref, nw_ref, win_ref, gq_ref, gkv_ref, wuq_ref, wukt_ref, cs_ref, sn_ref,
                   rows_ref, kv_ref, q_ref):
    u = _rms(x_ref[...], nw_ref[...]).astype(BF16)
    a = _dot(u, win_ref[...])
    c_q = _rms(a[:, :Q_LORA], gq_ref[...]).astype(BF16)
    c_kv = _rms(a[:, Q_LORA:Q_LORA + KV_LORA], gkv_ref[...])
    cs = cs_ref[...]
    sn = sn_ref[...]
    k_r = _rope(a[:, Q_LORA + KV_LORA:], cs, sn)
    rows_ref[:, :KV_LORA] = c_kv
    rows_ref[:, KV_LORA:] = k_r
    kv_ref[:, :KV_LORA] = c_kv.astype(BF16)
    kv_ref[:, KV_LORA:] = k_r.astype(BF16)
    q = _dot(c_q, wuq_ref[...])
    rope_base = MLA_HEADS * QK_NOPE
    for h in range(MLA_HEADS):
        q_lat = _dot(q[:, h * QK_NOPE:(h + 1) * QK_NOPE].astype(BF16), wukt_ref[h])
        q_rope = _rope(q[:, rope_base + h * QK_ROPE:rope_base + (h + 1) * QK_ROPE], cs, sn)
        q_ref[h, :, :KV_LORA] = (q_lat * MLA_SCALE).astype(BF16)
        q_ref[h, :, KV_LORA:] = (q_rope * MLA_SCALE).astype(BF16)


def _mla_project(x, nw, w_in, g_q, g_kv, w_uq, w_ukt, cs, sn, tm):
    m = x.shape[0]
    period_tiles = cs.shape[0] // tm
    assert cs.shape[0] % tm == 0 and m % cs.shape[0] == 0
    row = lambda i: (i, 0)
    pos_row = lambda i: (i % period_tiles, 0)
    return pl.pallas_call(
        _mla_proj_body,
        grid=(m // tm,),
        in_specs=[
            pl.BlockSpec((tm, D_MODEL), row),
            _const_spec((1, D_MODEL)),
            _const_spec(w_in.shape),
            _const_spec((1, Q_LORA)),
            _const_spec((1, KV_LORA)),
            _const_spec(w_uq.shape),
            _const_spec(w_ukt.shape),
            pl.BlockSpec((tm, QK_ROPE), pos_row),
            pl.BlockSpec((tm, QK_ROPE), pos_row),
        ],
        out_specs=[
            pl.BlockSpec((tm, MLA_ROW), row),
            pl.BlockSpec((tm, MLA_ROW), row),
            pl.BlockSpec((MLA_HEADS, tm, MLA_ROW), lambda i: (0, i, 0)),
        ],
        out_shape=[
            jax.ShapeDtypeStruct((m, MLA_ROW), F32),
            jax.ShapeDtypeStruct((m, MLA_ROW), BF16),
            jax.ShapeDtypeStruct((MLA_HEADS, m, MLA_ROW), BF16),
        ],
        compiler_params=_params("parallel"),
        name="mla_project",
    )(x, nw, w_in, g_q, g_kv, w_uq, w_ukt, cs, sn)


def _softmax_update(s, v, m_ref, l_ref, acc_ref, v_transposed=False):
    m_prev = m_ref[...]
    m_new = jnp.maximum(m_prev, jnp.max(s, axis=1, keepdims=True))
    alpha = jnp.exp(m_prev - m_new)
    p = jnp.exp(s - m_new)
    p16 = p.astype(BF16)
    pv = _dot_nt(p16, v) if v_transposed else _dot(p16, v)
    l_ref[...] = alpha * l_ref[...] + jnp.sum(p, axis=1, keepdims=True)
    acc_ref[...] = alpha * acc_ref[...] + pv
    m_ref[...] = m_new


def _softmax_init(m_ref, l_ref, acc_ref):
    m_ref[...] = jnp.full(m_ref.shape, -jnp.inf, F32)
    l_ref[...] = jnp.zeros(l_ref.shape, F32)
    acc_ref[...] = jnp.zeros(acc_ref.shape, F32)


def _attn_prompt_body(q_ref, kv_ref, kvt_ref, o_ref, m_ref, l_ref, acc_ref, sa_ref, sb_ref, *, tq):
    i = pl.program_id(1)
    heads = q_ref.shape[0]
    cols = heads * tq
    q = q_ref[...].reshape(cols, MLA_ROW)

    def scores(j, buf):
        k = kv_ref[pl.ds(pl.multiple_of(j * tq, tq), tq), :]
        buf[...] = _dot_nt(k, q)

    def absorb(j, buf, masked):
        st = buf[...]
        if masked:
            key = lax.broadcasted_iota(jnp.int32, (tq, cols), 0)
            tok = lax.broadcasted_iota(jnp.int32, (tq, cols), 1) % tq
            st = jnp.where(key <= tok, st, -jnp.inf)
        m_prev = m_ref[...]
        m_new = jnp.maximum(m_prev, jnp.max(st, axis=0, keepdims=True))
        alpha = jnp.exp(m_prev - m_new)
        p = jnp.exp(st - m_new)
        l_ref[...] = alpha * l_ref[...] + jnp.sum(p, axis=0, keepdims=True)
        acc_ref[...] = alpha * acc_ref[...] + _dot(kvt_ref[j], p.astype(BF16))
        m_ref[...] = m_new

    scores(0, sa_ref)
    _softmax_init(m_ref, l_ref, acc_ref)

    def body(p, carry):
        scores(2 * p + 1, sb_ref)
        absorb(2 * p, sa_ref, False)
        scores(2 * p + 2, sa_ref)
        absorb(2 * p + 1, sb_ref, False)
        return carry

    lax.fori_loop(0, i // 2, body, 0)

    @pl.when(i % 2 == 0)
    def _():
        absorb(i, sa_ref, True)

    @pl.when(i % 2 == 1)
    def _():
        scores(i, sb_ref)
        absorb(i - 1, sa_ref, False)
        absorb(i, sb_ref, True)

    o = (acc_ref[...] / l_ref[...]).T
    o_ref[...] = o.reshape(heads, tq, KV_LORA).astype(BF16)


def _attn_prompt(q, kv, batch, seq, tq):
    heads, m, _ = q.shape
    nq = seq // tq
    kvt = kv[:, :KV_LORA].reshape(batch * nq, tq, KV_LORA).transpose(0, 2, 1)
    return pl.pallas_call(
        functools.partial(_attn_prompt_body, tq=tq),
        grid=(batch, nq),
        in_specs=[
            pl.BlockSpec((heads, tq, MLA_ROW), lambda b, i: (0, b * nq + i, 0)),
            pl.BlockSpec((seq, MLA_ROW), lambda b, i: (b, 0)),
            pl.BlockSpec((nq, KV_LORA, tq), lambda b, i: (b, 0, 0)),
        ],
        out_specs=pl.BlockSpec((heads, tq, KV_LORA), lambda b, i: (0, b * nq + i, 0)),
        out_shape=jax.ShapeDtypeStruct((heads, m, KV_LORA), BF16),
        scratch_shapes=[
            pltpu.VMEM((1, heads * tq), F32),
            pltpu.VMEM((1, heads * tq), F32),
            pltpu.VMEM((KV_LORA, heads * tq), F32),
            pltpu.VMEM((tq, heads * tq), F32),
            pltpu.VMEM((tq, heads * tq), F32),
        ],
        compiler_params=_params("parallel", "parallel"),
        name="mla_attn_prompt",
    )(q, kv, kvt)


def _attn_decode_body(pt_ref, q_ref, kn_ref, cache_ref, o_ref, kbuf_ref, sem_ref,
                      m_ref, l_ref, acc_ref, *, pages, ts, layer_slot):
    b = pl.program_id(0)
    c = pl.program_id(1)
    n_chunks = pl.num_programs(1)
    step = b * n_chunks + c
    slot = step % 2

    def page_copy(page, p, sl):
        return pltpu.make_async_copy(cache_ref.at[layer_slot, page], kbuf_ref.at[sl, p],
                                     sem_ref.at[sl])

    def start_pages(bb, cc, sl):
        def body(p, carry):
            page_copy(pt_ref[bb, cc * pages + p], p, sl).start()
            return carry
        lax.fori_loop(0, pages, body, 0)

    @pl.when(step == 0)
    def _():
        start_pages(0, 0, 0)

    @pl.when(step + 1 < pl.num_programs(0) * n_chunks)
    def _():
        nxt = step + 1
        start_pages(nxt // n_chunks, nxt % n_chunks, 1 - slot)

    def wait_body(p, carry):
        page_copy(0, p, slot).wait()
        return carry

    lax.fori_loop(0, pages, wait_body, 0)

    @pl.when(c == 0)
    def _():
        _softmax_init(m_ref, l_ref, acc_ref)

    q = q_ref[0]
    group = DEC_PAGES_PER_GROUP
    kts, scores = [], []
    for g in range(pages // group):
        kt = jnp.concatenate([kbuf_ref[slot, p].astype(BF16)
                              for p in range(g * group, (g + 1) * group)],
                             axis=1)
        kts.append(kt)
        scores.append(_dot(q, kt))
    s = jnp.concatenate(scores, axis=1)
    m_prev = m_ref[...]
    m_new = jnp.maximum(m_prev, jnp.max(s, axis=1, keepdims=True))
    alpha = jnp.exp(m_prev - m_new)
    p = jnp.exp(s - m_new)
    l_ref[...] = alpha * l_ref[...] + jnp.sum(p, axis=1, keepdims=True)
    p16 = p.astype(BF16)
    width = group * PAGE_SIZE
    pv = sum(_dot_nt(p16[:, g * width:(g + 1) * width], kt[:KV_LORA, :])
             for g, kt in enumerate(kts))
    acc_ref[...] = alpha * acc_ref[...] + pv
    m_ref[...] = m_new

    @pl.when(c == pl.num_programs(1) - 1)
    def _():
        kn = kn_ref[0]
        s = _dot_nt(q, kn)
        tok = lax.broadcasted_iota(jnp.int32, s.shape, 0) % ts
        key = lax.broadcasted_iota(jnp.int32, s.shape, 1)
        s = jnp.where(key <= tok, s, -jnp.inf)
        _softmax_update(s, kn[:, :KV_LORA], m_ref, l_ref, acc_ref)
        o_ref[0] = (acc_ref[...] / l_ref[...]).astype(BF16)


def _attn_decode(q, k_new, cache_t, page_table, layer_slot, ts):
    bs, rows, _ = q.shape
    n_pages = page_table.shape[1]
    pages = DEC_PAGES_PER_STEP
    assert n_pages % pages == 0
    grid_spec = pltpu.PrefetchScalarGridSpec(
        num_scalar_prefetch=1,
        grid=(bs, n_pages // pages),
        in_specs=[
            pl.BlockSpec((1, rows, MLA_ROW), lambda b, c, pt: (b, 0, 0)),
            pl.BlockSpec((1, k_new.shape[1], MLA_ROW), lambda b, c, pt: (b, 0, 0)),
            pl.BlockSpec(memory_space=pl.ANY),
        ],
        out_specs=pl.BlockSpec((1, rows, KV_LORA), lambda b, c, pt: (b, 0, 0)),
        scratch_shapes=[
            pltpu.VMEM((2, pages, MLA_ROW, PAGE_SIZE), cache_t.dtype),
            pltpu.SemaphoreType.DMA((2,)),
            pltpu.VMEM((rows, 1), F32),
            pltpu.VMEM((rows, 1), F32),
            pltpu.VMEM((rows, KV_LORA), F32),
        ],
    )
    return pl.pallas_call(
        functools.partial(_attn_decode_body, pages=pages, ts=ts, layer_slot=layer_slot),
        grid_spec=grid_spec,
        out_shape=jax.ShapeDtypeStruct((bs, rows, KV_LORA), BF16),
        compiler_params=_params("arbitrary", "arbitrary"),
        name="mla_attn_decode",
    )(page_table, q, k_new, cache_t)


def _layer_tail_body(*refs, from_latent, d_ff, chunk):
    if from_latent:
        (a_ref, wuv_ref, wo_ref, h_ref, nw_mix_ref, nw_in_ref, win_ref, wout_ref, nw_out_ref,
         out_ref) = refs
        a16 = jnp.concatenate(
            [_dot(a_ref[h], wuv_ref[h]).astype(BF16) for h in range(MLA_HEADS)], axis=1)
    else:
        (a_ref, wo_ref, h_ref, nw_mix_ref, nw_in_ref, win_ref, wout_ref, nw_out_ref,
         out_ref) = refs
        a16 = a_ref[...]
    x = h_ref[...] + _rms(_dot(a16, wo_ref[...]), nw_mix_ref[...])
    u = _rms(x, nw_in_ref[...]).astype(BF16)
    acc = jnp.zeros(x.shape, F32)
    for c in range(d_ff // chunk):
        lo = c * chunk
        gate = _dot(u, win_ref[0, :, lo:lo + chunk])
        up = _dot(u, win_ref[0, :, d_ff + lo:d_ff + lo + chunk])
        act = (_silu(gate) * up).astype(BF16)
        acc = acc + _dot(act, wout_ref[0, lo:lo + chunk, :])
    out_ref[...] = x + _rms(acc, nw_out_ref[...])


def _layer_tail(a16, w_uv, w_o, h, nw_mix, ffn, tm):
    nw_in, w_in, w_out, layer, nw_out = ffn
    m = h.shape[0]
    d_ff = w_out.shape[1]
    assert d_ff % FFN_CHUNK == 0
    from_latent = w_uv is not None
    row = lambda i: (i, 0)
    if from_latent:
        in_specs = [pl.BlockSpec((MLA_HEADS, tm, KV_LORA), lambda i: (0, i, 0)),
                    _const_spec(w_uv.shape)]
        args = [a16, w_uv]
    else:
        in_specs = [pl.BlockSpec((tm, a16.shape[1]), row)]
        args = [a16]
    in_specs += [
        _const_spec(w_o.shape),
        pl.BlockSpec((tm, D_MODEL), row),
        _const_spec((1, D_MODEL)),
        _const_spec((1, D_MODEL)),
        _fixed_spec((1,) + w_in.shape[1:], (layer, 0, 0)),
        _fixed_spec((1,) + w_out.shape[1:], (layer, 0, 0)),
        _const_spec((1, D_MODEL)),
    ]
    args += [w_o, h, nw_mix, nw_in, w_in, w_out, nw_out]
    return pl.pallas_call(
        functools.partial(_layer_tail_body, from_latent=from_latent, d_ff=d_ff, chunk=FFN_CHUNK),
        grid=(m // tm,),
        in_specs=in_specs,
        out_specs=pl.BlockSpec((tm, D_MODEL), row),
        out_shape=jax.ShapeDtypeStruct((m, D_MODEL), F32),
        compiler_params=_params("parallel"),
        name="layer_tail",
    )(*args)


def _dn_proj_body(*refs, tm, seq_len, has_hist, tail_rows):
    if has_hist:
        (x_ref, nw_ref, wqkv_ref, wz_ref, wba_ref, cw_ref, alog_ref, dtb_ref, hist_ref,
         qkv_ref, z_ref, bg_ref, tail_ref, ext_ref) = refs
    else:
        (x_ref, nw_ref, wqkv_ref, wz_ref, wba_ref, cw_ref, alog_ref, dtb_ref,
         qkv_ref, z_ref, bg_ref, tail_ref, ext_ref) = refs
        hist_ref = None
    i = pl.program_id(0)
    carry = CONV_CARRY_ROWS

    @pl.when((i * tm) % seq_len == 0)
    def _():
        ext_ref[0:carry, :] = jnp.zeros((carry, DN_QKV), F32)

    u = _rms(x_ref[...], nw_ref[...]).astype(BF16)
    raw = _dot(u, wqkv_ref[...])
    ext_ref[carry:carry + tm, :] = raw
    cw = cw_ref[...]
    acc = raw * cw[CONV_W - 1:CONV_W, :]
    for k in range(1, CONV_W):
        shifted = ext_ref[carry - k:carry - k + tm, :]
        if has_hist:
            tpos = lax.broadcasted_iota(jnp.int32, (tm, 1), 0) % seq_len
            shifted = jnp.where(tpos >= k, shifted, hist_ref[k - 1])
        acc = acc + shifted * cw[CONV_W - 1 - k:CONV_W - k, :]
    ext_ref[0:carry, :] = raw[tm - carry:, :]
    tail_ref[0] = raw[tm - tail_rows:, :]
    act = _silu(acc)
    for h in range(DN_HEADS):
        qh = act[:, h * DN_DK:(h + 1) * DN_DK]
        kh = act[:, DN_HK + h * DN_DK:DN_HK + (h + 1) * DN_DK]
        qn = qh * lax.rsqrt(jnp.sum(qh * qh, axis=-1, keepdims=True) + L2_EPS) * (DN_DK ** -0.5)
        kn = kh * lax.rsqrt(jnp.sum(kh * kh, axis=-1, keepdims=True) + L2_EPS)
        qkv_ref[:, h * DN_DK:(h + 1) * DN_DK] = qn
        qkv_ref[:, DN_HK + h * DN_DK:DN_HK + (h + 1) * DN_DK] = kn
    qkv_ref[:, 2 * DN_HK:] = act[:, 2 * DN_HK:]
    z_ref[...] = _dot(u, wz_ref[...])
    ba = _dot(u, wba_ref[...])
    beta = _sigmoid(ba[:, :DN_HEADS])
    g = -jnp.exp(alog_ref[...]) * _softplus(ba[:, DN_HEADS:2 * DN_HEADS] + dtb_ref[...])
    bg_ref[:, :DN_HEADS] = beta
    bg_ref[:, DN_HEADS:] = g


def _dn_project(x, nw, w_all, w_ba, conv_w, a_log, dt_bias, hist, tm, seq_len, tail_rows):
    assert DN_QKV % DN_Z == 0
    m = x.shape[0]
    n_tiles = m // tm
    has_hist = hist is not None
    assert (tm % seq_len == 0 and n_tiles == 1) if has_hist else seq_len % tm == 0
    tiles_per_tail = max(seq_len // tm, 1)
    row = lambda i: (i, 0)
    in_specs = [
        pl.BlockSpec((tm, D_MODEL), row),
        _const_spec((1, D_MODEL)),
        _fixed_spec((D_MODEL, DN_QKV), (0, 0)),
        _fixed_spec((D_MODEL, DN_Z), (0, DN_QKV // DN_Z)),
        _const_spec(w_ba.shape),
        _const_spec(conv_w.shape),
        _const_spec((1, DN_HEADS)),
        _const_spec((1, DN_HEADS)),
    ]
    args = [x, nw, w_all, w_all, w_ba, conv_w, a_log, dt_bias]
    if has_hist:
        in_specs.append(pl.BlockSpec((CONV_W - 1, tm, DN_QKV), lambda i: (0, i, 0)))
        args.append(hist)
    return pl.pallas_call(
        functools.partial(_dn_proj_body, tm=tm, seq_len=seq_len, has_hist=has_hist,
                          tail_rows=tail_rows),
        grid=(n_tiles,),
        in_specs=in_specs,
        out_specs=[
            pl.BlockSpec((tm, DN_QKV), row),
            pl.BlockSpec((tm, DN_Z), row),
            pl.BlockSpec((tm, 2 * DN_HEADS), row),
            pl.BlockSpec((1, tail_rows, DN_QKV), lambda i: (i // tiles_per_tail, 0, 0)),
        ],
        out_shape=[
            jax.ShapeDtypeStruct((m, DN_QKV), F32),
            jax.ShapeDtypeStruct((m, DN_Z), F32),
            jax.ShapeDtypeStruct((m, 2 * DN_HEADS), F32),
            jax.ShapeDtypeStruct((n_tiles // tiles_per_tail, tail_rows, DN_QKV), F32),
        ],
        scratch_shapes=[pltpu.VMEM((CONV_CARRY_ROWS + tm, DN_QKV), F32)],
        compiler_params=_params("arbitrary"),
        name="dn_project",
    )(*args)


def _stack_heads(load, g):
    return jnp.concatenate(
        [load(slice((g * GDN_GROUP + hh) * DN_DK, (g * GDN_GROUP + hh + 1) * DN_DK))
         for hh in range(GDN_GROUP)], axis=0)


def _gdn_local_body(q_ref, k_ref, v_ref, bg_ref, bgt_ref,
                    u_ref, w_ref, qd_ref, kdt_ref, qk_ref, egl_ref, *, chunk, chunks_per_step):
    n_heads = DN_HEADS
    stack = GDN_GROUP * chunk
    ri = lax.broadcasted_iota(jnp.int32, (stack, stack), 0)
    ci = lax.broadcasted_iota(jnp.int32, (stack, stack), 1)
    same_head = (ri // chunk) == (ci // chunk)
    causal = same_head & (ri >= ci)
    strict = same_head & (ri > ci)
    r1 = lax.broadcasted_iota(jnp.int32, (chunk, chunk), 0)
    c1 = lax.broadcasted_iota(jnp.int32, (chunk, chunk), 1)
    tril = jnp.where(r1 >= c1, 1.0, 0.0).astype(BF16)
    triu = jnp.where(r1 <= c1, 1.0, 0.0).astype(BF16)
    n_double = int(math.log2(chunk)) - 2

    def store_heads(ref, rows, g, stacked):
        for hh in range(GDN_GROUP):
            h = g * GDN_GROUP + hh
            ref[rows, h * DN_DK:(h + 1) * DN_DK] = stacked[hh * chunk:(hh + 1) * chunk, :]

    chains = []
    for cp in range(chunks_per_step):
        rows = slice(cp * chunk, (cp + 1) * chunk)
        bg = bg_ref[rows, :]
        bgt = bgt_ref[cp]
        gc_col = sum(_dot(tril, part) for part in _split3(bg))
        gc_row = sum(_dot(part, triu) for part in _split3(bgt))
        egl_ref[cp] = jnp.broadcast_to(jnp.exp(gc_row[n_heads:, chunk - 1:chunk]),
                                       (n_heads, LANES))
        for g in range(n_heads // GDN_GROUP):
            heads = [g * GDN_GROUP + hh for hh in range(GDN_GROUP)]
            q = _stack_heads(lambda sl: q_ref[rows, sl], g)
            k = _stack_heads(lambda sl: k_ref[rows, sl], g)
            v = _stack_heads(lambda sl: v_ref[rows, sl], g)
            beta = jnp.concatenate([bg[:, h:h + 1] for h in heads], axis=0)
            gcc = jnp.concatenate([gc_col[:, n_heads + h:n_heads + h + 1] for h in heads], axis=0)
            gcr = jnp.concatenate([gc_row[n_heads + h:n_heads + h + 1, :] for h in heads], axis=1)
            g_last = jnp.concatenate(
                [jnp.broadcast_to(gc_col[chunk - 1:chunk, n_heads + h:n_heads + h + 1], (chunk, 1))
                 for h in heads], axis=0)
            eg = jnp.exp(gcc)
            kb = k * beta
            store_heads(qd_ref, rows, g, (q * eg).astype(BF16))
            kdt_ref[cp, :, g * stack:(g + 1) * stack] = (k * jnp.exp(g_last - gcc)).T.astype(BF16)
            chains.append(dict(
                rows=rows, g=g, q16=q.astype(BF16), k16=k.astype(BF16), kb16=kb.astype(BF16),
                decay=jnp.where(causal, jnp.exp(gcc - gcr), 0.0),
                rhs=jnp.concatenate([v * beta, kb * eg], axis=1)))

    for ch in chains:
        lower = jnp.where(strict, _dot_nt(ch["kb16"], ch["k16"]) * ch["decay"], 0.0)
        ch.update(lower=lower, power=lower, off=-lower)
    for ch in chains:
        qk = jnp.where(causal, _dot_nt(ch["q16"], ch["k16"]) * ch["decay"], 0.0)
        qk_packed = sum(qk[hh * chunk:(hh + 1) * chunk, :] for hh in range(GDN_GROUP))
        qk_ref[ch["rows"], ch["g"] * stack:(ch["g"] + 1) * stack] = qk_packed.astype(BF16)
    for _ in range(n_double):
        for ch in chains:
            p16 = ch["power"].astype(BF16)
            ch["power"] = _dot(p16, p16)
        for ch in chains:
            ch["off"] = ch["off"] + ch["power"] + _dot(ch["off"].astype(BF16),
                                                       ch["power"].astype(BF16))
    for ch in chains:
        ch["resid"] = (ch["lower"] + ch["off"]) + _dot_hl(ch["lower"], ch["off"])
    for ch in chains:
        ch["off"] = ch["off"] - ch["resid"] - _dot(ch["off"].astype(BF16),
                                                   ch["resid"].astype(BF16))
    for ch in chains:
        sol = ch["rhs"] + _dot(ch["off"].astype(BF16), ch["rhs"].astype(BF16))
        store_heads(u_ref, ch["rows"], ch["g"], sol[:, :DN_DV])
        store_heads(w_ref, ch["rows"], ch["g"], sol[:, DN_DV:].astype(BF16))


def _gdn_scan_body(u_ref, w_ref, qd_ref, kdt_ref, qk_ref, egl_ref, z_ref, gout_ref, s0_ref,
                   y_ref, s_ref, *, chunk, seqs):
    c = pl.program_id(1)

    @pl.when(c == 0)
    def _():
        s_ref[...] = s0_ref[...]

    stack = GDN_GROUP * chunk
    pair = 2 * DN_DK
    lhs_mask = ((lax.broadcasted_iota(jnp.int32, (4 * chunk, pair), 0) // chunk) % 2
                == lax.broadcasted_iota(jnp.int32, (4 * chunk, pair), 1) // DN_DK)
    qk_mask = (lax.broadcasted_iota(jnp.int32, (stack, stack), 0) // chunk
               == lax.broadcasted_iota(jnp.int32, (stack, stack), 1) // chunk)
    kd_mask = (lax.broadcasted_iota(jnp.int32, (GDN_GROUP * DN_DK, stack), 0) // DN_DK
               == lax.broadcasted_iota(jnp.int32, (GDN_GROUP * DN_DK, stack), 1) // chunk)
    zero16 = jnp.zeros((), BF16)
    chains = [dict(sq=sq, g=g) for sq in range(seqs) for g in range(DN_HEADS // GDN_GROUP)]
    for ch in chains:
        sq, g = ch["sq"], ch["g"]
        s = s_ref[sq, g * GDN_GROUP:(g + 1) * GDN_GROUP].reshape(GDN_GROUP * DN_DK, DN_DV)
        s16 = s.astype(BF16)
        ws, qs = [], []
        for p in range(GDN_GROUP // 2):
            cols = [slice((g * GDN_GROUP + 2 * p + e) * DN_DK, (g * GDN_GROUP + 2 * p + e + 1) * DN_DK)
                    for e in range(2)]
            lhs = jnp.concatenate([w_ref[sq, :, cols[0]], w_ref[sq, :, cols[1]],
                                   qd_ref[sq, :, cols[0]], qd_ref[sq, :, cols[1]]], axis=0)
            lhs_bd = jnp.where(lhs_mask, jnp.concatenate([lhs, lhs], axis=1), zero16)
            res = _dot(lhs_bd, s16[p * pair:(p + 1) * pair, :])
            ws.append(res[:2 * chunk])
            qs.append(res[2 * chunk:])
        ch.update(s=s, ws=jnp.concatenate(ws, axis=0), qs=jnp.concatenate(qs, axis=0))
    for ch in chains:
        sq, g = ch["sq"], ch["g"]
        v_new = _stack_heads(lambda sl: u_ref[sq, :, sl], g) - ch["ws"]
        ch["v16"] = v_new.astype(BF16)
    gout = gout_ref[...]
    for ch in chains:
        sq, g = ch["sq"], ch["g"]
        qk = qk_ref[sq, :, g * stack:(g + 1) * stack]
        qk_bd = jnp.where(qk_mask, jnp.concatenate([qk] * GDN_GROUP, axis=0), zero16)
        o = ch["qs"] + _dot(qk_bd, ch["v16"])
        y = _rms(o, gout) * _silu(_stack_heads(lambda sl: z_ref[sq, :, sl], g))
        y16 = y.astype(BF16)
        for hh in range(GDN_GROUP):
            h = g * GDN_GROUP + hh
            y_ref[sq, :, h * DN_DV:(h + 1) * DN_DV] = y16[hh * chunk:(hh + 1) * chunk, :]
    for ch in chains:
        sq, g = ch["sq"], ch["g"]
        kdt = kdt_ref[sq, 0, :, g * stack:(g + 1) * stack]
        kdt_bd = jnp.where(kd_mask, jnp.concatenate([kdt] * GDN_GROUP, axis=0), zero16)
        gate = jnp.concatenate(
            [jnp.broadcast_to(egl_ref[sq, 0, g * GDN_GROUP + hh:g * GDN_GROUP + hh + 1, :],
                              (DN_DK, DN_DV)) for hh in range(GDN_GROUP)], axis=0)
        s_new = ch["s"] * gate + _dot(kdt_bd, ch["v16"])
        s_ref[sq, g * GDN_GROUP:(g + 1) * GDN_GROUP] = s_new.reshape(GDN_GROUP, DN_DK, DN_DV)


def _gdn(qkv, bg, z, g_out, s0, n_seq, seq_len, chunk):
    m = qkv.shape[0]
    n_chunks = m // chunk
    nc = seq_len // chunk
    cps = GDN_CHUNKS_PER_STEP
    sps = GDN_SEQS_PER_STEP
    assert n_chunks % cps == 0 and n_seq % sps == 0
    bgt = bg.reshape(n_chunks, chunk, 2 * DN_HEADS).transpose(0, 2, 1)
    half = DN_HEADS * chunk
    row = lambda i: (i, 0)
    blk = lambda col: pl.BlockSpec((cps * chunk, DN_HK), lambda i, col=col: (i, col))
    u, w16, qd16, kdt16, qk16, egl = pl.pallas_call(
        functools.partial(_gdn_local_body, chunk=chunk, chunks_per_step=cps),
        grid=(n_chunks // cps,),
        in_specs=[
            blk(0), blk(1), blk(2),
            pl.BlockSpec((cps * chunk, 2 * DN_HEADS), row),
            pl.BlockSpec((cps, 2 * DN_HEADS, chunk), lambda i: (i, 0, 0)),
        ],
        out_specs=[
            pl.BlockSpec((cps * chunk, DN_Z), row),
            pl.BlockSpec((cps * chunk, DN_HK), row),
            pl.BlockSpec((cps * chunk, DN_HK), row),
            pl.BlockSpec((cps, DN_DK, half), lambda i: (i, 0, 0)),
            pl.BlockSpec((cps * chunk, half), row),
            pl.BlockSpec((cps, DN_HEADS, LANES), lambda i: (i, 0, 0)),
        ],
        out_shape=[
            jax.ShapeDtypeStruct((m, DN_Z), F32),
            jax.ShapeDtypeStruct((m, DN_HK), BF16),
            jax.ShapeDtypeStruct((m, DN_HK), BF16),
            jax.ShapeDtypeStruct((n_chunks, DN_DK, half), BF16),
            jax.ShapeDtypeStruct((m, half), BF16),
            jax.ShapeDtypeStruct((n_chunks, DN_HEADS, LANES), F32),
        ],
        compiler_params=_params("parallel"),
        name="gdn_local",
    )(qkv, qkv, qkv, bg, bgt)

    per_seq = lambda a: a.reshape((n_seq, nc if a.ndim == 3 else seq_len) + a.shape[1:])
    tok_spec = lambda width: pl.BlockSpec((sps, chunk, width), lambda b, c: (b, c, 0))
    lead_spec = lambda d1, d2: pl.BlockSpec((sps, 1, d1, d2), lambda b, c: (b, c, 0, 0))
    state_spec = pl.BlockSpec((sps, DN_HEADS, DN_DK, DN_DV), lambda b, c: (b, 0, 0, 0))
    y16, s = pl.pallas_call(
        functools.partial(_gdn_scan_body, chunk=chunk, seqs=sps),
        grid=(n_seq // sps, nc),
        in_specs=[
            tok_spec(DN_Z),
            tok_spec(DN_HK),
            tok_spec(DN_HK),
            lead_spec(DN_DK, half),
            tok_spec(half),
            lead_spec(DN_HEADS, LANES),
            tok_spec(DN_Z),
            pl.BlockSpec((1, DN_DV), lambda b, c: (0, 0)),
            state_spec,
        ],
        out_specs=[tok_spec(DN_Z), state_spec],
        out_shape=[
            jax.ShapeDtypeStruct((n_seq, seq_len, DN_Z), BF16),
            jax.ShapeDtypeStruct(s0.shape, F32),
        ],
        compiler_params=_params("parallel", "arbitrary"),
        name="gdn_scan",
    )(per_seq(u), per_seq(w16), per_seq(qd16), per_seq(kdt16), per_seq(qk16), per_seq(egl),
      per_seq(z), g_out, s0)
    return y16.reshape(m, DN_Z), s


def _mla_layer(hp, hs, cache_mla, page_table, slot, nw, w_in, g_q, g_kv, w_uq, w_uk, w_uv, w_o,
               ffn, bp, tp, bs, ts):
    past = page_table.shape[1] * PAGE_SIZE
    w_in16 = w_in.astype(BF16)
    w_uq_heads = w_uq.reshape(Q_LORA, MLA_HEADS, QK_NOPE + QK_ROPE)
    w_uq16 = jnp.concatenate(
        [w_uq_heads[:, :, :QK_NOPE].reshape(Q_LORA, MLA_HEADS * QK_NOPE),
         w_uq_heads[:, :, QK_NOPE:].reshape(Q_LORA, MLA_HEADS * QK_ROPE)], axis=1).astype(BF16)
    w_ukt16 = jnp.swapaxes(w_uk, 1, 2).astype(BF16)
    w_uv16 = w_uv.astype(BF16)
    w_o16 = w_o.astype(BF16)
    g_q = g_q.reshape(1, Q_LORA)
    g_kv = g_kv.reshape(1, KV_LORA)
    nw0 = nw[0].reshape(1, D_MODEL)
    nw1 = nw[1].reshape(1, D_MODEL)

    cs_p, sn_p = _rope_tables(jnp.arange(tp))
    cs_s, sn_s = _rope_tables(past + jnp.tile(jnp.arange(ts), bs))
    ms = bs * ts
    rows_p, kv_p, q_p = _mla_project(hp, nw0, w_in16, g_q, g_kv, w_uq16, w_ukt16, cs_p, sn_p,
                                     TM_TOKENS)
    rows_s, kv_s, q_s = _mla_project(hs, nw0, w_in16, g_q, g_kv, w_uq16, w_ukt16, cs_s, sn_s, ms)

    o_p = _attn_prompt(q_p, kv_p, bp, tp, TQ_ATTN)

    q_sb = q_s.reshape(MLA_HEADS, bs, ts, MLA_ROW).transpose(1, 0, 2, 3).reshape(
        bs, MLA_HEADS * ts, MLA_ROW)
    new_rows = 2 * SUBLANES
    k_new = jnp.pad(kv_s.reshape(bs, ts, MLA_ROW), ((0, 0), (0, new_rows - ts), (0, 0)))
    o_sb = _attn_decode(q_sb, k_new, jnp.swapaxes(cache_mla, 2, 3), page_table, slot, ts)
    o_s = o_sb.reshape(bs, MLA_HEADS, ts, KV_LORA).transpose(1, 0, 2, 3).reshape(
        MLA_HEADS, ms, KV_LORA)

    hp = _layer_tail(o_p, w_uv16, w_o16, hp, nw1, ffn, TM_TOKENS)
    hs = _layer_tail(o_s, w_uv16, w_o16, hs, nw1, ffn, ms)
    return hp, hs, rows_p.reshape(bp, tp, MLA_ROW), rows_s.reshape(bs, ts, MLA_ROW)


def _dn_layer(hp, hs, s0_s, conv0_s, nw, w_in, conv_w, a_log, dt_bias, g_out, w_o, ffn,
              bp, tp, bs, ts):
    w_in16 = w_in.astype(BF16)
    w_ba16 = jnp.pad(w_in[:, DN_QKV + DN_Z:], ((0, 0), (0, LANES - 2 * DN_HEADS))).astype(BF16)
    w_o16 = w_o.astype(BF16)
    a_log = a_log.reshape(1, DN_HEADS).astype(F32)
    dt_bias = dt_bias.reshape(1, DN_HEADS).astype(F32)
    g_out = g_out.reshape(1, DN_DV)
    nw0 = nw[0].reshape(1, D_MODEL)
    nw1 = nw[1].reshape(1, D_MODEL)
    hist_rows = CONV_W - 1
    ms = bs * ts

    qkv_p, z_p, bg_p, tail_p = _dn_project(hp, nw0, w_in16, w_ba16, conv_w, a_log, dt_bias,
                                           None, TM_DN_PROJ, tp, SUBLANES)
    conv_p = tail_p[:, SUBLANES - hist_rows:, :]
    s0_p = jnp.zeros((bp, DN_HEADS, DN_DK, DN_DV), F32)
    y_p, s_p = _gdn(qkv_p, bg_p, z_p, g_out, s0_p, bp, tp, DN_CHUNK)
    hp = _layer_tail(y_p, None, w_o16, hp, nw1, ffn, TM_TOKENS)

    tok = jnp.arange(ts)
    hist = jnp.stack([
        conv0_s[:, jnp.clip(hist_rows - k + tok, 0, hist_rows - 1), :].reshape(ms, DN_QKV)
        for k in range(1, CONV_W)
    ]).astype(F32)
    qkv_s, z_s, bg_s, tail_s = _dn_project(hs, nw0, w_in16, w_ba16, conv_w, a_log, dt_bias,
                                           hist, ms, ts, ms)
    raw_s = tail_s.reshape(bs, ts, DN_QKV)
    conv_s = jnp.concatenate([conv0_s.astype(F32), raw_s], axis=1)[:, ts:, :]
    cs = DN_CHUNK_SAMPLE
    assert ts <= cs
    pad = lambda a: jnp.pad(a.reshape(bs, ts, -1), ((0, 0), (0, cs - ts), (0, 0))).reshape(
        bs * cs, -1)
    y_s_pad, s_s = _gdn(pad(qkv_s), pad(bg_s), pad(z_s), g_out, s0_s.astype(F32), bs, cs, cs)
    y_s = y_s_pad.reshape(bs, cs, DN_Z)[:, :ts, :].reshape(ms, DN_Z)
    hs = _layer_tail(y_s, None, w_o16, hs, nw1, ffn, ms)
    return hp, hs, s_p, s_s, conv_p, conv_s


def kernel(x_prompt, x_sample, cache_mla, state_dn, state_dn_conv, page_table, norm_w, mla_w_in,
           mla_g_q, mla_g_kv, mla_w_uq, mla_w_uk, mla_w_uv, mla_w_o, dn_w_in, dn_conv_w, dn_a_log,
           dn_dt_bias, dn_g_out, dn_w_o, ffn_w_in, ffn_w_out):
    bp, tp, _ = x_prompt.shape
    bs, ts, _ = x_sample.shape
    depth = norm_w.shape[0]
    hp = x_prompt.reshape(bp * tp, D_MODEL)
    hs = x_sample.reshape(bs * ts, D_MODEL)
    rows_p_l, rows_s_l, sp_l, ss_l, cp_l, cs_l = [], [], [], [], [], []
    ffn_w_in16 = ffn_w_in.astype(BF16)
    ffn_w_out16 = ffn_w_out.astype(BF16)
    for layer in range(depth):
        j = layer // N_MIXERS
        nw = norm_w[layer]
        ffn = (nw[2].reshape(1, D_MODEL), ffn_w_in16, ffn_w_out16, layer, nw[3].reshape(1, D_MODEL))
        if layer % N_MIXERS == 0:
            hp, hs, rows_p, rows_s = _mla_layer(
                hp, hs, cache_mla, page_table, j, nw, mla_w_in[j], mla_g_q[j], mla_g_kv[j],
                mla_w_uq[j], mla_w_uk[j], mla_w_uv[j], mla_w_o[j], ffn, bp, tp, bs, ts)
            rows_p_l.append(rows_p)
            rows_s_l.append(rows_s)
        else:
            hp, hs, s_p, s_s, c_p, c_s = _dn_layer(
                hp, hs, state_dn[j], state_dn_conv[j], nw, dn_w_in[j], dn_conv_w[j], dn_a_log[j],
                dn_dt_bias[j], dn_g_out[j], dn_w_o[j], ffn, bp, tp, bs, ts)
            sp_l.append(s_p.astype(state_dn.dtype))
            ss_l.append(s_s.astype(state_dn.dtype))
            cp_l.append(c_p.astype(state_dn_conv.dtype))
            cs_l.append(c_s.astype(state_dn_conv.dtype))
    return (hp.reshape(bp, tp, D_MODEL), hs.reshape(bs, ts, D_MODEL),
            jnp.stack(rows_p_l), jnp.stack(rows_s_l), jnp.stack(sp_l), jnp.stack(ss_l),
            jnp.stack(cp_l), jnp.stack(cs_l))
```

```python
import functools
import math

import jax
import jax.numpy as jnp
from jax import lax
from jax.experimental import pallas as pl
from jax.experimental.pallas import tpu as pltpu

F32 = jnp.float32
BF16 = jnp.bfloat16

D_MODEL = 1024
PAGE_SIZE = 128
N_MIXERS = 2

MLA_HEADS = 8
QK_NOPE = 128
QK_ROPE = 64
V_HEAD = 128
KV_LORA = 256
Q_LORA = 384
MLA_ROW = KV_LORA + QK_ROPE
MLA_SCALE = (QK_NOPE + QK_ROPE) ** -0.5
ROPE_THETA = 10000.0

DN_HEADS = 8
DN_DK = 128
DN_DV = 128
CONV_W = 4
DN_HK = DN_HEADS * DN_DK
DN_QKV = DN_HEADS * (2 * DN_DK + DN_DV)
DN_Z = DN_HEADS * DN_DV

RMS_EPS = 1e-6
L2_EPS = 1e-6

LANES = 128
SUBLANES = 8
VMEM_LIMIT_BYTES = 56 * 1024 * 1024

TM_TOKENS = 512
DN_PROJ_ROW_CHAINS = 4
DN_PROJ_GATE_SLOT = 0
TM_DN_PROJ = 512
TQ_ATTN = 256
DN_CHUNK = 64
DN_CHUNK_SAMPLE = 16
GDN_GROUP = 4
GDN_CHUNKS_PER_STEP = 4
GDN_SEQS_PER_STEP = 4
DEC_PAGES_PER_STEP = 64
DEC_PAGES_PER_GROUP = 8
FFN_CHUNK = 256
CONV_CARRY_ROWS = SUBLANES


def _params(*sem):
    return pltpu.CompilerParams(dimension_semantics=sem, vmem_limit_bytes=VMEM_LIMIT_BYTES)


def _fixed_spec(block_shape, block_index):
    return pl.BlockSpec(block_shape, lambda *_: block_index, pipeline_mode=pl.Buffered(1))


def _const_spec(shape):
    return _fixed_spec(shape, (0,) * len(shape))


def _dot(a, b):
    return jnp.dot(a, b, preferred_element_type=F32)


def _dot_nt(a, b):
    return lax.dot_general(a, b, (((1,), (1,)), ((), ())), preferred_element_type=F32)


def _dot_tn(a, b):
    return lax.dot_general(a, b, (((0,), (0,)), ((), ())), preferred_element_type=F32)


def _split2(x):
    hi = x.astype(BF16)
    lo = (x - hi.astype(F32)).astype(BF16)
    return hi, lo


def _split3(x):
    hi = x.astype(BF16)
    r = x - hi.astype(F32)
    mid = r.astype(BF16)
    lo = (r - mid.astype(F32)).astype(BF16)
    return hi, mid, lo


def _dot_hl(a, b):
    ah, al = _split2(a)
    bh, bl = _split2(b)
    return _dot(ah, bh) + (_dot(ah, bl) + _dot(al, bh))


def _rms(x, w):
    return x * lax.rsqrt(jnp.mean(x * x, axis=-1, keepdims=True) + RMS_EPS) * w


def _sigmoid(x):
    return 1.0 / (1.0 + jnp.exp(-x))


def _silu(x):
    return x * _sigmoid(x)


def _softplus(x):
    return jnp.maximum(x, 0.0) + jnp.log1p(jnp.exp(-jnp.abs(x)))


def _rope(x, cs, sn):
    half = x.shape[-1] // 2
    swapped = jnp.concatenate([x[:, half:], x[:, :half]], axis=1)
    return x * cs + swapped * sn


def _rope_tables(pos):
    half = QK_ROPE // 2
    freq = ROPE_THETA ** (-jnp.arange(half, dtype=F32) / half)
    ang = pos.astype(F32)[:, None] * freq[None, :]
    cos, sin = jnp.cos(ang), jnp.sin(ang)
    return jnp.concatenate([cos, cos], axis=1), jnp.concatenate([-sin, sin], axis=1)


def _mla_proj_body(x_ref, nw_ref, win_ref, gq_ref, gkv_ref, wuq_ref, wukt_ref, cs_ref, sn_ref,
                   rows_ref, kv_ref, q_ref):
    u = _rms(x_ref[...], nw_ref[...]).astype(BF16)
    a = _dot(u, win_ref[...])
    c_q = _rms(a[:, :Q_LORA], gq_ref[...]).astype(BF16)
    c_kv = _rms(a[:, Q_LORA:Q_LORA + KV_LORA], gkv_ref[...])
    cs = cs_ref[...]
    sn = sn_ref[...]
    k_r = _rope(a[:, Q_LORA + KV_LORA:], cs, sn)
    rows_ref[:, :KV_LORA] = c_kv
    rows_ref[:, KV_LORA:] = k_r
    kv_ref[:, :KV_LORA] = c_kv.astype(BF16)
    kv_ref[:, KV_LORA:] = k_r.astype(BF16)
    q = _dot(c_q, wuq_ref[...])
    rope_base = MLA_HEADS * QK_NOPE
    for h in range(MLA_HEADS):
        q_lat = _dot(q[:, h * QK_NOPE:(h + 1) * QK_NOPE].astype(BF16), wukt_ref[h])
        q_rope = _rope(q[:, rope_base + h * QK_ROPE:rope_base + (h + 1) * QK_ROPE], cs, sn)
        q_ref[h, :, :KV_LORA] = (q_lat * MLA_SCALE).astype(BF16)
        q_ref[h, :, KV_LORA:] = (q_rope * MLA_SCALE).astype(BF16)


def _mla_project(x, nw, w_in, g_q, g_kv, w_uq, w_ukt, cs, sn, tm):
    m = x.shape[0]
    period_tiles = cs.shape[0] // tm
    assert cs.shape[0] % tm == 0 and m % cs.shape[0] == 0
    row = lambda i: (i, 0)
    pos_row = lambda i: (i % period_tiles, 0)
    return pl.pallas_call(
        _mla_proj_body,
        grid=(m // tm,),
        in_specs=[
            pl.BlockSpec((tm, D_MODEL), row),
            _const_spec((1, D_MODEL)),
            _const_spec(w_in.shape),
            _const_spec((1, Q_LORA)),
            _const_spec((1, KV_LORA)),
            _const_spec(w_uq.shape),
            _const_spec(w_ukt.shape),
            pl.BlockSpec((tm, QK_ROPE), pos_row),
            pl.BlockSpec((tm, QK_ROPE), pos_row),
        ],
        out_specs=[
            pl.BlockSpec((tm, MLA_ROW), row),
            pl.BlockSpec((tm, MLA_ROW), row),
            pl.BlockSpec((MLA_HEADS, tm, MLA_ROW), lambda i: (0, i, 0)),
        ],
        out_shape=[
            jax.ShapeDtypeStruct((m, MLA_ROW), F32),
            jax.ShapeDtypeStruct((m, MLA_ROW), BF16),
            jax.ShapeDtypeStruct((MLA_HEADS, m, MLA_ROW), BF16),
        ],
        compiler_params=_params("parallel"),
        name="mla_project",
    )(x, nw, w_in, g_q, g_kv, w_uq, w_ukt, cs, sn)


def _softmax_update(s, v, m_ref, l_ref, acc_ref, v_transposed=False):
    m_prev = m_ref[...]
    m_new = jnp.maximum(m_prev, jnp.max(s, axis=1, keepdims=True))
    alpha = jnp.exp(m_prev - m_new)
    p = jnp.exp(s - m_new)
    p16 = p.astype(BF16)
    pv = _dot_nt(p16, v) if v_transposed else _dot(p16, v)
    l_ref[...] = alpha * l_ref[...] + jnp.sum(p, axis=1, keepdims=True)
    acc_ref[...] = alpha * acc_ref[...] + pv
    m_ref[...] = m_new


def _softmax_init(m_ref, l_ref, acc_ref):
    m_ref[...] = jnp.full(m_ref.shape, -jnp.inf, F32)
    l_ref[...] = jnp.zeros(l_ref.shape, F32)
    acc_ref[...] = jnp.zeros(acc_ref.shape, F32)


def _attn_prompt_body(q_ref, kv_ref, kvt_ref, o_ref, m_ref, l_ref, acc_ref, sa_ref, sb_ref, *, tq):
    i = pl.program_id(1)
    heads = q_ref.shape[0]
    cols = heads * tq
    q = q_ref[...].reshape(cols, MLA_ROW)

    def scores(j, buf):
        k = kv_ref[pl.ds(pl.multiple_of(j * tq, tq), tq), :]
        buf[...] = _dot_nt(k, q)

    def absorb(j, buf, masked):
        st = buf[...]
        if masked:
            key = lax.broadcasted_iota(jnp.int32, (tq, cols), 0)
            tok = lax.broadcasted_iota(jnp.int32, (tq, cols), 1) % tq
            st = jnp.where(key <= tok, st, -jnp.inf)
        m_prev = m_ref[...]
        m_new = jnp.maximum(m_prev, jnp.max(st, axis=0, keepdims=True))
        alpha = jnp.exp(m_prev - m_new)
        p = jnp.exp(st - m_new)
        l_ref[...] = alpha * l_ref[...] + jnp.sum(p, axis=0, keepdims=True)
        acc_ref[...] = alpha * acc_ref[...] + _dot(kvt_ref[j], p.astype(BF16))
        m_ref[...] = m_new

    scores(0, sa_ref)
    _softmax_init(m_ref, l_ref, acc_ref)

    def body(p, carry):
        scores(2 * p + 1, sb_ref)
        absorb(2 * p, sa_ref, False)
        scores(2 * p + 2, sa_ref)
        absorb(2 * p + 1, sb_ref, False)
        return carry

    lax.fori_loop(0, i // 2, body, 0)

    @pl.when(i % 2 == 0)
    def _():
        absorb(i, sa_ref, True)

    @pl.when(i % 2 == 1)
    def _():
        scores(i, sb_ref)
        absorb(i - 1, sa_ref, False)
        absorb(i, sb_ref, True)

    o = (acc_ref[...] / l_ref[...]).T
    o_ref[...] = o.reshape(heads, tq, KV_LORA).astype(BF16)


def _attn_prompt(q, kv, batch, seq, tq):
    heads, m, _ = q.shape
    nq = seq // tq
    kvt = kv[:, :KV_LORA].reshape(batch * nq, tq, KV_LORA).transpose(0, 2, 1)
    return pl.pallas_call(
        functools.partial(_attn_prompt_body, tq=tq),
        grid=(batch, nq),
        in_specs=[
            pl.BlockSpec((heads, tq, MLA_ROW), lambda b, i: (0, b * nq + i, 0)),
            pl.BlockSpec((seq, MLA_ROW), lambda b, i: (b, 0)),
            pl.BlockSpec((nq, KV_LORA, tq), lambda b, i: (b, 0, 0)),
        ],
        out_specs=pl.BlockSpec((heads, tq, KV_LORA), lambda b, i: (0, b * nq + i, 0)),
        out_shape=jax.ShapeDtypeStruct((heads, m, KV_LORA), BF16),
        scratch_shapes=[
            pltpu.VMEM((1, heads * tq), F32),
            pltpu.VMEM((1, heads * tq), F32),
            pltpu.VMEM((KV_LORA, heads * tq), F32),
            pltpu.VMEM((tq, heads * tq), F32),
            pltpu.VMEM((tq, heads * tq), F32),
        ],
        compiler_params=_params("parallel", "parallel"),
        name="mla_attn_prompt",
    )(q, kv, kvt)


def _attn_decode_body(pt_ref, q_ref, kn_ref, cache_ref, o_ref, kbuf_ref, sem_ref,
                      m_ref, l_ref, acc_ref, *, pages, ts, layer_slot):
    b = pl.program_id(0)
    c = pl.program_id(1)
    n_chunks = pl.num_programs(1)
    step = b * n_chunks + c
    slot = step % 2

    def page_copy(page, p, sl):
        return pltpu.make_async_copy(cache_ref.at[layer_slot, page], kbuf_ref.at[sl, p],
                                     sem_ref.at[sl])

    def start_pages(bb, cc, sl):
        def body(p, carry):
            page_copy(pt_ref[bb, cc * pages + p], p, sl).start()
            return carry
        lax.fori_loop(0, pages, body, 0)

    @pl.when(step == 0)
    def _():
        start_pages(0, 0, 0)

    @pl.when(step + 1 < pl.num_programs(0) * n_chunks)
    def _():
        nxt = step + 1
        start_pages(nxt // n_chunks, nxt % n_chunks, 1 - slot)

    def wait_body(p, carry):
        page_copy(0, p, slot).wait()
        return carry

    lax.fori_loop(0, pages, wait_body, 0)

    @pl.when(c == 0)
    def _():
        _softmax_init(m_ref, l_ref, acc_ref)

    q = q_ref[0]
    group = DEC_PAGES_PER_GROUP
    kts, scores = [], []
    for g in range(pages // group):
        kt = jnp.concatenate([kbuf_ref[slot, p].astype(BF16)
                              for p in range(g * group, (g + 1) * group)],
                             axis=1)
        kts.append(kt)
        scores.append(_dot(q, kt))
    s = jnp.concatenate(scores, axis=1)
    m_prev = m_ref[...]
    m_new = jnp.maximum(m_prev, jnp.max(s, axis=1, keepdims=True))
    alpha = jnp.exp(m_prev - m_new)
    p = jnp.exp(s - m_new)
    l_ref[...] = alpha * l_ref[...] + jnp.sum(p, axis=1, keepdims=True)
    p16 = p.astype(BF16)
    width = group * PAGE_SIZE
    pv = sum(_dot_nt(p16[:, g * width:(g + 1) * width], kt[:KV_LORA, :])
             for g, kt in enumerate(kts))
    acc_ref[...] = alpha * acc_ref[...] + pv
    m_ref[...] = m_new

    @pl.when(c == pl.num_programs(1) - 1)
    def _():
        kn = kn_ref[0]
        s = _dot_nt(q, kn)
        tok = lax.broadcasted_iota(jnp.int32, s.shape, 0) % ts
        key = lax.broadcasted_iota(jnp.int32, s.shape, 1)
        s = jnp.where(key <= tok, s, -jnp.inf)
        _softmax_update(s, kn[:, :KV_LORA], m_ref, l_ref, acc_ref)
        o_ref[0] = (acc_ref[...] / l_ref[...]).astype(BF16)


def _attn_decode(q, k_new, cache_t, page_table, layer_slot, ts):
    bs, rows, _ = q.shape
    n_pages = page_table.shape[1]
    pages = DEC_PAGES_PER_STEP
    assert n_pages % pages == 0
    grid_spec = pltpu.PrefetchScalarGridSpec(
        num_scalar_prefetch=1,
        grid=(bs, n_pages // pages),
        in_specs=[
            pl.BlockSpec((1, rows, MLA_ROW), lambda b, c, pt: (b, 0, 0)),
            pl.BlockSpec((1, k_new.shape[1], MLA_ROW), lambda b, c, pt: (b, 0, 0)),
            pl.BlockSpec(memory_space=pl.ANY),
        ],
        out_specs=pl.BlockSpec((1, rows, KV_LORA), lambda b, c, pt: (b, 0, 0)),
        scratch_shapes=[
            pltpu.VMEM((2, pages, MLA_ROW, PAGE_SIZE), cache_t.dtype),
            pltpu.SemaphoreType.DMA((2,)),
            pltpu.VMEM((rows, 1), F32),
            pltpu.VMEM((rows, 1), F32),
            pltpu.VMEM((rows, KV_LORA), F32),
        ],
    )
    return pl.pallas_call(
        functools.partial(_attn_decode_body, pages=pages, ts=ts, layer_slot=layer_slot),
        grid_spec=grid_spec,
        out_shape=jax.ShapeDtypeStruct((bs, rows, KV_LORA), BF16),
        compiler_params=_params("arbitrary", "arbitrary"),
        name="mla_attn_decode",
    )(page_table, q, k_new, cache_t)


def _layer_tail_body(*refs, from_latent, d_ff, chunk):
    if from_latent:
        (a_ref, wuv_ref, wo_ref, h_ref, nw_mix_ref, nw_in_ref, win_ref, wout_ref, nw_out_ref,
         out_ref) = refs
        a16 = jnp.concatenate(
            [_dot(a_ref[h], wuv_ref[h]).astype(BF16) for h in range(MLA_HEADS)], axis=1)
    else:
        (a_ref, wo_ref, h_ref, nw_mix_ref, nw_in_ref, win_ref, wout_ref, nw_out_ref,
         out_ref) = refs
        a16 = a_ref[...]
    x = h_ref[...] + _rms(_dot(a16, wo_ref[...]), nw_mix_ref[...])
    u = _rms(x, nw_in_ref[...]).astype(BF16)
    acc = jnp.zeros(x.shape, F32)
    for c in range(d_ff // chunk):
        lo = c * chunk
        gate = _dot(u, win_ref[0, :, lo:lo + chunk])
        up = _dot(u, win_ref[0, :, d_ff + lo:d_ff + lo + chunk])
        act = (_silu(gate) * up).astype(BF16)
        acc = acc + _dot(act, wout_ref[0, lo:lo + chunk, :])
    out_ref[...] = x + _rms(acc, nw_out_ref[...])


def _layer_tail(a16, w_uv, w_o, h, nw_mix, ffn, tm):
    nw_in, w_in, w_out, layer, nw_out = ffn
    m = h.shape[0]
    d_ff = w_out.shape[1]
    assert d_ff % FFN_CHUNK == 0
    from_latent = w_uv is not None
    row = lambda i: (i, 0)
    if from_latent:
        in_specs = [pl.BlockSpec((MLA_HEADS, tm, KV_LORA), lambda i: (0, i, 0)),
                    _const_spec(w_uv.shape)]
        args = [a16, w_uv]
    else:
        in_specs = [pl.BlockSpec((tm, a16.shape[1]), row)]
        args = [a16]
    in_specs += [
        _const_spec(w_o.shape),
        pl.BlockSpec((tm, D_MODEL), row),
        _const_spec((1, D_MODEL)),
        _const_spec((1, D_MODEL)),
        _fixed_spec((1,) + w_in.shape[1:], (layer, 0, 0)),
        _fixed_spec((1,) + w_out.shape[1:], (layer, 0, 0)),
        _const_spec((1, D_MODEL)),
    ]
    args += [w_o, h, nw_mix, nw_in, w_in, w_out, nw_out]
    return pl.pallas_call(
        functools.partial(_layer_tail_body, from_latent=from_latent, d_ff=d_ff, chunk=FFN_CHUNK),
        grid=(m // tm,),
        in_specs=in_specs,
        out_specs=pl.BlockSpec((tm, D_MODEL), row),
        out_shape=jax.ShapeDtypeStruct((m, D_MODEL), F32),
        compiler_params=_params("parallel"),
        name="layer_tail",
    )(*args)


def _dn_proj_body(*refs, tm, seq_len, has_hist, tail_rows):
    if has_hist:
        (x_ref, nw_ref, wqkv_ref, wz_ref, wba_ref, cw_ref, alog_ref, dtb_ref, hist_ref,
         qkv_ref, z_ref, bg_ref, tail_ref, ext_ref) = refs
    else:
        (x_ref, nw_ref, wqkv_ref, wz_ref, wba_ref, cw_ref, alog_ref, dtb_ref,
         qkv_ref, z_ref, bg_ref, tail_ref, ext_ref) = refs
        hist_ref = None
    i = pl.program_id(0)
    carry = CONV_CARRY_ROWS

    @pl.when((i * tm) % seq_len == 0)
    def _():
        ext_ref[0:carry, :] = jnp.zeros((carry, DN_QKV), F32)

    u = _rms(x_ref[...], nw_ref[...]).astype(BF16)
    assert carry == SUBLANES and tm % (DN_PROJ_ROW_CHAINS * SUBLANES) == 0
    rows_per = tm // DN_PROJ_ROW_CHAINS
    assert rows_per % seq_len == 0 or not has_hist
    raws = []
    for r in range(DN_PROJ_ROW_CHAINS):
        raws.append(_dot(u[r * rows_per:(r + 1) * rows_per], wqkv_ref[...]))
        if r == DN_PROJ_GATE_SLOT:
            z_ref[...] = _dot(u, wz_ref[...])
            ba = _dot(u, wba_ref[...])
    sub = lax.broadcasted_iota(jnp.int32, (1, SUBLANES, 1), 1)
    cw = cw_ref[...]
    before = ext_ref[...]
    for r, raw in enumerate(raws):
        rows = slice(r * rows_per, (r + 1) * rows_per)
        cur = raw.reshape(rows_per // SUBLANES, SUBLANES, DN_QKV)
        prev = jnp.concatenate([before.reshape(1, SUBLANES, DN_QKV), cur[:-1]], axis=0)
        acc = raw * cw[CONV_W - 1:CONV_W, :]
        for k in range(1, CONV_W):
            mixed = jnp.where(sub >= SUBLANES - k, prev, cur)
            shifted = pltpu.roll(mixed, k, axis=1).reshape(rows_per, DN_QKV)
            if has_hist:
                tpos = lax.broadcasted_iota(jnp.int32, (rows_per, 1), 0) % seq_len
                shifted = jnp.where(tpos >= k, shifted, hist_ref[k - 1, rows, :])
            acc = acc + shifted * cw[CONV_W - 1 - k:CONV_W - k, :]
        before = raw[rows_per - carry:, :]
        act = _silu(acc)
        for h in range(DN_HEADS):
            qh = act[:, h * DN_DK:(h + 1) * DN_DK]
            kh = act[:, DN_HK + h * DN_DK:DN_HK + (h + 1) * DN_DK]
            qn = qh * lax.rsqrt(jnp.sum(qh * qh, axis=-1, keepdims=True) + L2_EPS) * (DN_DK ** -0.5)
            kn = kh * lax.rsqrt(jnp.sum(kh * kh, axis=-1, keepdims=True) + L2_EPS)
            qkv_ref[rows, h * DN_DK:(h + 1) * DN_DK] = qn
            qkv_ref[rows, DN_HK + h * DN_DK:DN_HK + (h + 1) * DN_DK] = kn
        qkv_ref[rows, 2 * DN_HK:] = act[:, 2 * DN_HK:]
    ext_ref[...] = before
    tail_ref[0] = jnp.concatenate(raws, axis=0)[tm - tail_rows:, :]
    beta = _sigmoid(ba[:, :DN_HEADS])
    g = -jnp.exp(alog_ref[...]) * _softplus(ba[:, DN_HEADS:2 * DN_HEADS] + dtb_ref[...])
    bg_ref[:, :DN_HEADS] = beta
    bg_ref[:, DN_HEADS:] = g


def _dn_project(x, nw, w_all, w_ba, conv_w, a_log, dt_bias, hist, tm, seq_len, tail_rows):
    assert DN_QKV % DN_Z == 0
    m = x.shape[0]
    n_tiles = m // tm
    has_hist = hist is not None
    assert (tm % seq_len == 0 and n_tiles == 1) if has_hist else seq_len % tm == 0
    tiles_per_tail = max(seq_len // tm, 1)
    row = lambda i: (i, 0)
    in_specs = [
        pl.BlockSpec((tm, D_MODEL), row),
        _const_spec((1, D_MODEL)),
        _fixed_spec((D_MODEL, DN_QKV), (0, 0)),
        _fixed_spec((D_MODEL, DN_Z), (0, DN_QKV // DN_Z)),
        _const_spec(w_ba.shape),
        _const_spec(conv_w.shape),
        _const_spec((1, DN_HEADS)),
        _const_spec((1, DN_HEADS)),
    ]
    args = [x, nw, w_all, w_all, w_ba, conv_w, a_log, dt_bias]
    if has_hist:
        in_specs.append(pl.BlockSpec((CONV_W - 1, tm, DN_QKV), lambda i: (0, i, 0)))
        args.append(hist)
    return pl.pallas_call(
        functools.partial(_dn_proj_body, tm=tm, seq_len=seq_len, has_hist=has_hist,
                          tail_rows=tail_rows),
        grid=(n_tiles,),
        in_specs=in_specs,
        out_specs=[
            pl.BlockSpec((tm, DN_QKV), row),
            pl.BlockSpec((tm, DN_Z), row),
            pl.BlockSpec((tm, 2 * DN_HEADS), row),
            pl.BlockSpec((1, tail_rows, DN_QKV), lambda i: (i // tiles_per_tail, 0, 0)),
        ],
        out_shape=[
            jax.ShapeDtypeStruct((m, DN_QKV), F32),
            jax.ShapeDtypeStruct((m, DN_Z), F32),
            jax.ShapeDtypeStruct((m, 2 * DN_HEADS), F32),
            jax.ShapeDtypeStruct((n_tiles // tiles_per_tail, tail_rows, DN_QKV), F32),
        ],
        scratch_shapes=[pltpu.VMEM((CONV_CARRY_ROWS, DN_QKV), F32)],
        compiler_params=_params("arbitrary"),
        name="dn_project",
    )(*args)


def _stack_heads(load, g):
    return jnp.concatenate(
        [load(slice((g * GDN_GROUP + hh) * DN_DK, (g * GDN_GROUP + hh + 1) * DN_DK))
         for hh in range(GDN_GROUP)], axis=0)


def _gdn_local_body(q_ref, k_ref, v_ref, bg_ref, bgt_ref,
                    u_ref, w_ref, qd_ref, kdt_ref, qk_ref, egl_ref, *, chunk, chunks_per_step):
    n_heads = DN_HEADS
    stack = GDN_GROUP * chunk
    ri = lax.broadcasted_iota(jnp.int32, (stack, stack), 0)
    ci = lax.broadcasted_iota(jnp.int32, (stack, stack), 1)
    same_head = (ri // chunk) == (ci // chunk)
    causal = same_head & (ri >= ci)
    strict = same_head & (ri > ci)
    r1 = lax.broadcasted_iota(jnp.int32, (chunk, chunk), 0)
    c1 = lax.broadcasted_iota(jnp.int32, (chunk, chunk), 1)
    tril = jnp.where(r1 >= c1, 1.0, 0.0).astype(BF16)
    triu = jnp.where(r1 <= c1, 1.0, 0.0).astype(BF16)
    n_double = int(math.log2(chunk)) - 2

    def store_heads(ref, rows, g, stacked):
        for hh in range(GDN_GROUP):
            h = g * GDN_GROUP + hh
            ref[rows, h * DN_DK:(h + 1) * DN_DK] = stacked[hh * chunk:(hh + 1) * chunk, :]

    chains = []
    for cp in range(chunks_per_step):
        rows = slice(cp * chunk, (cp + 1) * chunk)
        bg = bg_ref[rows, :]
        bgt = bgt_ref[cp]
        gc_col = sum(_dot(tril, part) for part in _split3(bg))
        gc_row = sum(_dot(part, triu) for part in _split3(bgt))
        egl_ref[cp] = jnp.broadcast_to(jnp.exp(gc_row[n_heads:, chunk - 1:chunk]),
                                       (n_heads, LANES))
        for g in range(n_heads // GDN_GROUP):
            heads = [g * GDN_GROUP + hh for hh in range(GDN_GROUP)]
            q = _stack_heads(lambda sl: q_ref[rows, sl], g)
            k = _stack_heads(lambda sl: k_ref[rows, sl], g)
            v = _stack_heads(lambda sl: v_ref[rows, sl], g)
            beta = jnp.concatenate([bg[:, h:h + 1] for h in heads], axis=0)
            gcc = jnp.concatenate([gc_col[:, n_heads + h:n_heads + h + 1] for h in heads], axis=0)
            gcr = jnp.concatenate([gc_row[n_heads + h:n_heads + h + 1, :] for h in heads], axis=1)
            g_last = jnp.concatenate(
                [jnp.broadcast_to(gc_col[chunk - 1:chunk, n_heads + h:n_heads + h + 1], (chunk, 1))
                 for h in heads], axis=0)
            eg = jnp.exp(gcc)
            kb = k * beta
            store_heads(qd_ref, rows, g, (q * eg).astype(BF16))
            kdt_ref[cp, :, g * stack:(g + 1) * stack] = (k * jnp.exp(g_last - gcc)).T.astype(BF16)
            chains.append(dict(
                rows=rows, g=g, q16=q.astype(BF16), k16=k.astype(BF16), kb16=kb.astype(BF16),
                decay=jnp.where(causal, jnp.exp(gcc - gcr), 0.0),
                rhs=jnp.concatenate([v * beta, kb * eg], axis=1)))

    for ch in chains:
        lower = jnp.where(strict, _dot_nt(ch["kb16"], ch["k16"]) * ch["decay"], 0.0)
        ch.update(lower=lower, power=lower, off=-lower)
    for ch in chains:
        qk = jnp.where(causal, _dot_nt(ch["q16"], ch["k16"]) * ch["decay"], 0.0)
        qk_packed = sum(qk[hh * chunk:(hh + 1) * chunk, :] for hh in range(GDN_GROUP))
        qk_ref[ch["rows"], ch["g"] * stack:(ch["g"] + 1) * stack] = qk_packed.astype(BF16)
    for _ in range(n_double):
        for ch in chains:
            p16 = ch["power"].astype(BF16)
            ch["power"] = _dot(p16, p16)
        for ch in chains:
            ch["off"] = ch["off"] + ch["power"] + _dot(ch["off"].astype(BF16),
                                                       ch["power"].astype(BF16))
    for ch in chains:
        ch["resid"] = (ch["lower"] + ch["off"]) + _dot_hl(ch["lower"], ch["off"])
    for ch in chains:
        ch["off"] = ch["off"] - ch["resid"] - _dot(ch["off"].astype(BF16),
                                                   ch["resid"].astype(BF16))
    for ch in chains:
        sol = ch["rhs"] + _dot(ch["off"].astype(BF16), ch["rhs"].astype(BF16))
        store_heads(u_ref, ch["rows"], ch["g"], sol[:, :DN_DV])
        store_heads(w_ref, ch["rows"], ch["g"], sol[:, DN_DV:].astype(BF16))


def _gdn_scan_body(u_ref, w_ref, qd_ref, kdt_ref, qk_ref, egl_ref, z_ref, gout_ref, s0_ref,
                   y_ref, s_ref, *, chunk, seqs):
    c = pl.program_id(1)

    @pl.when(c == 0)
    def _():
        s_ref[...] = s0_ref[...]

    stack = GDN_GROUP * chunk
    pair = 2 * DN_DK
    lhs_mask = ((lax.broadcasted_iota(jnp.int32, (4 * chunk, pair), 0) // chunk) % 2
                == lax.broadcasted_iota(jnp.int32, (4 * chunk, pair), 1) // DN_DK)
    qk_mask = (lax.broadcasted_iota(jnp.int32, (stack, stack), 0) // chunk
               == lax.broadcasted_iota(jnp.int32, (stack, stack), 1) // chunk)
    kd_mask = (lax.broadcasted_iota(jnp.int32, (GDN_GROUP * DN_DK, stack), 0) // DN_DK
               == lax.broadcasted_iota(jnp.int32, (GDN_GROUP * DN_DK, stack), 1) // chunk)
    zero16 = jnp.zeros((), BF16)
    chains = [dict(sq=sq, g=g) for sq in range(seqs) for g in range(DN_HEADS // GDN_GROUP)]
    for ch in chains:
        sq, g = ch["sq"], ch["g"]
        s = s_ref[sq, g * GDN_GROUP:(g + 1) * GDN_GROUP].reshape(GDN_GROUP * DN_DK, DN_DV)
        s16 = s.astype(BF16)
        ws, qs = [], []
        for p in range(GDN_GROUP // 2):
            cols = [slice((g * GDN_GROUP + 2 * p + e) * DN_DK, (g * GDN_GROUP + 2 * p + e + 1) * DN_DK)
                    for e in range(2)]
            lhs = jnp.concatenate([w_ref[sq, :, cols[0]], w_ref[sq, :, cols[1]],
                                   qd_ref[sq, :, cols[0]], qd_ref[sq, :, cols[1]]], axis=0)
            lhs_bd = jnp.where(lhs_mask, jnp.concatenate([lhs, lhs], axis=1), zero16)
            res = _dot(lhs_bd, s16[p * pair:(p + 1) * pair, :])
            ws.append(res[:2 * chunk])
            qs.append(res[2 * chunk:])
        ch.update(s=s, ws=jnp.concatenate(ws, axis=0), qs=jnp.concatenate(qs, axis=0))
    for ch in chains:
        sq, g = ch["sq"], ch["g"]
        v_new = _stack_heads(lambda sl: u_ref[sq, :, sl], g) - ch["ws"]
        ch["v16"] = v_new.astype(BF16)
    gout = gout_ref[...]
    for ch in chains:
        sq, g = ch["sq"], ch["g"]
        qk = qk_ref[sq, :, g * stack:(g + 1) * stack]
        qk_bd = jnp.where(qk_mask, jnp.concatenate([qk] * GDN_GROUP, axis=0), zero16)
        o = ch["qs"] + _dot(qk_bd, ch["v16"])
        y = _rms(o, gout) * _silu(_stack_heads(lambda sl: z_ref[sq, :, sl], g))
        y16 = y.astype(BF16)
        for hh in range(GDN_GROUP):
            h = g * GDN_GROUP + hh
            y_ref[sq, :, h * DN_DV:(h + 1) * DN_DV] = y16[hh * chunk:(hh + 1) * chunk, :]
    for ch in chains:
        sq, g = ch["sq"], ch["g"]
        kdt = kdt_ref[sq, 0, :, g * stack:(g + 1) * stack]
        kdt_bd = jnp.where(kd_mask, jnp.concatenate([kdt] * GDN_GROUP, axis=0), zero16)
        gate = jnp.concatenate(
            [jnp.broadcast_to(egl_ref[sq, 0, g * GDN_GROUP + hh:g * GDN_GROUP + hh + 1, :],
                              (DN_DK, DN_DV)) for hh in range(GDN_GROUP)], axis=0)
        s_new = ch["s"] * gate + _dot(kdt_bd, ch["v16"])
        s_ref[sq, g * GDN_GROUP:(g + 1) * GDN_GROUP] = s_new.reshape(GDN_GROUP, DN_DK, DN_DV)


def _gdn(qkv, bg, z, g_out, s0, n_seq, seq_len, chunk):
    m = qkv.shape[0]
    n_chunks = m // chunk
    nc = seq_len // chunk
    cps = GDN_CHUNKS_PER_STEP
    sps = GDN_SEQS_PER_STEP
    assert n_chunks % cps == 0 and n_seq % sps == 0
    bgt = bg.reshape(n_chunks, chunk, 2 * DN_HEADS).transpose(0, 2, 1)
    half = DN_HEADS * chunk
    row = lambda i: (i, 0)
    blk = lambda col: pl.BlockSpec((cps * chunk, DN_HK), lambda i, col=col: (i, col))
    u, w16, qd16, kdt16, qk16, egl = pl.pallas_call(
        functools.partial(_gdn_local_body, chunk=chunk, chunks_per_step=cps),
        grid=(n_chunks // cps,),
        in_specs=[
            blk(0), blk(1), blk(2),
            pl.BlockSpec((cps * chunk, 2 * DN_HEADS), row),
            pl.BlockSpec((cps, 2 * DN_HEADS, chunk), lambda i: (i, 0, 0)),
        ],
        out_specs=[
            pl.BlockSpec((cps * chunk, DN_Z), row),
            pl.BlockSpec((cps * chunk, DN_HK), row),
            pl.BlockSpec((cps * chunk, DN_HK), row),
            pl.BlockSpec((cps, DN_DK, half), lambda i: (i, 0, 0)),
            pl.BlockSpec((cps * chunk, half), row),
            pl.BlockSpec((cps, DN_HEADS, LANES), lambda i: (i, 0, 0)),
        ],
        out_shape=[
            jax.ShapeDtypeStruct((m, DN_Z), F32),
            jax.ShapeDtypeStruct((m, DN_HK), BF16),
            jax.ShapeDtypeStruct((m, DN_HK), BF16),
            jax.ShapeDtypeStruct((n_chunks, DN_DK, half), BF16),
            jax.ShapeDtypeStruct((m, half), BF16),
            jax.ShapeDtypeStruct((n_chunks, DN_HEADS, LANES), F32),
        ],
        compiler_params=_params("parallel"),
        name="gdn_local",
    )(qkv, qkv, qkv, bg, bgt)

    per_seq = lambda a: a.reshape((n_seq, nc if a.ndim == 3 else seq_len) + a.shape[1:])
    tok_spec = lambda width: pl.BlockSpec((sps, chunk, width), lambda b, c: (b, c, 0))
    lead_spec = lambda d1, d2: pl.BlockSpec((sps, 1, d1, d2), lambda b, c: (b, c, 0, 0))
    state_spec = pl.BlockSpec((sps, DN_HEADS, DN_DK, DN_DV), lambda b, c: (b, 0, 0, 0))
    y16, s = pl.pallas_call(
        functools.partial(_gdn_scan_body, chunk=chunk, seqs=sps),
        grid=(n_seq // sps, nc),
        in_specs=[
            tok_spec(DN_Z),
            tok_spec(DN_HK),
            tok_spec(DN_HK),
            lead_spec(DN_DK, half),
            tok_spec(half),
            lead_spec(DN_HEADS, LANES),
            tok_spec(DN_Z),
            pl.BlockSpec((1, DN_DV), lambda b, c: (0, 0)),
            state_spec,
        ],
        out_specs=[tok_spec(DN_Z), state_spec],
        out_shape=[
            jax.ShapeDtypeStruct((n_seq, seq_len, DN_Z), BF16),
            jax.ShapeDtypeStruct(s0.shape, F32),
        ],
        compiler_params=_params("parallel", "arbitrary"),
        name="gdn_scan",
    )(per_seq(u), per_seq(w16), per_seq(qd16), per_seq(kdt16), per_seq(qk16), per_seq(egl),
      per_seq(z), g_out, s0)
    return y16.reshape(m, DN_Z), s


def _mla_layer(hp, hs, cache_mla, page_table, slot, nw, w_in, g_q, g_kv, w_uq, w_uk, w_uv, w_o,
               ffn, bp, tp, bs, ts):
    past = page_table.shape[1] * PAGE_SIZE
    w_in16 = w_in.astype(BF16)
    w_uq_heads = w_uq.reshape(Q_LORA, MLA_HEADS, QK_NOPE + QK_ROPE)
    w_uq16 = jnp.concatenate(
        [w_uq_heads[:, :, :QK_NOPE].reshape(Q_LORA, MLA_HEADS * QK_NOPE),
         w_uq_heads[:, :, QK_NOPE:].reshape(Q_LORA, MLA_HEADS * QK_ROPE)], axis=1).astype(BF16)
    w_ukt16 = jnp.swapaxes(w_uk, 1, 2).astype(BF16)
    w_uv16 = w_uv.astype(BF16)
    w_o16 = w_o.astype(BF16)
    g_q = g_q.reshape(1, Q_LORA)
    g_kv = g_kv.reshape(1, KV_LORA)
    nw0 = nw[0].reshape(1, D_MODEL)
    nw1 = nw[1].reshape(1, D_MODEL)

    cs_p, sn_p = _rope_tables(jnp.arange(tp))
    cs_s, sn_s = _rope_tables(past + jnp.tile(jnp.arange(ts), bs))
    ms = bs * ts
    rows_p, kv_p, q_p = _mla_project(hp, nw0, w_in16, g_q, g_kv, w_uq16, w_ukt16, cs_p, sn_p,
                                     TM_TOKENS)
    rows_s, kv_s, q_s = _mla_project(hs, nw0, w_in16, g_q, g_kv, w_uq16, w_ukt16, cs_s, sn_s, ms)

    o_p = _attn_prompt(q_p, kv_p, bp, tp, TQ_ATTN)

    q_sb = q_s.reshape(MLA_HEADS, bs, ts, MLA_ROW).transpose(1, 0, 2, 3).reshape(
        bs, MLA_HEADS * ts, MLA_ROW)
    new_rows = 2 * SUBLANES
    k_new = jnp.pad(kv_s.reshape(bs, ts, MLA_ROW), ((0, 0), (0, new_rows - ts), (0, 0)))
    o_sb = _attn_decode(q_sb, k_new, jnp.swapaxes(cache_mla, 2, 3), page_table, slot, ts)
    o_s = o_sb.reshape(bs, MLA_HEADS, ts, KV_LORA).transpose(1, 0, 2, 3).reshape(
        MLA_HEADS, ms, KV_LORA)

    hp = _layer_tail(o_p, w_uv16, w_o16, hp, nw1, ffn, TM_TOKENS)
    hs = _layer_tail(o_s, w_uv16, w_o16, hs, nw1, ffn, ms)
    return hp, hs, rows_p.reshape(bp, tp, MLA_ROW), rows_s.reshape(bs, ts, MLA_ROW)


def _dn_layer(hp, hs, s0_s, conv0_s, nw, w_in, conv_w, a_log, dt_bias, g_out, w_o, ffn,
              bp, tp, bs, ts):
    w_in16 = w_in.astype(BF16)
    w_ba16 = jnp.pad(w_in[:, DN_QKV + DN_Z:], ((0, 0), (0, LANES - 2 * DN_HEADS))).astype(BF16)
    w_o16 = w_o.astype(BF16)
    a_log = a_log.reshape(1, DN_HEADS).astype(F32)
    dt_bias = dt_bias.reshape(1, DN_HEADS).astype(F32)
    g_out = g_out.reshape(1, DN_DV)
    nw0 = nw[0].reshape(1, D_MODEL)
    nw1 = nw[1].reshape(1, D_MODEL)
    hist_rows = CONV_W - 1
    ms = bs * ts

    qkv_p, z_p, bg_p, tail_p = _dn_project(hp, nw0, w_in16, w_ba16, conv_w, a_log, dt_bias,
                                           None, TM_DN_PROJ, tp, SUBLANES)
    conv_p = tail_p[:, SUBLANES - hist_rows:, :]
    s0_p = jnp.zeros((bp, DN_HEADS, DN_DK, DN_DV), F32)
    y_p, s_p = _gdn(qkv_p, bg_p, z_p, g_out, s0_p, bp, tp, DN_CHUNK)
    hp = _layer_tail(y_p, None, w_o16, hp, nw1, ffn, TM_TOKENS)

    tok = jnp.arange(ts)
    hist = jnp.stack([
        conv0_s[:, jnp.clip(hist_rows - k + tok, 0, hist_rows - 1), :].reshape(ms, DN_QKV)
        for k in range(1, CONV_W)
    ]).astype(F32)
    qkv_s, z_s, bg_s, tail_s = _dn_project(hs, nw0, w_in16, w_ba16, conv_w, a_log, dt_bias,
                                           hist, ms, ts, ms)
    raw_s = tail_s.reshape(bs, ts, DN_QKV)
    conv_s = jnp.concatenate([conv0_s.astype(F32), raw_s], axis=1)[:, ts:, :]
    cs = DN_CHUNK_SAMPLE
    assert ts <= cs
    pad = lambda a: jnp.pad(a.reshape(bs, ts, -1), ((0, 0), (0, cs - ts), (0, 0))).reshape(
        bs * cs, -1)
    y_s_pad, s_s = _gdn(pad(qkv_s), pad(bg_s), pad(z_s), g_out, s0_s.astype(F32), bs, cs, cs)
    y_s = y_s_pad.reshape(bs, cs, DN_Z)[:, :ts, :].reshape(ms, DN_Z)
    hs = _layer_tail(y_s, None, w_o16, hs, nw1, ffn, ms)
    return hp, hs, s_p, s_s, conv_p, conv_s


def kernel(x_prompt, x_sample, cache_mla, state_dn, state_dn_conv, page_table, norm_w, mla_w_in,
           mla_g_q, mla_g_kv, mla_w_uq, mla_w_uk, mla_w_uv, mla_w_o, dn_w_in, dn_conv_w, dn_a_log,
           dn_dt_bias, dn_g_out, dn_w_o, ffn_w_in, ffn_w_out):
    bp, tp, _ = x_prompt.shape
    bs, ts, _ = x_sample.shape
    depth = norm_w.shape[0]
    hp = x_prompt.reshape(bp * tp, D_MODEL)
    hs = x_sample.reshape(bs * ts, D_MODEL)
    rows_p_l, rows_s_l, sp_l, ss_l, cp_l, cs_l = [], [], [], [], [], []
    ffn_w_in16 = ffn_w_in.astype(BF16)
    ffn_w_out16 = ffn_w_out.astype(BF16)
    for layer in range(depth):
        j = layer // N_MIXERS
        nw = norm_w[layer]
        ffn = (nw[2].reshape(1, D_MODEL), ffn_w_in16, ffn_w_out16, layer, nw[3].reshape(1, D_MODEL))
        if layer % N_MIXERS == 0:
            hp, hs, rows_p, rows_s = _mla_layer(
                hp, hs, cache_mla, page_table, j, nw, mla_w_in[j], mla_g_q[j], mla_g_kv[j],
                mla_w_uq[j], mla_w_uk[j], mla_w_uv[j], mla_w_o[j], ffn, bp, tp, bs, ts)
            rows_p_l.append(rows_p)
            rows_s_l.append(rows_s)
        else:
            hp, hs, s_p, s_s, c_p, c_s = _dn_layer(
                hp, hs, state_dn[j], state_dn_conv[j], nw, dn_w_in[j], dn_conv_w[j], dn_a_log[j],
                dn_dt_bias[j], dn_g_out[j], dn_w_o[j], ffn, bp, tp, bs, ts)
            sp_l.append(s_p.astype(state_dn.dtype))
            ss_l.append(s_s.astype(state_dn.dtype))
            cp_l.append(c_p.astype(state_dn_conv.dtype))
            cs_l.append(c_s.astype(state_dn_conv.dtype))
    return (hp.reshape(bp, tp, D_MODEL), hs.reshape(bs, ts, D_MODEL),
            jnp.stack(rows_p_l), jnp.stack(rows_s_l), jnp.stack(sp_l), jnp.stack(ss_l),
            jnp.stack(cp_l), jnp.stack(cs_l))
```

```python
import functools
import math

import jax
import jax.numpy as jnp
from jax import lax
from jax.experimental import pallas as pl
from jax.experimental.pallas import tpu as pltpu

F32 = jnp.float32
BF16 = jnp.bfloat16

D_MODEL = 1024
PAGE_SIZE = 128
N_MIXERS = 2

MLA_HEADS = 8
QK_NOPE = 128
QK_ROPE = 64
V_HEAD = 128
KV_LORA = 256
Q_LORA = 384
MLA_ROW = KV_LORA + QK_ROPE
MLA_SCALE = (QK_NOPE + QK_ROPE) ** -0.5
ROPE_THETA = 10000.0

DN_HEADS = 8
DN_DK = 128
DN_DV = 128
CONV_W = 4
DN_HK = DN_HEADS * DN_DK
DN_QKV = DN_HEADS * (2 * DN_DK + DN_DV)
DN_Z = DN_HEADS * DN_DV

RMS_EPS = 1e-6
L2_EPS = 1e-6

LANES = 128
SUBLANES = 8
VMEM_LIMIT_BYTES = 56 * 1024 * 1024

TM_TOKENS = 512
MLA_PROJ_ROW_CHAINS = 4
TAIL_ROW_CHAINS = 2
DN_PROJ_ROW_CHAINS = 4
DN_PROJ_GATE_SLOT = 0
TM_DN_PROJ = 512
TQ_ATTN = 256
DN_CHUNK = 64
DN_CHUNK_SAMPLE = 16
GDN_GROUP = 4
GDN_CHUNKS_PER_STEP = 4
GDN_SEQS_PER_STEP = 4
DEC_PAGES_PER_STEP = 64
DEC_PAGES_PER_GROUP = 8
FFN_CHUNK = 256
CONV_CARRY_ROWS = SUBLANES


def _params(*sem):
    return pltpu.CompilerParams(dimension_semantics=sem, vmem_limit_bytes=VMEM_LIMIT_BYTES)


def _fixed_spec(block_shape, block_index):
    return pl.BlockSpec(block_shape, lambda *_: block_index, pipeline_mode=pl.Buffered(1))


def _const_spec(shape):
    return _fixed_spec(shape, (0,) * len(shape))


def _dot(a, b):
    return jnp.dot(a, b, preferred_element_type=F32)


def _dot_nt(a, b):
    return lax.dot_general(a, b, (((1,), (1,)), ((), ())), preferred_element_type=F32)


def _dot_tn(a, b):
    return lax.dot_general(a, b, (((0,), (0,)), ((), ())), preferred_element_type=F32)


def _split2(x):
    hi = x.astype(BF16)
    lo = (x - hi.astype(F32)).astype(BF16)
    return hi, lo


def _split3(x):
    hi = x.astype(BF16)
    r = x - hi.astype(F32)
    mid = r.astype(BF16)
    lo = (r - mid.astype(F32)).astype(BF16)
    return hi, mid, lo


def _dot_hl(a, b):
    ah, al = _split2(a)
    bh, bl = _split2(b)
    return _dot(ah, bh) + (_dot(ah, bl) + _dot(al, bh))


def _rms(x, w):
    return x * lax.rsqrt(jnp.mean(x * x, axis=-1, keepdims=True) + RMS_EPS) * w


def _sigmoid(x):
    return 1.0 / (1.0 + jnp.exp(-x))


def _silu(x):
    return x * _sigmoid(x)


def _softplus(x):
    return jnp.maximum(x, 0.0) + jnp.log1p(jnp.exp(-jnp.abs(x)))


def _rope(x, cs, sn):
    half = x.shape[-1] // 2
    swapped = jnp.concatenate([x[:, half:], x[:, :half]], axis=1)
    return x * cs + swapped * sn


def _rope_tables(pos):
    half = QK_ROPE // 2
    freq = ROPE_THETA ** (-jnp.arange(half, dtype=F32) / half)
    ang = pos.astype(F32)[:, None] * freq[None, :]
    cos, sin = jnp.cos(ang), jnp.sin(ang)
    return jnp.concatenate([cos, cos], axis=1), jnp.concatenate([-sin, sin], axis=1)


def _mla_proj_body(x_ref, nw_ref, win_ref, gq_ref, gkv_ref, wuq_ref, wukt_ref, cs_ref, sn_ref,
                   rows_ref, kv_ref, q_ref):
    tm = x_ref.shape[0]
    n_chains = MLA_PROJ_ROW_CHAINS if tm % (MLA_PROJ_ROW_CHAINS * 2 * SUBLANES) == 0 else 1
    rows_per = tm // n_chains
    chains = [dict(rows=slice(r * rows_per, (r + 1) * rows_per)) for r in range(n_chains)]
    rope_base = MLA_HEADS * QK_NOPE
    for ch in chains:
        u = _rms(x_ref[ch["rows"], :], nw_ref[...]).astype(BF16)
        ch["a"] = _dot(u, win_ref[...])
    for ch in chains:
        rows, a = ch["rows"], ch["a"]
        ch["cs"] = cs_ref[rows, :]
        ch["sn"] = sn_ref[rows, :]
        c_q = _rms(a[:, :Q_LORA], gq_ref[...]).astype(BF16)
        ch["q"] = _dot(c_q, wuq_ref[...])
    for ch in chains:
        rows, a = ch["rows"], ch["a"]
        c_kv = _rms(a[:, Q_LORA:Q_LORA + KV_LORA], gkv_ref[...])
        k_r = _rope(a[:, Q_LORA + KV_LORA:], ch["cs"], ch["sn"])
        rows_ref[rows, :KV_LORA] = c_kv
        rows_ref[rows, KV_LORA:] = k_r
        kv_ref[rows, :KV_LORA] = c_kv.astype(BF16)
        kv_ref[rows, KV_LORA:] = k_r.astype(BF16)
    for h in range(MLA_HEADS):
        for ch in chains:
            rows, q = ch["rows"], ch["q"]
            q_lat = _dot(q[:, h * QK_NOPE:(h + 1) * QK_NOPE].astype(BF16), wukt_ref[h])
            q_rope = _rope(q[:, rope_base + h * QK_ROPE:rope_base + (h + 1) * QK_ROPE],
                           ch["cs"], ch["sn"])
            q_ref[h, rows, :KV_LORA] = (q_lat * MLA_SCALE).astype(BF16)
            q_ref[h, rows, KV_LORA:] = (q_rope * MLA_SCALE).astype(BF16)


def _mla_project(x, nw, w_in, g_q, g_kv, w_uq, w_ukt, cs, sn, tm):
    m = x.shape[0]
    period_tiles = cs.shape[0] // tm
    assert cs.shape[0] % tm == 0 and m % cs.shape[0] == 0
    row = lambda i: (i, 0)
    pos_row = lambda i: (i % period_tiles, 0)
    return pl.pallas_call(
        _mla_proj_body,
        grid=(m // tm,),
        in_specs=[
            pl.BlockSpec((tm, D_MODEL), row),
            _const_spec((1, D_MODEL)),
            _const_spec(w_in.shape),
            _const_spec((1, Q_LORA)),
            _const_spec((1, KV_LORA)),
            _const_spec(w_uq.shape),
            _const_spec(w_ukt.shape),
            pl.BlockSpec((tm, QK_ROPE), pos_row),
            pl.BlockSpec((tm, QK_ROPE), pos_row),
        ],
        out_specs=[
            pl.BlockSpec((tm, MLA_ROW), row),
            pl.BlockSpec((tm, MLA_ROW), row),
            pl.BlockSpec((MLA_HEADS, tm, MLA_ROW), lambda i: (0, i, 0)),
        ],
        out_shape=[
            jax.ShapeDtypeStruct((m, MLA_ROW), F32),
            jax.ShapeDtypeStruct((m, MLA_ROW), BF16),
            jax.ShapeDtypeStruct((MLA_HEADS, m, MLA_ROW), BF16),
        ],
        compiler_params=_params("parallel"),
        name="mla_project",
    )(x, nw, w_in, g_q, g_kv, w_uq, w_ukt, cs, sn)


def _softmax_update(s, v, m_ref, l_ref, acc_ref, v_transposed=False):
    m_prev = m_ref[...]
    m_new = jnp.maximum(m_prev, jnp.max(s, axis=1, keepdims=True))
    alpha = jnp.exp(m_prev - m_new)
    p = jnp.exp(s - m_new)
    p16 = p.astype(BF16)
    pv = _dot_nt(p16, v) if v_transposed else _dot(p16, v)
    l_ref[...] = alpha * l_ref[...] + jnp.sum(p, axis=1, keepdims=True)
    acc_ref[...] = alpha * acc_ref[...] + pv
    m_ref[...] = m_new


def _softmax_init(m_ref, l_ref, acc_ref):
    m_ref[...] = jnp.full(m_ref.shape, -jnp.inf, F32)
    l_ref[...] = jnp.zeros(l_ref.shape, F32)
    acc_ref[...] = jnp.zeros(acc_ref.shape, F32)


def _attn_prompt_body(q_ref, kv_ref, kvt_ref, o_ref, m_ref, l_ref, acc_ref, sa_ref, sb_ref, *, tq):
    i = pl.program_id(1)
    heads = q_ref.shape[0]
    cols = heads * tq
    q = q_ref[...].reshape(cols, MLA_ROW)

    def scores(j, buf):
        k = kv_ref[pl.ds(pl.multiple_of(j * tq, tq), tq), :]
        buf[...] = _dot_nt(k, q)

    def absorb(j, buf, masked):
        st = buf[...]
        if masked:
            key = lax.broadcasted_iota(jnp.int32, (tq, cols), 0)
            tok = lax.broadcasted_iota(jnp.int32, (tq, cols), 1) % tq
            st = jnp.where(key <= tok, st, -jnp.inf)
        m_prev = m_ref[...]
        m_new = jnp.maximum(m_prev, jnp.max(st, axis=0, keepdims=True))
        alpha = jnp.exp(m_prev - m_new)
        p = jnp.exp(st - m_new)
        l_ref[...] = alpha * l_ref[...] + jnp.sum(p, axis=0, keepdims=True)
        acc_ref[...] = alpha * acc_ref[...] + _dot(kvt_ref[j], p.astype(BF16))
        m_ref[...] = m_new

    scores(0, sa_ref)
    _softmax_init(m_ref, l_ref, acc_ref)

    def body(p, carry):
        scores(2 * p + 1, sb_ref)
        absorb(2 * p, sa_ref, False)
        scores(2 * p + 2, sa_ref)
        absorb(2 * p + 1, sb_ref, False)
        return carry

    lax.fori_loop(0, i // 2, body, 0)

    @pl.when(i % 2 == 0)
    def _():
        absorb(i, sa_ref, True)

    @pl.when(i % 2 == 1)
    def _():
        scores(i, sb_ref)
        absorb(i - 1, sa_ref, False)
        absorb(i, sb_ref, True)

    o = (acc_ref[...] / l_ref[...]).T
    o_ref[...] = o.reshape(heads, tq, KV_LORA).astype(BF16)


def _attn_prompt(q, kv, batch, seq, tq):
    heads, m, _ = q.shape
    nq = seq // tq
    kvt = kv[:, :KV_LORA].reshape(batch * nq, tq, KV_LORA).transpose(0, 2, 1)
    return pl.pallas_call(
        functools.partial(_attn_prompt_body, tq=tq),
        grid=(batch, nq),
        in_specs=[
            pl.BlockSpec((heads, tq, MLA_ROW), lambda b, i: (0, b * nq + i, 0)),
            pl.BlockSpec((seq, MLA_ROW), lambda b, i: (b, 0)),
            pl.BlockSpec((nq, KV_LORA, tq), lambda b, i: (b, 0, 0)),
        ],
        out_specs=pl.BlockSpec((heads, tq, KV_LORA), lambda b, i: (0, b * nq + i, 0)),
        out_shape=jax.ShapeDtypeStruct((heads, m, KV_LORA), BF16),
        scratch_shapes=[
            pltpu.VMEM((1, heads * tq), F32),
            pltpu.VMEM((1, heads * tq), F32),
            pltpu.VMEM((KV_LORA, heads * tq), F32),
            pltpu.VMEM((tq, heads * tq), F32),
            pltpu.VMEM((tq, heads * tq), F32),
        ],
        compiler_params=_params("parallel", "parallel"),
        name="mla_attn_prompt",
    )(q, kv, kvt)


def _attn_decode_body(pt_ref, q_ref, kn_ref, cache_ref, o_ref, kbuf_ref, sem_ref,
                      m_ref, l_ref, acc_ref, *, pages, ts, layer_slot):
    b = pl.program_id(0)
    c = pl.program_id(1)
    n_chunks = pl.num_programs(1)
    step = b * n_chunks + c
    slot = step % 2

    def page_copy(page, p, sl):
        return pltpu.make_async_copy(cache_ref.at[layer_slot, page], kbuf_ref.at[sl, p],
                                     sem_ref.at[sl])

    def start_pages(bb, cc, sl):
        def body(p, carry):
            page_copy(pt_ref[bb, cc * pages + p], p, sl).start()
            return carry
        lax.fori_loop(0, pages, body, 0)

    @pl.when(step == 0)
    def _():
        start_pages(0, 0, 0)

    @pl.when(step + 1 < pl.num_programs(0) * n_chunks)
    def _():
        nxt = step + 1
        start_pages(nxt // n_chunks, nxt % n_chunks, 1 - slot)

    def wait_body(p, carry):
        page_copy(0, p, slot).wait()
        return carry

    lax.fori_loop(0, pages, wait_body, 0)

    @pl.when(c == 0)
    def _():
        _softmax_init(m_ref, l_ref, acc_ref)

    q = q_ref[0]
    group = DEC_PAGES_PER_GROUP
    kts, scores = [], []
    for g in range(pages // group):
        kt = jnp.concatenate([kbuf_ref[slot, p].astype(BF16)
                              for p in range(g * group, (g + 1) * group)],
                             axis=1)
        kts.append(kt)
        scores.append(_dot(q, kt))
    s = jnp.concatenate(scores, axis=1)
    m_prev = m_ref[...]
    m_new = jnp.maximum(m_prev, jnp.max(s, axis=1, keepdims=True))
    alpha = jnp.exp(m_prev - m_new)
    p = jnp.exp(s - m_new)
    l_ref[...] = alpha * l_ref[...] + jnp.sum(p, axis=1, keepdims=True)
    p16 = p.astype(BF16)
    width = group * PAGE_SIZE
    pv = sum(_dot_nt(p16[:, g * width:(g + 1) * width], kt[:KV_LORA, :])
             for g, kt in enumerate(kts))
    acc_ref[...] = alpha * acc_ref[...] + pv
    m_ref[...] = m_new

    @pl.when(c == pl.num_programs(1) - 1)
    def _():
        kn = kn_ref[0]
        s = _dot_nt(q, kn)
        tok = lax.broadcasted_iota(jnp.int32, s.shape, 0) % ts
        key = lax.broadcasted_iota(jnp.int32, s.shape, 1)
        s = jnp.where(key <= tok, s, -jnp.inf)
        _softmax_update(s, kn[:, :KV_LORA], m_ref, l_ref, acc_ref)
        o_ref[0] = (acc_ref[...] / l_ref[...]).astype(BF16)


def _attn_decode(q, k_new, cache_t, page_table, layer_slot, ts):
    bs, rows, _ = q.shape
    n_pages = page_table.shape[1]
    pages = DEC_PAGES_PER_STEP
    assert n_pages % pages == 0
    grid_spec = pltpu.PrefetchScalarGridSpec(
        num_scalar_prefetch=1,
        grid=(bs, n_pages // pages),
        in_specs=[
            pl.BlockSpec((1, rows, MLA_ROW), lambda b, c, pt: (b, 0, 0)),
            pl.BlockSpec((1, k_new.shape[1], MLA_ROW), lambda b, c, pt: (b, 0, 0)),
            pl.BlockSpec(memory_space=pl.ANY),
        ],
        out_specs=pl.BlockSpec((1, rows, KV_LORA), lambda b, c, pt: (b, 0, 0)),
        scratch_shapes=[
            pltpu.VMEM((2, pages, MLA_ROW, PAGE_SIZE), cache_t.dtype),
            pltpu.SemaphoreType.DMA((2,)),
            pltpu.VMEM((rows, 1), F32),
            pltpu.VMEM((rows, 1), F32),
            pltpu.VMEM((rows, KV_LORA), F32),
        ],
    )
    return pl.pallas_call(
        functools.partial(_attn_decode_body, pages=pages, ts=ts, layer_slot=layer_slot),
        grid_spec=grid_spec,
        out_shape=jax.ShapeDtypeStruct((bs, rows, KV_LORA), BF16),
        compiler_params=_params("arbitrary", "arbitrary"),
        name="mla_attn_decode",
    )(page_table, q, k_new, cache_t)


def _layer_tail_body(*refs, from_latent, d_ff, chunk):
    if from_latent:
        (a_ref, wuv_ref, wo_ref, h_ref, nw_mix_ref, nw_in_ref, win_ref, wout_ref, nw_out_ref,
         out_ref) = refs
    else:
        (a_ref, wo_ref, h_ref, nw_mix_ref, nw_in_ref, win_ref, wout_ref, nw_out_ref,
         out_ref) = refs
    tm = h_ref.shape[0]
    n_chains = TAIL_ROW_CHAINS if tm % (TAIL_ROW_CHAINS * 2 * SUBLANES) == 0 else 1
    rows_per = tm // n_chains
    chains = [dict(rows=slice(r * rows_per, (r + 1) * rows_per)) for r in range(n_chains)]
    for ch in chains:
        rows = ch["rows"]
        if from_latent:
            a16 = jnp.concatenate(
                [_dot(a_ref[h, rows, :], wuv_ref[h]).astype(BF16) for h in range(MLA_HEADS)],
                axis=1)
        else:
            a16 = a_ref[rows, :]
        ch["mix"] = _dot(a16, wo_ref[...])
    for ch in chains:
        x = h_ref[ch["rows"], :] + _rms(ch["mix"], nw_mix_ref[...])
        ch.update(x=x, u=_rms(x, nw_in_ref[...]).astype(BF16), acc=jnp.zeros(x.shape, F32))
    for c in range(d_ff // chunk):
        lo = c * chunk
        for ch in chains:
            gate = _dot(ch["u"], win_ref[0, :, lo:lo + chunk])
            up = _dot(ch["u"], win_ref[0, :, d_ff + lo:d_ff + lo + chunk])
            ch["act"] = (_silu(gate) * up).astype(BF16)
        for ch in chains:
            ch["acc"] = ch["acc"] + _dot(ch["act"], wout_ref[0, lo:lo + chunk, :])
    for ch in chains:
        out_ref[ch["rows"], :] = ch["x"] + _rms(ch["acc"], nw_out_ref[...])


def _layer_tail(a16, w_uv, w_o, h, nw_mix, ffn, tm):
    nw_in, w_in, w_out, layer, nw_out = ffn
    m = h.shape[0]
    d_ff = w_out.shape[1]
    assert d_ff % FFN_CHUNK == 0
    from_latent = w_uv is not None
    row = lambda i: (i, 0)
    if from_latent:
        in_specs = [pl.BlockSpec((MLA_HEADS, tm, KV_LORA), lambda i: (0, i, 0)),
                    _const_spec(w_uv.shape)]
        args = [a16, w_uv]
    else:
        in_specs = [pl.BlockSpec((tm, a16.shape[1]), row)]
        args = [a16]
    in_specs += [
        _const_spec(w_o.shape),
        pl.BlockSpec((tm, D_MODEL), row),
        _const_spec((1, D_MODEL)),
        _const_spec((1, D_MODEL)),
        _fixed_spec((1,) + w_in.shape[1:], (layer, 0, 0)),
        _fixed_spec((1,) + w_out.shape[1:], (layer, 0, 0)),
        _const_spec((1, D_MODEL)),
    ]
    args += [w_o, h, nw_mix, nw_in, w_in, w_out, nw_out]
    return pl.pallas_call(
        functools.partial(_layer_tail_body, from_latent=from_latent, d_ff=d_ff, chunk=FFN_CHUNK),
        grid=(m // tm,),
        in_specs=in_specs,
        out_specs=pl.BlockSpec((tm, D_MODEL), row),
        out_shape=jax.ShapeDtypeStruct((m, D_MODEL), F32),
        compiler_params=_params("parallel"),
        name="layer_tail",
    )(*args)


def _dn_proj_body(*refs, tm, seq_len, has_hist, tail_rows):
    if has_hist:
        (x_ref, nw_ref, wqkv_ref, wz_ref, wba_ref, cw_ref, alog_ref, dtb_ref, hist_ref,
         qkv_ref, z_ref, bg_ref, tail_ref, ext_ref) = refs
    else:
        (x_ref, nw_ref, wqkv_ref, wz_ref, wba_ref, cw_ref, alog_ref, dtb_ref,
         qkv_ref, z_ref, bg_ref, tail_ref, ext_ref) = refs
        hist_ref = None
    i = pl.program_id(0)
    carry = CONV_CARRY_ROWS

    @pl.when((i * tm) % seq_len == 0)
    def _():
        ext_ref[0:carry, :] = jnp.zeros((carry, DN_QKV), F32)

    u = _rms(x_ref[...], nw_ref[...]).astype(BF16)
    assert carry == SUBLANES and tm % (DN_PROJ_ROW_CHAINS * SUBLANES) == 0
    rows_per = tm // DN_PROJ_ROW_CHAINS
    assert rows_per % seq_len == 0 or not has_hist
    raws = []
    for r in range(DN_PROJ_ROW_CHAINS):
        raws.append(_dot(u[r * rows_per:(r + 1) * rows_per], wqkv_ref[...]))
        if r == DN_PROJ_GATE_SLOT:
            z_ref[...] = _dot(u, wz_ref[...])
            ba = _dot(u, wba_ref[...])
    sub = lax.broadcasted_iota(jnp.int32, (1, SUBLANES, 1), 1)
    cw = cw_ref[...]
    before = ext_ref[...]
    for r, raw in enumerate(raws):
        rows = slice(r * rows_per, (r + 1) * rows_per)
        cur = raw.reshape(rows_per // SUBLANES, SUBLANES, DN_QKV)
        prev = jnp.concatenate([before.reshape(1, SUBLANES, DN_QKV), cur[:-1]], axis=0)
        acc = raw * cw[CONV_W - 1:CONV_W, :]
        for k in range(1, CONV_W):
            mixed = jnp.where(sub >= SUBLANES - k, prev, cur)
            shifted = pltpu.roll(mixed, k, axis=1).reshape(rows_per, DN_QKV)
            if has_hist:
                tpos = lax.broadcasted_iota(jnp.int32, (rows_per, 1), 0) % seq_len
                shifted = jnp.where(tpos >= k, shifted, hist_ref[k - 1, rows, :])
            acc = acc + shifted * cw[CONV_W - 1 - k:CONV_W - k, :]
        before = raw[rows_per - carry:, :]
        act = _silu(acc)
        for h in range(DN_HEADS):
            qh = act[:, h * DN_DK:(h + 1) * DN_DK]
            kh = act[:, DN_HK + h * DN_DK:DN_HK + (h + 1) * DN_DK]
            qn = qh * lax.rsqrt(jnp.sum(qh * qh, axis=-1, keepdims=True) + L2_EPS) * (DN_DK ** -0.5)
            kn = kh * lax.rsqrt(jnp.sum(kh * kh, axis=-1, keepdims=True) + L2_EPS)
            qkv_ref[rows, h * DN_DK:(h + 1) * DN_DK] = qn
            qkv_ref[rows, DN_HK + h * DN_DK:DN_HK + (h + 1) * DN_DK] = kn
        qkv_ref[rows, 2 * DN_HK:] = act[:, 2 * DN_HK:]
    ext_ref[...] = before
    tail_ref[0] = jnp.concatenate(raws, axis=0)[tm - tail_rows:, :]
    beta = _sigmoid(ba[:, :DN_HEADS])
    g = -jnp.exp(alog_ref[...]) * _softplus(ba[:, DN_HEADS:2 * DN_HEADS] + dtb_ref[...])
    bg_ref[:, :DN_HEADS] = beta
    bg_ref[:, DN_HEADS:] = g


def _dn_project(x, nw, w_all, w_ba, conv_w, a_log, dt_bias, hist, tm, seq_len, tail_rows):
    assert DN_QKV % DN_Z == 0
    m = x.shape[0]
    n_tiles = m // tm
    has_hist = hist is not None
    assert (tm % seq_len == 0 and n_tiles == 1) if has_hist else seq_len % tm == 0
    tiles_per_tail = max(seq_len // tm, 1)
    row = lambda i: (i, 0)
    in_specs = [
        pl.BlockSpec((tm, D_MODEL), row),
        _const_spec((1, D_MODEL)),
        _fixed_spec((D_MODEL, DN_QKV), (0, 0)),
        _fixed_spec((D_MODEL, DN_Z), (0, DN_QKV // DN_Z)),
        _const_spec(w_ba.shape),
        _const_spec(conv_w.shape),
        _const_spec((1, DN_HEADS)),
        _const_spec((1, DN_HEADS)),
    ]
    args = [x, nw, w_all, w_all, w_ba, conv_w, a_log, dt_bias]
    if has_hist:
        in_specs.append(pl.BlockSpec((CONV_W - 1, tm, DN_QKV), lambda i: (0, i, 0)))
        args.append(hist)
    return pl.pallas_call(
        functools.partial(_dn_proj_body, tm=tm, seq_len=seq_len, has_hist=has_hist,
                          tail_rows=tail_rows),
        grid=(n_tiles,),
        in_specs=in_specs,
        out_specs=[
            pl.BlockSpec((tm, DN_QKV), row),
            pl.BlockSpec((tm, DN_Z), row),
            pl.BlockSpec((tm, 2 * DN_HEADS), row),
            pl.BlockSpec((1, tail_rows, DN_QKV), lambda i: (i // tiles_per_tail, 0, 0)),
        ],
        out_shape=[
            jax.ShapeDtypeStruct((m, DN_QKV), F32),
            jax.ShapeDtypeStruct((m, DN_Z), F32),
            jax.ShapeDtypeStruct((m, 2 * DN_HEADS), F32),
            jax.ShapeDtypeStruct((n_tiles // tiles_per_tail, tail_rows, DN_QKV), F32),
        ],
        scratch_shapes=[pltpu.VMEM((CONV_CARRY_ROWS, DN_QKV), F32)],
        compiler_params=_params("arbitrary"),
        name="dn_project",
    )(*args)


def _stack_heads(load, g):
    return jnp.concatenate(
        [load(slice((g * GDN_GROUP + hh) * DN_DK, (g * GDN_GROUP + hh + 1) * DN_DK))
         for hh in range(GDN_GROUP)], axis=0)


def _gdn_local_body(q_ref, k_ref, v_ref, bg_ref, bgt_ref,
                    u_ref, w_ref, qd_ref, kdt_ref, qk_ref, egl_ref, *, chunk, chunks_per_step):
    n_heads = DN_HEADS
    stack = GDN_GROUP * chunk
    ri = lax.broadcasted_iota(jnp.int32, (stack, stack), 0)
    ci = lax.broadcasted_iota(jnp.int32, (stack, stack), 1)
    same_head = (ri // chunk) == (ci // chunk)
    causal = same_head & (ri >= ci)
    strict = same_head & (ri > ci)
    r1 = lax.broadcasted_iota(jnp.int32, (chunk, chunk), 0)
    c1 = lax.broadcasted_iota(jnp.int32, (chunk, chunk), 1)
    tril = jnp.where(r1 >= c1, 1.0, 0.0).astype(BF16)
    triu = jnp.where(r1 <= c1, 1.0, 0.0).astype(BF16)
    n_double = int(math.log2(chunk)) - 2

    def store_heads(ref, rows, g, stacked):
        for hh in range(GDN_GROUP):
            h = g * GDN_GROUP + hh
            ref[rows, h * DN_DK:(h + 1) * DN_DK] = stacked[hh * chunk:(hh + 1) * chunk, :]

    chains = []
    for cp in range(chunks_per_step):
        rows = slice(cp * chunk, (cp + 1) * chunk)
        bg = bg_ref[rows, :]
        bgt = bgt_ref[cp]
        gc_col = sum(_dot(tril, part) for part in _split3(bg))
        gc_row = sum(_dot(part, triu) for part in _split3(bgt))
        egl_ref[cp] = jnp.broadcast_to(jnp.exp(gc_row[n_heads:, chunk - 1:chunk]),
                                       (n_heads, LANES))
        for g in range(n_heads // GDN_GROUP):
            heads = [g * GDN_GROUP + hh for hh in range(GDN_GROUP)]
            q = _stack_heads(lambda sl: q_ref[rows, sl], g)
            k = _stack_heads(lambda sl: k_ref[rows, sl], g)
            v = _stack_heads(lambda sl: v_ref[rows, sl], g)
            beta = jnp.concatenate([bg[:, h:h + 1] for h in heads], axis=0)
            gcc = jnp.concatenate([gc_col[:, n_heads + h:n_heads + h + 1] for h in heads], axis=0)
            gcr = jnp.concatenate([gc_row[n_heads + h:n_heads + h + 1, :] for h in heads], axis=1)
            g_last = jnp.concatenate(
                [jnp.broadcast_to(gc_col[chunk - 1:chunk, n_heads + h:n_heads + h + 1], (chunk, 1))
                 for h in heads], axis=0)
            eg = jnp.exp(gcc)
            kb = k * beta
            store_heads(qd_ref, rows, g, (q * eg).astype(BF16))
            kdt_ref[cp, :, g * stack:(g + 1) * stack] = (k * jnp.exp(g_last - gcc)).T.astype(BF16)
            chains.append(dict(
                rows=rows, g=g, q16=q.astype(BF16), k16=k.astype(BF16), kb16=kb.astype(BF16),
                decay=jnp.where(causal, jnp.exp(gcc - gcr), 0.0),
                rhs=jnp.concatenate([v * beta, kb * eg], axis=1)))

    for ch in chains:
        lower = jnp.where(strict, _dot_nt(ch["kb16"], ch["k16"]) * ch["decay"], 0.0)
        ch.update(lower=lower, power=lower, off=-lower)
    for ch in chains:
        qk = jnp.where(causal, _dot_nt(ch["q16"], ch["k16"]) * ch["decay"], 0.0)
        qk_packed = sum(qk[hh * chunk:(hh + 1) * chunk, :] for hh in range(GDN_GROUP))
        qk_ref[ch["rows"], ch["g"] * stack:(ch["g"] + 1) * stack] = qk_packed.astype(BF16)
    for _ in range(n_double):
        for ch in chains:
            p16 = ch["power"].astype(BF16)
            ch["power"] = _dot(p16, p16)
        for ch in chains:
            ch["off"] = ch["off"] + ch["power"] + _dot(ch["off"].astype(BF16),
                                                       ch["power"].astype(BF16))
    for ch in chains:
        ch["resid"] = (ch["lower"] + ch["off"]) + _dot_hl(ch["lower"], ch["off"])
    for ch in chains:
        ch["off"] = ch["off"] - ch["resid"] - _dot(ch["off"].astype(BF16),
                                                   ch["resid"].astype(BF16))
    for ch in chains:
        sol = ch["rhs"] + _dot(ch["off"].astype(BF16), ch["rhs"].astype(BF16))
        store_heads(u_ref, ch["rows"], ch["g"], sol[:, :DN_DV])
        store_heads(w_ref, ch["rows"], ch["g"], sol[:, DN_DV:].astype(BF16))


def _gdn_scan_body(u_ref, w_ref, qd_ref, kdt_ref, qk_ref, egl_ref, z_ref, gout_ref, s0_ref,
                   y_ref, s_ref, *, chunk, seqs):
    c = pl.program_id(1)

    @pl.when(c == 0)
    def _():
        s_ref[...] = s0_ref[...]

    stack = GDN_GROUP * chunk
    pair = 2 * DN_DK
    lhs_mask = ((lax.broadcasted_iota(jnp.int32, (4 * chunk, pair), 0) // chunk) % 2
                == lax.broadcasted_iota(jnp.int32, (4 * chunk, pair), 1) // DN_DK)
    qk_mask = (lax.broadcasted_iota(jnp.int32, (stack, stack), 0) // chunk
               == lax.broadcasted_iota(jnp.int32, (stack, stack), 1) // chunk)
    kd_mask = (lax.broadcasted_iota(jnp.int32, (GDN_GROUP * DN_DK, stack), 0) // DN_DK
               == lax.broadcasted_iota(jnp.int32, (GDN_GROUP * DN_DK, stack), 1) // chunk)
    zero16 = jnp.zeros((), BF16)
    chains = [dict(sq=sq, g=g) for sq in range(seqs) for g in range(DN_HEADS // GDN_GROUP)]
    for ch in chains:
        sq, g = ch["sq"], ch["g"]
        s = s_ref[sq, g * GDN_GROUP:(g + 1) * GDN_GROUP].reshape(GDN_GROUP * DN_DK, DN_DV)
        s16 = s.astype(BF16)
        ws, qs = [], []
        for p in range(GDN_GROUP // 2):
            cols = [slice((g * GDN_GROUP + 2 * p + e) * DN_DK, (g * GDN_GROUP + 2 * p + e + 1) * DN_DK)
                    for e in range(2)]
            lhs = jnp.concatenate([w_ref[sq, :, cols[0]], w_ref[sq, :, cols[1]],
                                   qd_ref[sq, :, cols[0]], qd_ref[sq, :, cols[1]]], axis=0)
            lhs_bd = jnp.where(lhs_mask, jnp.concatenate([lhs, lhs], axis=1), zero16)
            res = _dot(lhs_bd, s16[p * pair:(p + 1) * pair, :])
            ws.append(res[:2 * chunk])
            qs.append(res[2 * chunk:])
        ch.update(s=s, ws=jnp.concatenate(ws, axis=0), qs=jnp.concatenate(qs, axis=0))
    for ch in chains:
        sq, g = ch["sq"], ch["g"]
        v_new = _stack_heads(lambda sl: u_ref[sq, :, sl], g) - ch["ws"]
        ch["v16"] = v_new.astype(BF16)
    gout = gout_ref[...]
    for ch in chains:
        sq, g = ch["sq"], ch["g"]
        qk = qk_ref[sq, :, g * stack:(g + 1) * stack]
        qk_bd = jnp.where(qk_mask, jnp.concatenate([qk] * GDN_GROUP, axis=0), zero16)
        o = ch["qs"] + _dot(qk_bd, ch["v16"])
        y = _rms(o, gout) * _silu(_stack_heads(lambda sl: z_ref[sq, :, sl], g))
        y16 = y.astype(BF16)
        for hh in range(GDN_GROUP):
            h = g * GDN_GROUP + hh
            y_ref[sq, :, h * DN_DV:(h + 1) * DN_DV] = y16[hh * chunk:(hh + 1) * chunk, :]
    for ch in chains:
        sq, g = ch["sq"], ch["g"]
        kdt = kdt_ref[sq, 0, :, g * stack:(g + 1) * stack]
        kdt_bd = jnp.where(kd_mask, jnp.concatenate([kdt] * GDN_GROUP, axis=0), zero16)
        gate = jnp.concatenate(
            [jnp.broadcast_to(egl_ref[sq, 0, g * GDN_GROUP + hh:g * GDN_GROUP + hh + 1, :],
                              (DN_DK, DN_DV)) for hh in range(GDN_GROUP)], axis=0)
        s_new = ch["s"] * gate + _dot(kdt_bd, ch["v16"])
        s_ref[sq, g * GDN_GROUP:(g + 1) * GDN_GROUP] = s_new.reshape(GDN_GROUP, DN_DK, DN_DV)


def _gdn(qkv, bg, z, g_out, s0, n_seq, seq_len, chunk):
    m = qkv.shape[0]
    n_chunks = m // chunk
    nc = seq_len // chunk
    cps = GDN_CHUNKS_PER_STEP
    sps = GDN_SEQS_PER_STEP
    assert n_chunks % cps == 0 and n_seq % sps == 0
    bgt = bg.reshape(n_chunks, chunk, 2 * DN_HEADS).transpose(0, 2, 1)
    half = DN_HEADS * chunk
    row = lambda i: (i, 0)
    blk = lambda col: pl.BlockSpec((cps * chunk, DN_HK), lambda i, col=col: (i, col))
    u, w16, qd16, kdt16, qk16, egl = pl.pallas_call(
        functools.partial(_gdn_local_body, chunk=chunk, chunks_per_step=cps),
        grid=(n_chunks // cps,),
        in_specs=[
            blk(0), blk(1), blk(2),
            pl.BlockSpec((cps * chunk, 2 * DN_HEADS), row),
            pl.BlockSpec((cps, 2 * DN_HEADS, chunk), lambda i: (i, 0, 0)),
        ],
        out_specs=[
            pl.BlockSpec((cps * chunk, DN_Z), row),
            pl.BlockSpec((cps * chunk, DN_HK), row),
            pl.BlockSpec((cps * chunk, DN_HK), row),
            pl.BlockSpec((cps, DN_DK, half), lambda i: (i, 0, 0)),
            pl.BlockSpec((cps * chunk, half), row),
            pl.BlockSpec((cps, DN_HEADS, LANES), lambda i: (i, 0, 0)),
        ],
        out_shape=[
            jax.ShapeDtypeStruct((m, DN_Z), F32),
            jax.ShapeDtypeStruct((m, DN_HK), BF16),
            jax.ShapeDtypeStruct((m, DN_HK), BF16),
            jax.ShapeDtypeStruct((n_chunks, DN_DK, half), BF16),
            jax.ShapeDtypeStruct((m, half), BF16),
            jax.ShapeDtypeStruct((n_chunks, DN_HEADS, LANES), F32),
        ],
        compiler_params=_params("parallel"),
        name="gdn_local",
    )(qkv, qkv, qkv, bg, bgt)

    per_seq = lambda a: a.reshape((n_seq, nc if a.ndim == 3 else seq_len) + a.shape[1:])
    tok_spec = lambda width: pl.BlockSpec((sps, chunk, width), lambda b, c: (b, c, 0))
    lead_spec = lambda d1, d2: pl.BlockSpec((sps, 1, d1, d2), lambda b, c: (b, c, 0, 0))
    state_spec = pl.BlockSpec((sps, DN_HEADS, DN_DK, DN_DV), lambda b, c: (b, 0, 0, 0))
    y16, s = pl.pallas_call(
        functools.partial(_gdn_scan_body, chunk=chunk, seqs=sps),
        grid=(n_seq // sps, nc),
        in_specs=[
            tok_spec(DN_Z),
            tok_spec(DN_HK),
            tok_spec(DN_HK),
            lead_spec(DN_DK, half),
            tok_spec(half),
            lead_spec(DN_HEADS, LANES),
            tok_spec(DN_Z),
            pl.BlockSpec((1, DN_DV), lambda b, c: (0, 0)),
            state_spec,
        ],
        out_specs=[tok_spec(DN_Z), state_spec],
        out_shape=[
            jax.ShapeDtypeStruct((n_seq, seq_len, DN_Z), BF16),
            jax.ShapeDtypeStruct(s0.shape, F32),
        ],
        compiler_params=_params("parallel", "arbitrary"),
        name="gdn_scan",
    )(per_seq(u), per_seq(w16), per_seq(qd16), per_seq(kdt16), per_seq(qk16), per_seq(egl),
      per_seq(z), g_out, s0)
    return y16.reshape(m, DN_Z), s


def _mla_layer(hp, hs, cache_mla, page_table, slot, nw, w_in, g_q, g_kv, w_uq, w_uk, w_uv, w_o,
               ffn, bp, tp, bs, ts):
    past = page_table.shape[1] * PAGE_SIZE
    w_in16 = w_in.astype(BF16)
    w_uq_heads = w_uq.reshape(Q_LORA, MLA_HEADS, QK_NOPE + QK_ROPE)
    w_uq16 = jnp.concatenate(
        [w_uq_heads[:, :, :QK_NOPE].reshape(Q_LORA, MLA_HEADS * QK_NOPE),
         w_uq_heads[:, :, QK_NOPE:].reshape(Q_LORA, MLA_HEADS * QK_ROPE)], axis=1).astype(BF16)
    w_ukt16 = jnp.swapaxes(w_uk, 1, 2).astype(BF16)
    w_uv16 = w_uv.astype(BF16)
    w_o16 = w_o.astype(BF16)
    g_q = g_q.reshape(1, Q_LORA)
    g_kv = g_kv.reshape(1, KV_LORA)
    nw0 = nw[0].reshape(1, D_MODEL)
    nw1 = nw[1].reshape(1, D_MODEL)

    cs_p, sn_p = _rope_tables(jnp.arange(tp))
    cs_s, sn_s = _rope_tables(past + jnp.tile(jnp.arange(ts), bs))
    ms = bs * ts
    rows_p, kv_p, q_p = _mla_project(hp, nw0, w_in16, g_q, g_kv, w_uq16, w_ukt16, cs_p, sn_p,
                                     TM_TOKENS)
    rows_s, kv_s, q_s = _mla_project(hs, nw0, w_in16, g_q, g_kv, w_uq16, w_ukt16, cs_s, sn_s, ms)

    o_p = _attn_prompt(q_p, kv_p, bp, tp, TQ_ATTN)

    q_sb = q_s.reshape(MLA_HEADS, bs, ts, MLA_ROW).transpose(1, 0, 2, 3).reshape(
        bs, MLA_HEADS * ts, MLA_ROW)
    new_rows = 2 * SUBLANES
    k_new = jnp.pad(kv_s.reshape(bs, ts, MLA_ROW), ((0, 0), (0, new_rows - ts), (0, 0)))
    o_sb = _attn_decode(q_sb, k_new, jnp.swapaxes(cache_mla, 2, 3), page_table, slot, ts)
    o_s = o_sb.reshape(bs, MLA_HEADS, ts, KV_LORA).transpose(1, 0, 2, 3).reshape(
        MLA_HEADS, ms, KV_LORA)

    hp = _layer_tail(o_p, w_uv16, w_o16, hp, nw1, ffn, TM_TOKENS)
    hs = _layer_tail(o_s, w_uv16, w_o16, hs, nw1, ffn, ms)
    return hp, hs, rows_p.reshape(bp, tp, MLA_ROW), rows_s.reshape(bs, ts, MLA_ROW)


def _dn_layer(hp, hs, s0_s, conv0_s, nw, w_in, conv_w, a_log, dt_bias, g_out, w_o, ffn,
              bp, tp, bs, ts):
    w_in16 = w_in.astype(BF16)
    w_ba16 = jnp.pad(w_in[:, DN_QKV + DN_Z:], ((0, 0), (0, LANES - 2 * DN_HEADS))).astype(BF16)
    w_o16 = w_o.astype(BF16)
    a_log = a_log.reshape(1, DN_HEADS).astype(F32)
    dt_bias = dt_bias.reshape(1, DN_HEADS).astype(F32)
    g_out = g_out.reshape(1, DN_DV)
    nw0 = nw[0].reshape(1, D_MODEL)
    nw1 = nw[1].reshape(1, D_MODEL)
    hist_rows = CONV_W - 1
    ms = bs * ts

    qkv_p, z_p, bg_p, tail_p = _dn_project(hp, nw0, w_in16, w_ba16, conv_w, a_log, dt_bias,
                                           None, TM_DN_PROJ, tp, SUBLANES)
    conv_p = tail_p[:, SUBLANES - hist_rows:, :]
    s0_p = jnp.zeros((bp, DN_HEADS, DN_DK, DN_DV), F32)
    y_p, s_p = _gdn(qkv_p, bg_p, z_p, g_out, s0_p, bp, tp, DN_CHUNK)
    hp = _layer_tail(y_p, None, w_o16, hp, nw1, ffn, TM_TOKENS)

    tok = jnp.arange(ts)
    hist = jnp.stack([
        conv0_s[:, jnp.clip(hist_rows - k + tok, 0, hist_rows - 1), :].reshape(ms, DN_QKV)
        for k in range(1, CONV_W)
    ]).astype(F32)
    qkv_s, z_s, bg_s, tail_s = _dn_project(hs, nw0, w_in16, w_ba16, conv_w, a_log, dt_bias,
                                           hist, ms, ts, ms)
    raw_s = tail_s.reshape(bs, ts, DN_QKV)
    conv_s = jnp.concatenate([conv0_s.astype(F32), raw_s], axis=1)[:, ts:, :]
    cs = DN_CHUNK_SAMPLE
    assert ts <= cs
    pad = lambda a: jnp.pad(a.reshape(bs, ts, -1), ((0, 0), (0, cs - ts), (0, 0))).reshape(
        bs * cs, -1)
    y_s_pad, s_s = _gdn(pad(qkv_s), pad(bg_s), pad(z_s), g_out, s0_s.astype(F32), bs, cs, cs)
    y_s = y_s_pad.reshape(bs, cs, DN_Z)[:, :ts, :].reshape(ms, DN_Z)
    hs = _layer_tail(y_s, None, w_o16, hs, nw1, ffn, ms)
    return hp, hs, s_p, s_s, conv_p, conv_s


def kernel(x_prompt, x_sample, cache_mla, state_dn, state_dn_conv, page_table, norm_w, mla_w_in,
           mla_g_q, mla_g_kv, mla_w_uq, mla_w_uk, mla_w_uv, mla_w_o, dn_w_in, dn_conv_w, dn_a_log,
           dn_dt_bias, dn_g_out, dn_w_o, ffn_w_in, ffn_w_out):
    bp, tp, _ = x_prompt.shape
    bs, ts, _ = x_sample.shape
    depth = norm_w.shape[0]
    hp = x_prompt.reshape(bp * tp, D_MODEL)
    hs = x_sample.reshape(bs * ts, D_MODEL)
    rows_p_l, rows_s_l, sp_l, ss_l, cp_l, cs_l = [], [], [], [], [], []
    ffn_w_in16 = ffn_w_in.astype(BF16)
    ffn_w_out16 = ffn_w_out.astype(BF16)
    for layer in range(depth):
        j = layer // N_MIXERS
        nw = norm_w[layer]
        ffn = (nw[2].reshape(1, D_MODEL), ffn_w_in16, ffn_w_out16, layer, nw[3].reshape(1, D_MODEL))
        if layer % N_MIXERS == 0:
            hp, hs, rows_p, rows_s = _mla_layer(
                hp, hs, cache_mla, page_table, j, nw, mla_w_in[j], mla_g_q[j], mla_g_kv[j],
                mla_w_uq[j], mla_w_uk[j], mla_w_uv[j], mla_w_o[j], ffn, bp, tp, bs, ts)
            rows_p_l.append(rows_p)
            rows_s_l.append(rows_s)
        else:
            hp, hs, s_p, s_s, c_p, c_s = _dn_layer(
                hp, hs, state_dn[j], state_dn_conv[j], nw, dn_w_in[j], dn_conv_w[j], dn_a_log[j],
                dn_dt_bias[j], dn_g_out[j], dn_w_o[j], ffn, bp, tp, bs, ts)
            sp_l.append(s_p.astype(state_dn.dtype))
            ss_l.append(s_s.astype(state_dn.dtype))
            cp_l.append(c_p.astype(state_dn_conv.dtype))
            cs_l.append(c_s.astype(state_dn_conv.dtype))
    return (hp.reshape(bp, tp, D_MODEL), hs.reshape(bs, ts, D_MODEL),
            jnp.stack(rows_p_l), jnp.stack(rows_s_l), jnp.stack(sp_l), jnp.stack(ss_l),
            jnp.stack(cp_l), jnp.stack(cs_l))
```

```python
import functools
import math

import jax
import jax.numpy as jnp
from jax import lax
from jax.experimental import pallas as pl
from jax.experimental.pallas import tpu as pltpu

F32 = jnp.float32
BF16 = jnp.bfloat16

D_MODEL = 1024
PAGE_SIZE = 128
N_MIXERS = 2

MLA_HEADS = 8
QK_NOPE = 128
QK_ROPE = 64
V_HEAD = 128
KV_LORA = 256
Q_LORA = 384
MLA_ROW = KV_LORA + QK_ROPE
MLA_SCALE = (QK_NOPE + QK_ROPE) ** -0.5
ROPE_THETA = 10000.0

DN_HEADS = 8
DN_DK = 128
DN_DV = 128
CONV_W = 4
DN_HK = DN_HEADS * DN_DK
DN_QKV = DN_HEADS * (2 * DN_DK + DN_DV)
DN_Z = DN_HEADS * DN_DV

RMS_EPS = 1e-6
L2_EPS = 1e-6

LANES = 128
SUBLANES = 8
VMEM_LIMIT_BYTES = 56 * 1024 * 1024

TM_TOKENS = 512
MLA_PROJ_ROW_CHAINS = 4
TAIL_ROW_CHAINS = 4
DN_PROJ_ROW_CHAINS = 4
DN_PROJ_GATE_SLOT = 0
TM_DN_PROJ = 512
TQ_ATTN = 256
DN_CHUNK = 64
DN_CHUNK_SAMPLE = 16
GDN_GROUP = 4
GDN_CHUNKS_PER_STEP = 4
GDN_SEQS_PER_STEP = 4
DEC_PAGES_PER_STEP = 64
DEC_PAGES_PER_GROUP = 8
FFN_CHUNK = 256
CONV_CARRY_ROWS = SUBLANES


def _params(*sem):
    return pltpu.CompilerParams(dimension_semantics=sem, vmem_limit_bytes=VMEM_LIMIT_BYTES)


def _fixed_spec(block_shape, block_index):
    return pl.BlockSpec(block_shape, lambda *_: block_index, pipeline_mode=pl.Buffered(1))


def _const_spec(shape):
    return _fixed_spec(shape, (0,) * len(shape))


def _dot(a, b):
    return jnp.dot(a, b, preferred_element_type=F32)


def _dot_nt(a, b):
    return lax.dot_general(a, b, (((1,), (1,)), ((), ())), preferred_element_type=F32)


def _dot_tn(a, b):
    return lax.dot_general(a, b, (((0,), (0,)), ((), ())), preferred_element_type=F32)


def _split2(x):
    hi = x.astype(BF16)
    lo = (x - hi.astype(F32)).astype(BF16)
    return hi, lo


def _split3(x):
    hi = x.astype(BF16)
    r = x - hi.astype(F32)
    mid = r.astype(BF16)
    lo = (r - mid.astype(F32)).astype(BF16)
    return hi, mid, lo


def _dot_hl(a, b):
    ah, al = _split2(a)
    bh, bl = _split2(b)
    return _dot(ah, bh) + (_dot(ah, bl) + _dot(al, bh))


def _rms(x, w):
    return x * lax.rsqrt(jnp.mean(x * x, axis=-1, keepdims=True) + RMS_EPS) * w


def _sigmoid(x):
    return 1.0 / (1.0 + jnp.exp(-x))


def _silu(x):
    return x * _sigmoid(x)


def _softplus(x):
    return jnp.maximum(x, 0.0) + jnp.log1p(jnp.exp(-jnp.abs(x)))


def _rope(x, cs, sn):
    half = x.shape[-1] // 2
    swapped = jnp.concatenate([x[:, half:], x[:, :half]], axis=1)
    return x * cs + swapped * sn


def _rope_tables(pos):
    half = QK_ROPE // 2
    freq = ROPE_THETA ** (-jnp.arange(half, dtype=F32) / half)
    ang = pos.astype(F32)[:, None] * freq[None, :]
    cos, sin = jnp.cos(ang), jnp.sin(ang)
    return jnp.concatenate([cos, cos], axis=1), jnp.concatenate([-sin, sin], axis=1)


def _mla_proj_body(x_ref, nw_ref, win_ref, gq_ref, gkv_ref, wuq_ref, wukt_ref, cs_ref, sn_ref,
                   rows_ref, kv_ref, q_ref):
    tm = x_ref.shape[0]
    n_chains = MLA_PROJ_ROW_CHAINS if tm % (MLA_PROJ_ROW_CHAINS * 2 * SUBLANES) == 0 else 1
    rows_per = tm // n_chains
    chains = [dict(rows=slice(r * rows_per, (r + 1) * rows_per)) for r in range(n_chains)]
    rope_base = MLA_HEADS * QK_NOPE
    for ch in chains:
        u = _rms(x_ref[ch["rows"], :], nw_ref[...]).astype(BF16)
        ch["a"] = _dot(u, win_ref[...])
    for ch in chains:
        rows, a = ch["rows"], ch["a"]
        ch["cs"] = cs_ref[rows, :]
        ch["sn"] = sn_ref[rows, :]
        c_q = _rms(a[:, :Q_LORA], gq_ref[...]).astype(BF16)
        ch["q"] = _dot(c_q, wuq_ref[...])
    for ch in chains:
        rows, a = ch["rows"], ch["a"]
        c_kv = _rms(a[:, Q_LORA:Q_LORA + KV_LORA], gkv_ref[...])
        k_r = _rope(a[:, Q_LORA + KV_LORA:], ch["cs"], ch["sn"])
        rows_ref[rows, :KV_LORA] = c_kv
        rows_ref[rows, KV_LORA:] = k_r
        kv_ref[rows, :KV_LORA] = c_kv.astype(BF16)
        kv_ref[rows, KV_LORA:] = k_r.astype(BF16)
    for h in range(MLA_HEADS):
        for ch in chains:
            rows, q = ch["rows"], ch["q"]
            q_lat = _dot(q[:, h * QK_NOPE:(h + 1) * QK_NOPE].astype(BF16), wukt_ref[h])
            q_rope = _rope(q[:, rope_base + h * QK_ROPE:rope_base + (h + 1) * QK_ROPE],
                           ch["cs"], ch["sn"])
            q_ref[h, rows, :KV_LORA] = (q_lat * MLA_SCALE).astype(BF16)
            q_ref[h, rows, KV_LORA:] = (q_rope * MLA_SCALE).astype(BF16)


def _mla_project(x, nw, w_in, g_q, g_kv, w_uq, w_ukt, cs, sn, tm):
    m = x.shape[0]
    period_tiles = cs.shape[0] // tm
    assert cs.shape[0] % tm == 0 and m % cs.shape[0] == 0
    row = lambda i: (i, 0)
    pos_row = lambda i: (i % period_tiles, 0)
    return pl.pallas_call(
        _mla_proj_body,
        grid=(m // tm,),
        in_specs=[
            pl.BlockSpec((tm, D_MODEL), row),
            _const_spec((1, D_MODEL)),
            _const_spec(w_in.shape),
            _const_spec((1, Q_LORA)),
            _const_spec((1, KV_LORA)),
            _const_spec(w_uq.shape),
            _const_spec(w_ukt.shape),
            pl.BlockSpec((tm, QK_ROPE), pos_row),
            pl.BlockSpec((tm, QK_ROPE), pos_row),
        ],
        out_specs=[
            pl.BlockSpec((tm, MLA_ROW), row),
            pl.BlockSpec((tm, MLA_ROW), row),
            pl.BlockSpec((MLA_HEADS, tm, MLA_ROW), lambda i: (0, i, 0)),
        ],
        out_shape=[
            jax.ShapeDtypeStruct((m, MLA_ROW), F32),
            jax.ShapeDtypeStruct((m, MLA_ROW), BF16),
            jax.ShapeDtypeStruct((MLA_HEADS, m, MLA_ROW), BF16),
        ],
        compiler_params=_params("parallel"),
        name="mla_project",
    )(x, nw, w_in, g_q, g_kv, w_uq, w_ukt, cs, sn)


def _softmax_update(s, v, m_ref, l_ref, acc_ref, v_transposed=False):
    m_prev = m_ref[...]
    m_new = jnp.maximum(m_prev, jnp.max(s, axis=1, keepdims=True))
    alpha = jnp.exp(m_prev - m_new)
    p = jnp.exp(s - m_new)
    p16 = p.astype(BF16)
    pv = _dot_nt(p16, v) if v_transposed else _dot(p16, v)
    l_ref[...] = alpha * l_ref[...] + jnp.sum(p, axis=1, keepdims=True)
    acc_ref[...] = alpha * acc_ref[...] + pv
    m_ref[...] = m_new


def _softmax_init(m_ref, l_ref, acc_ref):
    m_ref[...] = jnp.full(m_ref.shape, -jnp.inf, F32)
    l_ref[...] = jnp.zeros(l_ref.shape, F32)
    acc_ref[...] = jnp.zeros(acc_ref.shape, F32)


def _attn_prompt_body(q_ref, kv_ref, kvt_ref, o_ref, m_ref, l_ref, acc_ref, sa_ref, sb_ref, *, tq):
    i = pl.program_id(1)
    heads = q_ref.shape[0]
    cols = heads * tq
    q = q_ref[...].reshape(cols, MLA_ROW)

    def scores(j, buf):
        k = kv_ref[pl.ds(pl.multiple_of(j * tq, tq), tq), :]
        buf[...] = _dot_nt(k, q)

    def absorb(j, buf, masked):
        st = buf[...]
        if masked:
            key = lax.broadcasted_iota(jnp.int32, (tq, cols), 0)
            tok = lax.broadcasted_iota(jnp.int32, (tq, cols), 1) % tq
            st = jnp.where(key <= tok, st, -jnp.inf)
        m_prev = m_ref[...]
        m_new = jnp.maximum(m_prev, jnp.max(st, axis=0, keepdims=True))
        alpha = jnp.exp(m_prev - m_new)
        p = jnp.exp(st - m_new)
        l_ref[...] = alpha * l_ref[...] + jnp.sum(p, axis=0, keepdims=True)
        acc_ref[...] = alpha * acc_ref[...] + _dot(kvt_ref[j], p.astype(BF16))
        m_ref[...] = m_new

    scores(0, sa_ref)
    _softmax_init(m_ref, l_ref, acc_ref)

    def body(p, carry):
        scores(2 * p + 1, sb_ref)
        absorb(2 * p, sa_ref, False)
        scores(2 * p + 2, sa_ref)
        absorb(2 * p + 1, sb_ref, False)
        return carry

    lax.fori_loop(0, i // 2, body, 0)

    @pl.when(i % 2 == 0)
    def _():
        absorb(i, sa_ref, True)

    @pl.when(i % 2 == 1)
    def _():
        scores(i, sb_ref)
        absorb(i - 1, sa_ref, False)
        absorb(i, sb_ref, True)

    o = (acc_ref[...] / l_ref[...]).T
    o_ref[...] = o.reshape(heads, tq, KV_LORA).astype(BF16)


def _attn_prompt(q, kv, batch, seq, tq):
    heads, m, _ = q.shape
    nq = seq // tq
    kvt = kv[:, :KV_LORA].reshape(batch * nq, tq, KV_LORA).transpose(0, 2, 1)
    return pl.pallas_call(
        functools.partial(_attn_prompt_body, tq=tq),
        grid=(batch, nq),
        in_specs=[
            pl.BlockSpec((heads, tq, MLA_ROW), lambda b, i: (0, b * nq + i, 0)),
            pl.BlockSpec((seq, MLA_ROW), lambda b, i: (b, 0)),
            pl.BlockSpec((nq, KV_LORA, tq), lambda b, i: (b, 0, 0)),
        ],
        out_specs=pl.BlockSpec((heads, tq, KV_LORA), lambda b, i: (0, b * nq + i, 0)),
        out_shape=jax.ShapeDtypeStruct((heads, m, KV_LORA), BF16),
        scratch_shapes=[
            pltpu.VMEM((1, heads * tq), F32),
            pltpu.VMEM((1, heads * tq), F32),
            pltpu.VMEM((KV_LORA, heads * tq), F32),
            pltpu.VMEM((tq, heads * tq), F32),
            pltpu.VMEM((tq, heads * tq), F32),
        ],
        compiler_params=_params("parallel", "parallel"),
        name="mla_attn_prompt",
    )(q, kv, kvt)


def _attn_decode_body(pt_ref, q_ref, kn_ref, cache_ref, o_ref, kbuf_ref, sem_ref,
                      m_ref, l_ref, acc_ref, *, pages, ts, layer_slot):
    b = pl.program_id(0)
    c = pl.program_id(1)
    n_chunks = pl.num_programs(1)
    step = b * n_chunks + c
    slot = step % 2

    def page_copy(page, p, sl):
        return pltpu.make_async_copy(cache_ref.at[layer_slot, page], kbuf_ref.at[sl, p],
                                     sem_ref.at[sl])

    def start_pages(bb, cc, sl):
        def body(p, carry):
            page_copy(pt_ref[bb, cc * pages + p], p, sl).start()
            return carry
        lax.fori_loop(0, pages, body, 0, unroll=8)

    @pl.when(step == 0)
    def _():
        start_pages(0, 0, 0)

    @pl.when(step + 1 < pl.num_programs(0) * n_chunks)
    def _():
        nxt = step + 1
        start_pages(nxt // n_chunks, nxt % n_chunks, 1 - slot)

    def wait_body(p, carry):
        page_copy(0, p, slot).wait()
        return carry

    lax.fori_loop(0, pages, wait_body, 0, unroll=8)

    @pl.when(c == 0)
    def _():
        _softmax_init(m_ref, l_ref, acc_ref)

    q = q_ref[0]
    group = DEC_PAGES_PER_GROUP
    kts, scores = [], []
    for g in range(pages // group):
        kt = jnp.concatenate([kbuf_ref[slot, p].astype(BF16)
                              for p in range(g * group, (g + 1) * group)],
                             axis=1)
        kts.append(kt)
        scores.append(_dot(q, kt))
    s = jnp.concatenate(scores, axis=1)
    m_prev = m_ref[...]
    m_new = jnp.maximum(m_prev, jnp.max(s, axis=1, keepdims=True))
    alpha = jnp.exp(m_prev - m_new)
    p = jnp.exp(s - m_new)
    l_ref[...] = alpha * l_ref[...] + jnp.sum(p, axis=1, keepdims=True)
    p16 = p.astype(BF16)
    width = group * PAGE_SIZE
    pv = sum(_dot_nt(p16[:, g * width:(g + 1) * width], kt[:KV_LORA, :])
             for g, kt in enumerate(kts))
    acc_ref[...] = alpha * acc_ref[...] + pv
    m_ref[...] = m_new

    @pl.when(c == pl.num_programs(1) - 1)
    def _():
        kn = kn_ref[0]
        s = _dot_nt(q, kn)
        tok = lax.broadcasted_iota(jnp.int32, s.shape, 0) % ts
        key = lax.broadcasted_iota(jnp.int32, s.shape, 1)
        s = jnp.where(key <= tok, s, -jnp.inf)
        _softmax_update(s, kn[:, :KV_LORA], m_ref, l_ref, acc_ref)
        o_ref[0] = (acc_ref[...] / l_ref[...]).astype(BF16)


def _attn_decode(q, k_new, cache_t, page_table, layer_slot, ts):
    bs, rows, _ = q.shape
    n_pages = page_table.shape[1]
    pages = DEC_PAGES_PER_STEP
    assert n_pages % pages == 0
    grid_spec = pltpu.PrefetchScalarGridSpec(
        num_scalar_prefetch=1,
        grid=(bs, n_pages // pages),
        in_specs=[
            pl.BlockSpec((1, rows, MLA_ROW), lambda b, c, pt: (b, 0, 0)),
            pl.BlockSpec((1, k_new.shape[1], MLA_ROW), lambda b, c, pt: (b, 0, 0)),
            pl.BlockSpec(memory_space=pl.ANY),
        ],
        out_specs=pl.BlockSpec((1, rows, KV_LORA), lambda b, c, pt: (b, 0, 0)),
        scratch_shapes=[
            pltpu.VMEM((2, pages, MLA_ROW, PAGE_SIZE), cache_t.dtype),
            pltpu.SemaphoreType.DMA((2,)),
            pltpu.VMEM((rows, 1), F32),
            pltpu.VMEM((rows, 1), F32),
            pltpu.VMEM((rows, KV_LORA), F32),
        ],
    )
    return pl.pallas_call(
        functools.partial(_attn_decode_body, pages=pages, ts=ts, layer_slot=layer_slot),
        grid_spec=grid_spec,
        out_shape=jax.ShapeDtypeStruct((bs, rows, KV_LORA), BF16),
        compiler_params=_params("arbitrary", "arbitrary"),
        name="mla_attn_decode",
    )(page_table, q, k_new, cache_t)


def _layer_tail_body(*refs, from_latent, d_ff, chunk):
    if from_latent:
        (a_ref, wuv_ref, wo_ref, h_ref, nw_mix_ref, nw_in_ref, win_ref, wout_ref, nw_out_ref,
         out_ref) = refs
    else:
        (a_ref, wo_ref, h_ref, nw_mix_ref, nw_in_ref, win_ref, wout_ref, nw_out_ref,
         out_ref) = refs
    tm = h_ref.shape[0]
    n_chains = TAIL_ROW_CHAINS if tm % (TAIL_ROW_CHAINS * 2 * SUBLANES) == 0 else 1
    rows_per = tm // n_chains
    chains = [dict(rows=slice(r * rows_per, (r + 1) * rows_per)) for r in range(n_chains)]
    for ch in chains:
        rows = ch["rows"]
        if from_latent:
            a16 = jnp.concatenate(
                [_dot(a_ref[h, rows, :], wuv_ref[h]).astype(BF16) for h in range(MLA_HEADS)],
                axis=1)
        else:
            a16 = a_ref[rows, :]
        ch["mix"] = _dot(a16, wo_ref[...])
    for ch in chains:
        x = h_ref[ch["rows"], :] + _rms(ch["mix"], nw_mix_ref[...])
        ch.update(x=x, u=_rms(x, nw_in_ref[...]).astype(BF16), acc=jnp.zeros(x.shape, F32))
    for c in range(d_ff // chunk):
        lo = c * chunk
        for ch in chains:
            gate = _dot(ch["u"], win_ref[0, :, lo:lo + chunk])
            up = _dot(ch["u"], win_ref[0, :, d_ff + lo:d_ff + lo + chunk])
            ch["act"] = (_silu(gate) * up).astype(BF16)
        for ch in chains:
            ch["acc"] = ch["acc"] + _dot(ch["act"], wout_ref[0, lo:lo + chunk, :])
    for ch in chains:
        out_ref[ch["rows"], :] = ch["x"] + _rms(ch["acc"], nw_out_ref[...])


def _layer_tail(a16, w_uv, w_o, h, nw_mix, ffn, tm):
    nw_in, w_in, w_out, layer, nw_out = ffn
    m = h.shape[0]
    d_ff = w_out.shape[1]
    assert d_ff % FFN_CHUNK == 0
    from_latent = w_uv is not None
    row = lambda i: (i, 0)
    if from_latent:
        in_specs = [pl.BlockSpec((MLA_HEADS, tm, KV_LORA), lambda i: (0, i, 0)),
                    _const_spec(w_uv.shape)]
        args = [a16, w_uv]
    else:
        in_specs = [pl.BlockSpec((tm, a16.shape[1]), row)]
        args = [a16]
    in_specs += [
        _const_spec(w_o.shape),
        pl.BlockSpec((tm, D_MODEL), row),
        _const_spec((1, D_MODEL)),
        _const_spec((1, D_MODEL)),
        _fixed_spec((1,) + w_in.shape[1:], (layer, 0, 0)),
        _fixed_spec((1,) + w_out.shape[1:], (layer, 0, 0)),
        _const_spec((1, D_MODEL)),
    ]
    args += [w_o, h, nw_mix, nw_in, w_in, w_out, nw_out]
    return pl.pallas_call(
        functools.partial(_layer_tail_body, from_latent=from_latent, d_ff=d_ff, chunk=FFN_CHUNK),
        grid=(m // tm,),
        in_specs=in_specs,
        out_specs=pl.BlockSpec((tm, D_MODEL), row),
        out_shape=jax.ShapeDtypeStruct((m, D_MODEL), F32),
        compiler_params=_params("parallel"),
        name="layer_tail",
    )(*args)


def _dn_proj_body(*refs, tm, seq_len, has_hist, tail_rows):
    if has_hist:
        (x_ref, nw_ref, wqkv_ref, wz_ref, wba_ref, cw_ref, alog_ref, dtb_ref, hist_ref,
         qkv_ref, z_ref, bg_ref, tail_ref, ext_ref) = refs
    else:
        (x_ref, nw_ref, wqkv_ref, wz_ref, wba_ref, cw_ref, alog_ref, dtb_ref,
         qkv_ref, z_ref, bg_ref, tail_ref, ext_ref) = refs
        hist_ref = None
    i = pl.program_id(0)
    carry = CONV_CARRY_ROWS

    @pl.when((i * tm) % seq_len == 0)
    def _():
        ext_ref[0:carry, :] = jnp.zeros((carry, DN_QKV), F32)

    u = _rms(x_ref[...], nw_ref[...]).astype(BF16)
    assert carry == SUBLANES and tm % (DN_PROJ_ROW_CHAINS * SUBLANES) == 0
    rows_per = tm // DN_PROJ_ROW_CHAINS
    assert rows_per % seq_len == 0 or not has_hist
    raws = []
    for r in range(DN_PROJ_ROW_CHAINS):
        raws.append(_dot(u[r * rows_per:(r + 1) * rows_per], wqkv_ref[...]))
        if r == DN_PROJ_GATE_SLOT:
            z_ref[...] = _dot(u, wz_ref[...])
            ba = _dot(u, wba_ref[...])
    sub = lax.broadcasted_iota(jnp.int32, (1, SUBLANES, 1), 1)
    cw = cw_ref[...]
    before = ext_ref[...]
    for r, raw in enumerate(raws):
        rows = slice(r * rows_per, (r + 1) * rows_per)
        cur = raw.reshape(rows_per // SUBLANES, SUBLANES, DN_QKV)
        prev = jnp.concatenate([before.reshape(1, SUBLANES, DN_QKV), cur[:-1]], axis=0)
        acc = raw * cw[CONV_W - 1:CONV_W, :]
        for k in range(1, CONV_W):
            mixed = jnp.where(sub >= SUBLANES - k, prev, cur)
            shifted = pltpu.roll(mixed, k, axis=1).reshape(rows_per, DN_QKV)
            if has_hist:
                tpos = lax.broadcasted_iota(jnp.int32, (rows_per, 1), 0) % seq_len
                shifted = jnp.where(tpos >= k, shifted, hist_ref[k - 1, rows, :])
            acc = acc + shifted * cw[CONV_W - 1 - k:CONV_W - k, :]
        before = raw[rows_per - carry:, :]
        act = _silu(acc)
        for h in range(DN_HEADS):
            qh = act[:, h * DN_DK:(h + 1) * DN_DK]
            kh = act[:, DN_HK + h * DN_DK:DN_HK + (h + 1) * DN_DK]
            qn = qh * lax.rsqrt(jnp.sum(qh * qh, axis=-1, keepdims=True) + L2_EPS) * (DN_DK ** -0.5)
            kn = kh * lax.rsqrt(jnp.sum(kh * kh, axis=-1, keepdims=True) + L2_EPS)
            qkv_ref[rows, h * DN_DK:(h + 1) * DN_DK] = qn
            qkv_ref[rows, DN_HK + h * DN_DK:DN_HK + (h + 1) * DN_DK] = kn
        qkv_ref[rows, 2 * DN_HK:] = act[:, 2 * DN_HK:]
    ext_ref[...] = before
    tail_ref[0] = jnp.concatenate(raws, axis=0)[tm - tail_rows:, :]
    beta = _sigmoid(ba[:, :DN_HEADS])
    g = -jnp.exp(alog_ref[...]) * _softplus(ba[:, DN_HEADS:2 * DN_HEADS] + dtb_ref[...])
    bg_ref[:, :DN_HEADS] = beta
    bg_ref[:, DN_HEADS:] = g


def _dn_project(x, nw, w_all, w_ba, conv_w, a_log, dt_bias, hist, tm, seq_len, tail_rows):
    assert DN_QKV % DN_Z == 0
    m = x.shape[0]
    n_tiles = m // tm
    has_hist = hist is not None
    assert (tm % seq_len == 0 and n_tiles == 1) if has_hist else seq_len % tm == 0
    tiles_per_tail = max(seq_len // tm, 1)
    row = lambda i: (i, 0)
    in_specs = [
        pl.BlockSpec((tm, D_MODEL), row),
        _const_spec((1, D_MODEL)),
        _fixed_spec((D_MODEL, DN_QKV), (0, 0)),
        _fixed_spec((D_MODEL, DN_Z), (0, DN_QKV // DN_Z)),
        _const_spec(w_ba.shape),
        _const_spec(conv_w.shape),
        _const_spec((1, DN_HEADS)),
        _const_spec((1, DN_HEADS)),
    ]
    args = [x, nw, w_all, w_all, w_ba, conv_w, a_log, dt_bias]
    if has_hist:
        in_specs.append(pl.BlockSpec((CONV_W - 1, tm, DN_QKV), lambda i: (0, i, 0)))
        args.append(hist)
    return pl.pallas_call(
        functools.partial(_dn_proj_body, tm=tm, seq_len=seq_len, has_hist=has_hist,
                          tail_rows=tail_rows),
        grid=(n_tiles,),
        in_specs=in_specs,
        out_specs=[
            pl.BlockSpec((tm, DN_QKV), row),
            pl.BlockSpec((tm, DN_Z), row),
            pl.BlockSpec((tm, 2 * DN_HEADS), row),
            pl.BlockSpec((1, tail_rows, DN_QKV), lambda i: (i // tiles_per_tail, 0, 0)),
        ],
        out_shape=[
            jax.ShapeDtypeStruct((m, DN_QKV), F32),
            jax.ShapeDtypeStruct((m, DN_Z), F32),
            jax.ShapeDtypeStruct((m, 2 * DN_HEADS), F32),
            jax.ShapeDtypeStruct((n_tiles // tiles_per_tail, tail_rows, DN_QKV), F32),
        ],
        scratch_shapes=[pltpu.VMEM((CONV_CARRY_ROWS, DN_QKV), F32)],
        compiler_params=_params("arbitrary"),
        name="dn_project",
    )(*args)


def _stack_heads(load, g):
    return jnp.concatenate(
        [load(slice((g * GDN_GROUP + hh) * DN_DK, (g * GDN_GROUP + hh + 1) * DN_DK))
         for hh in range(GDN_GROUP)], axis=0)


def _gdn_local_body(q_ref, k_ref, v_ref, bg_ref, bgt_ref,
                    u_ref, w_ref, qd_ref, kdt_ref, qk_ref, egl_ref, *, chunk, chunks_per_step):
    n_heads = DN_HEADS
    stack = GDN_GROUP * chunk
    ri = lax.broadcasted_iota(jnp.int32, (stack, stack), 0)
    ci = lax.broadcasted_iota(jnp.int32, (stack, stack), 1)
    same_head = (ri // chunk) == (ci // chunk)
    causal = same_head & (ri >= ci)
    strict = same_head & (ri > ci)
    r1 = lax.broadcasted_iota(jnp.int32, (chunk, chunk), 0)
    c1 = lax.broadcasted_iota(jnp.int32, (chunk, chunk), 1)
    tril = jnp.where(r1 >= c1, 1.0, 0.0).astype(BF16)
    triu = jnp.where(r1 <= c1, 1.0, 0.0).astype(BF16)
    n_double = int(math.log2(chunk)) - 2

    def store_heads(ref, rows, g, stacked):
        for hh in range(GDN_GROUP):
            h = g * GDN_GROUP + hh
            ref[rows, h * DN_DK:(h + 1) * DN_DK] = stacked[hh * chunk:(hh + 1) * chunk, :]

    chains = []
    for cp in range(chunks_per_step):
        rows = slice(cp * chunk, (cp + 1) * chunk)
        bg = bg_ref[rows, :]
        bgt = bgt_ref[cp]
        gc_col = sum(_dot(tril, part) for part in _split3(bg))
        gc_row = sum(_dot(part, triu) for part in _split3(bgt))
        egl_ref[cp] = jnp.broadcast_to(jnp.exp(gc_row[n_heads:, chunk - 1:chunk]),
                                       (n_heads, LANES))
        for g in range(n_heads // GDN_GROUP):
            heads = [g * GDN_GROUP + hh for hh in range(GDN_GROUP)]
            q = _stack_heads(lambda sl: q_ref[rows, sl], g)
            k = _stack_heads(lambda sl: k_ref[rows, sl], g)
            v = _stack_heads(lambda sl: v_ref[rows, sl], g)
            beta = jnp.concatenate([bg[:, h:h + 1] for h in heads], axis=0)
            gcc = jnp.concatenate([gc_col[:, n_heads + h:n_heads + h + 1] for h in heads], axis=0)
            gcr = jnp.concatenate([gc_row[n_heads + h:n_heads + h + 1, :] for h in heads], axis=1)
            g_last = jnp.concatenate(
                [jnp.broadcast_to(gc_col[chunk - 1:chunk, n_heads + h:n_heads + h + 1], (chunk, 1))
                 for h in heads], axis=0)
            eg = jnp.exp(gcc)
            kb = k * beta
            store_heads(qd_ref, rows, g, (q * eg).astype(BF16))
            kdt_ref[cp, :, g * stack:(g + 1) * stack] = (k * jnp.exp(g_last - gcc)).T.astype(BF16)
            chains.append(dict(
                rows=rows, g=g, q16=q.astype(BF16), k16=k.astype(BF16), kb16=kb.astype(BF16),
                decay=jnp.where(causal, jnp.exp(gcc - gcr), 0.0),
                rhs=jnp.concatenate([v * beta, kb * eg], axis=1)))

    for ch in chains:
        lower = jnp.where(strict, _dot_nt(ch["kb16"], ch["k16"]) * ch["decay"], 0.0)
        ch.update(lower=lower, power=lower, off=-lower)
    for ch in chains:
        qk = jnp.where(causal, _dot_nt(ch["q16"], ch["k16"]) * ch["decay"], 0.0)
        qk_packed = sum(qk[hh * chunk:(hh + 1) * chunk, :] for hh in range(GDN_GROUP))
        qk_ref[ch["rows"], ch["g"] * stack:(ch["g"] + 1) * stack] = qk_packed.astype(BF16)
    for _ in range(n_double):
        for ch in chains:
            p16 = ch["power"].astype(BF16)
            ch["power"] = _dot(p16, p16)
        for ch in chains:
            ch["off"] = ch["off"] + ch["power"] + _dot(ch["off"].astype(BF16),
                                                       ch["power"].astype(BF16))
    for ch in chains:
        ch["resid"] = (ch["lower"] + ch["off"]) + _dot_hl(ch["lower"], ch["off"])
    for ch in chains:
        ch["off"] = ch["off"] - ch["resid"] - _dot(ch["off"].astype(BF16),
                                                   ch["resid"].astype(BF16))
    for ch in chains:
        sol = ch["rhs"] + _dot(ch["off"].astype(BF16), ch["rhs"].astype(BF16))
        store_heads(u_ref, ch["rows"], ch["g"], sol[:, :DN_DV])
        store_heads(w_ref, ch["rows"], ch["g"], sol[:, DN_DV:].astype(BF16))


def _gdn_scan_body(u_ref, w_ref, qd_ref, kdt_ref, qk_ref, egl_ref, z_ref, gout_ref, s0_ref,
                   y_ref, s_ref, *, chunk, seqs):
    c = pl.program_id(1)

    @pl.when(c == 0)
    def _():
        s_ref[...] = s0_ref[...]

    stack = GDN_GROUP * chunk
    pair = 2 * DN_DK
    lhs_mask = ((lax.broadcasted_iota(jnp.int32, (4 * chunk, pair), 0) // chunk) % 2
                == lax.broadcasted_iota(jnp.int32, (4 * chunk, pair), 1) // DN_DK)
    qk_mask = (lax.broadcasted_iota(jnp.int32, (stack, stack), 0) // chunk
               == lax.broadcasted_iota(jnp.int32, (stack, stack), 1) // chunk)
    kd_mask = (lax.broadcasted_iota(jnp.int32, (GDN_GROUP * DN_DK, stack), 0) // DN_DK
               == lax.broadcasted_iota(jnp.int32, (GDN_GROUP * DN_DK, stack), 1) // chunk)
    zero16 = jnp.zeros((), BF16)
    chains = [dict(sq=sq, g=g) for sq in range(seqs) for g in range(DN_HEADS // GDN_GROUP)]
    for ch in chains:
        sq, g = ch["sq"], ch["g"]
        s = s_ref[sq, g * GDN_GROUP:(g + 1) * GDN_GROUP].reshape(GDN_GROUP * DN_DK, DN_DV)
        s16 = s.astype(BF16)
        ws, qs = [], []
        for p in range(GDN_GROUP // 2):
            cols = [slice((g * GDN_GROUP + 2 * p + e) * DN_DK, (g * GDN_GROUP + 2 * p + e + 1) * DN_DK)
                    for e in range(2)]
            lhs = jnp.concatenate([w_ref[sq, :, cols[0]], w_ref[sq, :, cols[1]],
                                   qd_ref[sq, :, cols[0]], qd_ref[sq, :, cols[1]]], axis=0)
            lhs_bd = jnp.where(lhs_mask, jnp.concatenate([lhs, lhs], axis=1), zero16)
            res = _dot(lhs_bd, s16[p * pair:(p + 1) * pair, :])
            ws.append(res[:2 * chunk])
            qs.append(res[2 * chunk:])
        ch.update(s=s, ws=jnp.concatenate(ws, axis=0), qs=jnp.concatenate(qs, axis=0))
    for ch in chains:
        sq, g = ch["sq"], ch["g"]
        v_new = _stack_heads(lambda sl: u_ref[sq, :, sl], g) - ch["ws"]
        ch["v16"] = v_new.astype(BF16)
    gout = gout_ref[...]
    for ch in chains:
        sq, g = ch["sq"], ch["g"]
        qk = qk_ref[sq, :, g * stack:(g + 1) * stack]
        qk_bd = jnp.where(qk_mask, jnp.concatenate([qk] * GDN_GROUP, axis=0), zero16)
        o = ch["qs"] + _dot(qk_bd, ch["v16"])
        y = _rms(o, gout) * _silu(_stack_heads(lambda sl: z_ref[sq, :, sl], g))
        y16 = y.astype(BF16)
        for hh in range(GDN_GROUP):
            h = g * GDN_GROUP + hh
            y_ref[sq, :, h * DN_DV:(h + 1) * DN_DV] = y16[hh * chunk:(hh + 1) * chunk, :]
    for ch in chains:
        sq, g = ch["sq"], ch["g"]
        kdt = kdt_ref[sq, 0, :, g * stack:(g + 1) * stack]
        kdt_bd = jnp.where(kd_mask, jnp.concatenate([kdt] * GDN_GROUP, axis=0), zero16)
        gate = jnp.concatenate(
            [jnp.broadcast_to(egl_ref[sq, 0, g * GDN_GROUP + hh:g * GDN_GROUP + hh + 1, :],
                              (DN_DK, DN_DV)) for hh in range(GDN_GROUP)], axis=0)
        s_new = ch["s"] * gate + _dot(kdt_bd, ch["v16"])
        s_ref[sq, g * GDN_GROUP:(g + 1) * GDN_GROUP] = s_new.reshape(GDN_GROUP, DN_DK, DN_DV)


def _gdn(qkv, bg, z, g_out, s0, n_seq, seq_len, chunk):
    m = qkv.shape[0]
    n_chunks = m // chunk
    nc = seq_len // chunk
    cps = GDN_CHUNKS_PER_STEP
    sps = GDN_SEQS_PER_STEP
    assert n_chunks % cps == 0 and n_seq % sps == 0
    bgt = bg.reshape(n_chunks, chunk, 2 * DN_HEADS).transpose(0, 2, 1)
    half = DN_HEADS * chunk
    row = lambda i: (i, 0)
    blk = lambda col: pl.BlockSpec((cps * chunk, DN_HK), lambda i, col=col: (i, col))
    u, w16, qd16, kdt16, qk16, egl = pl.pallas_call(
        functools.partial(_gdn_local_body, chunk=chunk, chunks_per_step=cps),
        grid=(n_chunks // cps,),
        in_specs=[
            blk(0), blk(1), blk(2),
            pl.BlockSpec((cps * chunk, 2 * DN_HEADS), row),
            pl.BlockSpec((cps, 2 * DN_HEADS, chunk), lambda i: (i, 0, 0)),
        ],
        out_specs=[
            pl.BlockSpec((cps * chunk, DN_Z), row),
            pl.BlockSpec((cps * chunk, DN_HK), row),
            pl.BlockSpec((cps * chunk, DN_HK), row),
            pl.BlockSpec((cps, DN_DK, half), lambda i: (i, 0, 0)),
            pl.BlockSpec((cps * chunk, half), row),
            pl.BlockSpec((cps, DN_HEADS, LANES), lambda i: (i, 0, 0)),
        ],
        out_shape=[
            jax.ShapeDtypeStruct((m, DN_Z), F32),
            jax.ShapeDtypeStruct((m, DN_HK), BF16),
            jax.ShapeDtypeStruct((m, DN_HK), BF16),
            jax.ShapeDtypeStruct((n_chunks, DN_DK, half), BF16),
            jax.ShapeDtypeStruct((m, half), BF16),
            jax.ShapeDtypeStruct((n_chunks, DN_HEADS, LANES), F32),
        ],
        compiler_params=_params("parallel"),
        name="gdn_local",
    )(qkv, qkv, qkv, bg, bgt)

    per_seq = lambda a: a.reshape((n_seq, nc if a.ndim == 3 else seq_len) + a.shape[1:])
    tok_spec = lambda width: pl.BlockSpec((sps, chunk, width), lambda b, c: (b, c, 0))
    lead_spec = lambda d1, d2: pl.BlockSpec((sps, 1, d1, d2), lambda b, c: (b, c, 0, 0))
    state_spec = pl.BlockSpec((sps, DN_HEADS, DN_DK, DN_DV), lambda b, c: (b, 0, 0, 0))
    y16, s = pl.pallas_call(
        functools.partial(_gdn_scan_body, chunk=chunk, seqs=sps),
        grid=(n_seq // sps, nc),
        in_specs=[
            tok_spec(DN_Z),
            tok_spec(DN_HK),
            tok_spec(DN_HK),
            lead_spec(DN_DK, half),
            tok_spec(half),
            lead_spec(DN_HEADS, LANES),
            tok_spec(DN_Z),
            pl.BlockSpec((1, DN_DV), lambda b, c: (0, 0)),
            state_spec,
        ],
        out_specs=[tok_spec(DN_Z), state_spec],
        out_shape=[
            jax.ShapeDtypeStruct((n_seq, seq_len, DN_Z), BF16),
            jax.ShapeDtypeStruct(s0.shape, F32),
        ],
        compiler_params=_params("parallel", "arbitrary"),
        name="gdn_scan",
    )(per_seq(u), per_seq(w16), per_seq(qd16), per_seq(kdt16), per_seq(qk16), per_seq(egl),
      per_seq(z), g_out, s0)
    return y16.reshape(m, DN_Z), s


def _mla_layer(hp, hs, cache_mla, page_table, slot, nw, w_in, g_q, g_kv, w_uq, w_uk, w_uv, w_o,
               ffn, bp, tp, bs, ts):
    past = page_table.shape[1] * PAGE_SIZE
    w_in16 = w_in.astype(BF16)
    w_uq_heads = w_uq.reshape(Q_LORA, MLA_HEADS, QK_NOPE + QK_ROPE)
    w_uq16 = jnp.concatenate(
        [w_uq_heads[:, :, :QK_NOPE].reshape(Q_LORA, MLA_HEADS * QK_NOPE),
         w_uq_heads[:, :, QK_NOPE:].reshape(Q_LORA, MLA_HEADS * QK_ROPE)], axis=1).astype(BF16)
    w_ukt16 = jnp.swapaxes(w_uk, 1, 2).astype(BF16)
    w_uv16 = w_uv.astype(BF16)
    w_o16 = w_o.astype(BF16)
    g_q = g_q.reshape(1, Q_LORA)
    g_kv = g_kv.reshape(1, KV_LORA)
    nw0 = nw[0].reshape(1, D_MODEL)
    nw1 = nw[1].reshape(1, D_MODEL)

    cs_p, sn_p = _rope_tables(jnp.arange(tp))
    cs_s, sn_s = _rope_tables(past + jnp.tile(jnp.arange(ts), bs))
    ms = bs * ts
    rows_p, kv_p, q_p = _mla_project(hp, nw0, w_in16, g_q, g_kv, w_uq16, w_ukt16, cs_p, sn_p,
                                     TM_TOKENS)
    rows_s, kv_s, q_s = _mla_project(hs, nw0, w_in16, g_q, g_kv, w_uq16, w_ukt16, cs_s, sn_s, ms)

    o_p = _attn_prompt(q_p, kv_p, bp, tp, TQ_ATTN)

    q_sb = q_s.reshape(MLA_HEADS, bs, ts, MLA_ROW).transpose(1, 0, 2, 3).reshape(
        bs, MLA_HEADS * ts, MLA_ROW)
    new_rows = 2 * SUBLANES
    k_new = jnp.pad(kv_s.reshape(bs, ts, MLA_ROW), ((0, 0), (0, new_rows - ts), (0, 0)))
    o_sb = _attn_decode(q_sb, k_new, jnp.swapaxes(cache_mla, 2, 3), page_table, slot, ts)
    o_s = o_sb.reshape(bs, MLA_HEADS, ts, KV_LORA).transpose(1, 0, 2, 3).reshape(
        MLA_HEADS, ms, KV_LORA)

    hp = _layer_tail(o_p, w_uv16, w_o16, hp, nw1, ffn, TM_TOKENS)
    hs = _layer_tail(o_s, w_uv16, w_o16, hs, nw1, ffn, ms)
    return hp, hs, rows_p.reshape(bp, tp, MLA_ROW), rows_s.reshape(bs, ts, MLA_ROW)


def _dn_layer(hp, hs, s0_s, conv0_s, nw, w_in, conv_w, a_log, dt_bias, g_out, w_o, ffn,
              bp, tp, bs, ts):
    w_in16 = w_in.astype(BF16)
    w_ba16 = jnp.pad(w_in[:, DN_QKV + DN_Z:], ((0, 0), (0, LANES - 2 * DN_HEADS))).astype(BF16)
    w_o16 = w_o.astype(BF16)
    a_log = a_log.reshape(1, DN_HEADS).astype(F32)
    dt_bias = dt_bias.reshape(1, DN_HEADS).astype(F32)
    g_out = g_out.reshape(1, DN_DV)
    nw0 = nw[0].reshape(1, D_MODEL)
    nw1 = nw[1].reshape(1, D_MODEL)
    hist_rows = CONV_W - 1
    ms = bs * ts

    qkv_p, z_p, bg_p, tail_p = _dn_project(hp, nw0, w_in16, w_ba16, conv_w, a_log, dt_bias,
                                           None, TM_DN_PROJ, tp, SUBLANES)
    conv_p = tail_p[:, SUBLANES - hist_rows:, :]
    s0_p = jnp.zeros((bp, DN_HEADS, DN_DK, DN_DV), F32)
    y_p, s_p = _gdn(qkv_p, bg_p, z_p, g_out, s0_p, bp, tp, DN_CHUNK)
    hp = _layer_tail(y_p, None, w_o16, hp, nw1, ffn, TM_TOKENS)

    tok = jnp.arange(ts)
    hist = jnp.stack([
        conv0_s[:, jnp.clip(hist_rows - k + tok, 0, hist_rows - 1), :].reshape(ms, DN_QKV)
        for k in range(1, CONV_W)
    ]).astype(F32)
    qkv_s, z_s, bg_s, tail_s = _dn_project(hs, nw0, w_in16, w_ba16, conv_w, a_log, dt_bias,
                                           hist, ms, ts, ms)
    raw_s = tail_s.reshape(bs, ts, DN_QKV)
    conv_s = jnp.concatenate([conv0_s.astype(F32), raw_s], axis=1)[:, ts:, :]
    cs = DN_CHUNK_SAMPLE
    assert ts <= cs
    pad = lambda a: jnp.pad(a.reshape(bs, ts, -1), ((0, 0), (0, cs - ts), (0, 0))).reshape(
        bs * cs, -1)
    y_s_pad, s_s = _gdn(pad(qkv_s), pad(bg_s), pad(z_s), g_out, s0_s.astype(F32), bs, cs, cs)
    y_s = y_s_pad.reshape(bs, cs, DN_Z)[:, :ts, :].reshape(ms, DN_Z)
    hs = _layer_tail(y_s, None, w_o16, hs, nw1, ffn, ms)
    return hp, hs, s_p, s_s, conv_p, conv_s


def kernel(x_prompt, x_sample, cache_mla, state_dn, state_dn_conv, page_table, norm_w, mla_w_in,
           mla_g_q, mla_g_kv, mla_w_uq, mla_w_uk, mla_w_uv, mla_w_o, dn_w_in, dn_conv_w, dn_a_log,
           dn_dt_bias, dn_g_out, dn_w_o, ffn_w_in, ffn_w_out):
    bp, tp, _ = x_prompt.shape
    bs, ts, _ = x_sample.shape
    depth = norm_w.shape[0]
    hp = x_prompt.reshape(bp * tp, D_MODEL)
    hs = x_sample.reshape(bs * ts, D_MODEL)
    rows_p_l, rows_s_l, sp_l, ss_l, cp_l, cs_l = [], [], [], [], [], []
    ffn_w_in16 = ffn_w_in.astype(BF16)
    ffn_w_out16 = ffn_w_out.astype(BF16)
    for layer in range(depth):
        j = layer // N_MIXERS
        nw = norm_w[layer]
        ffn = (nw[2].reshape(1, D_MODEL), ffn_w_in16, ffn_w_out16, layer, nw[3].reshape(1, D_MODEL))
        if layer % N_MIXERS == 0:
            hp, hs, rows_p, rows_s = _mla_layer(
                hp, hs, cache_mla, page_table, j, nw, mla_w_in[j], mla_g_q[j], mla_g_kv[j],
                mla_w_uq[j], mla_w_uk[j], mla_w_uv[j], mla_w_o[j], ffn, bp, tp, bs, ts)
            rows_p_l.append(rows_p)
            rows_s_l.append(rows_s)
        else:
            hp, hs, s_p, s_s, c_p, c_s = _dn_layer(
                hp, hs, state_dn[j], state_dn_conv[j], nw, dn_w_in[j], dn_conv_w[j], dn_a_log[j],
                dn_dt_bias[j], dn_g_out[j], dn_w_o[j], ffn, bp, tp, bs, ts)
            sp_l.append(s_p.astype(state_dn.dtype))
            ss_l.append(s_s.astype(state_dn.dtype))
            cp_l.append(c_p.astype(state_dn_conv.dtype))
            cs_l.append(c_s.astype(state_dn_conv.dtype))
    return (hp.reshape(bp, tp, D_MODEL), hs.reshape(bs, ts, D_MODEL),
            jnp.stack(rows_p_l), jnp.stack(rows_s_l), jnp.stack(sp_l), jnp.stack(ss_l),
            jnp.stack(cp_l), jnp.stack(cs_l))
```

```python
import functools
import math

import jax
import jax.numpy as jnp
from jax import lax
from jax.experimental import pallas as pl
from jax.experimental.pallas import tpu as pltpu

F32 = jnp.float32
BF16 = jnp.bfloat16

D_MODEL = 1024
PAGE_SIZE = 128
N_MIXERS = 2

MLA_HEADS = 8
QK_NOPE = 128
QK_ROPE = 64
V_HEAD = 128
KV_LORA = 256
Q_LORA = 384
MLA_ROW = KV_LORA + QK_ROPE
MLA_SCALE = (QK_NOPE + QK_ROPE) ** -0.5
ROPE_THETA = 10000.0

DN_HEADS = 8
DN_DK = 128
DN_DV = 128
CONV_W = 4
DN_HK = DN_HEADS * DN_DK
DN_QKV = DN_HEADS * (2 * DN_DK + DN_DV)
DN_Z = DN_HEADS * DN_DV

RMS_EPS = 1e-6
L2_EPS = 1e-6

LANES = 128
SUBLANES = 8
VMEM_LIMIT_BYTES = 56 * 1024 * 1024

TM_TOKENS = 512
MLA_PROJ_ROW_CHAINS = 4
TAIL_ROW_CHAINS = 4
DN_PROJ_ROW_CHAINS = 4
DN_PROJ_GATE_SLOT = 0
TM_DN_PROJ = 512
TQ_ATTN = 256
DN_CHUNK = 64
DN_CHUNK_SAMPLE = 16
GDN_GROUP = 4
GDN_CHUNKS_PER_STEP = 4
GDN_SEQS_PER_STEP = 4
DEC_PAGES_PER_STEP = 128
DEC_PAGES_PER_GROUP = 8
FFN_CHUNK = 256
CONV_CARRY_ROWS = SUBLANES


def _params(*sem):
    return pltpu.CompilerParams(dimension_semantics=sem, vmem_limit_bytes=VMEM_LIMIT_BYTES)


def _fixed_spec(block_shape, block_index):
    return pl.BlockSpec(block_shape, lambda *_: block_index, pipeline_mode=pl.Buffered(1))


def _const_spec(shape):
    return _fixed_spec(shape, (0,) * len(shape))


def _dot(a, b):
    return jnp.dot(a, b, preferred_element_type=F32)


def _dot_nt(a, b):
    return lax.dot_general(a, b, (((1,), (1,)), ((), ())), preferred_element_type=F32)


def _dot_tn(a, b):
    return lax.dot_general(a, b, (((0,), (0,)), ((), ())), preferred_element_type=F32)


def _split2(x):
    hi = x.astype(BF16)
    lo = (x - hi.astype(F32)).astype(BF16)
    return hi, lo


def _split3(x):
    hi = x.astype(BF16)
    r = x - hi.astype(F32)
    mid = r.astype(BF16)
    lo = (r - mid.astype(F32)).astype(BF16)
    return hi, mid, lo


def _dot_hl(a, b):
    ah, al = _split2(a)
    bh, bl = _split2(b)
    return _dot(ah, bh) + (_dot(ah, bl) + _dot(al, bh))


def _rms(x, w):
    return x * lax.rsqrt(jnp.mean(x * x, axis=-1, keepdims=True) + RMS_EPS) * w


def _sigmoid(x):
    return 1.0 / (1.0 + jnp.exp(-x))


def _silu(x):
    return x * _sigmoid(x)


def _softplus(x):
    return jnp.maximum(x, 0.0) + jnp.log1p(jnp.exp(-jnp.abs(x)))


def _rope(x, cs, sn):
    half = x.shape[-1] // 2
    swapped = jnp.concatenate([x[:, half:], x[:, :half]], axis=1)
    return x * cs + swapped * sn


def _rope_tables(pos):
    half = QK_ROPE // 2
    freq = ROPE_THETA ** (-jnp.arange(half, dtype=F32) / half)
    ang = pos.astype(F32)[:, None] * freq[None, :]
    cos, sin = jnp.cos(ang), jnp.sin(ang)
    return jnp.concatenate([cos, cos], axis=1), jnp.concatenate([-sin, sin], axis=1)


def _mla_proj_body(x_ref, nw_ref, win_ref, gq_ref, gkv_ref, wuq_ref, wukt_ref, cs_ref, sn_ref,
                   rows_ref, kv_ref, q_ref):
    tm = x_ref.shape[0]
    n_chains = MLA_PROJ_ROW_CHAINS if tm % (MLA_PROJ_ROW_CHAINS * 2 * SUBLANES) == 0 else 1
    rows_per = tm // n_chains
    chains = [dict(rows=slice(r * rows_per, (r + 1) * rows_per)) for r in range(n_chains)]
    rope_base = MLA_HEADS * QK_NOPE
    for ch in chains:
        u = _rms(x_ref[ch["rows"], :], nw_ref[...]).astype(BF16)
        ch["a"] = _dot(u, win_ref[...])
    for ch in chains:
        rows, a = ch["rows"], ch["a"]
        ch["cs"] = cs_ref[rows, :]
        ch["sn"] = sn_ref[rows, :]
        c_q = _rms(a[:, :Q_LORA], gq_ref[...]).astype(BF16)
        ch["q"] = _dot(c_q, wuq_ref[...])
    for ch in chains:
        rows, a = ch["rows"], ch["a"]
        c_kv = _rms(a[:, Q_LORA:Q_LORA + KV_LORA], gkv_ref[...])
        k_r = _rope(a[:, Q_LORA + KV_LORA:], ch["cs"], ch["sn"])
        rows_ref[rows, :KV_LORA] = c_kv
        rows_ref[rows, KV_LORA:] = k_r
        kv_ref[rows, :KV_LORA] = c_kv.astype(BF16)
        kv_ref[rows, KV_LORA:] = k_r.astype(BF16)
    for h in range(MLA_HEADS):
        for ch in chains:
            rows, q = ch["rows"], ch["q"]
            q_lat = _dot(q[:, h * QK_NOPE:(h + 1) * QK_NOPE].astype(BF16), wukt_ref[h])
            q_rope = _rope(q[:, rope_base + h * QK_ROPE:rope_base + (h + 1) * QK_ROPE],
                           ch["cs"], ch["sn"])
            q_ref[h, rows, :KV_LORA] = (q_lat * MLA_SCALE).astype(BF16)
            q_ref[h, rows, KV_LORA:] = (q_rope * MLA_SCALE).astype(BF16)


def _mla_project(x, nw, w_in, g_q, g_kv, w_uq, w_ukt, cs, sn, tm):
    m = x.shape[0]
    period_tiles = cs.shape[0] // tm
    assert cs.shape[0] % tm == 0 and m % cs.shape[0] == 0
    row = lambda i: (i, 0)
    pos_row = lambda i: (i % period_tiles, 0)
    return pl.pallas_call(
        _mla_proj_body,
        grid=(m // tm,),
        in_specs=[
            pl.BlockSpec((tm, D_MODEL), row),
            _const_spec((1, D_MODEL)),
            _const_spec(w_in.shape),
            _const_spec((1, Q_LORA)),
            _const_spec((1, KV_LORA)),
            _const_spec(w_uq.shape),
            _const_spec(w_ukt.shape),
            pl.BlockSpec((tm, QK_ROPE), pos_row),
            pl.BlockSpec((tm, QK_ROPE), pos_row),
        ],
        out_specs=[
            pl.BlockSpec((tm, MLA_ROW), row),
            pl.BlockSpec((tm, MLA_ROW), row),
            pl.BlockSpec((MLA_HEADS, tm, MLA_ROW), lambda i: (0, i, 0)),
        ],
        out_shape=[
            jax.ShapeDtypeStruct((m, MLA_ROW), F32),
            jax.ShapeDtypeStruct((m, MLA_ROW), BF16),
            jax.ShapeDtypeStruct((MLA_HEADS, m, MLA_ROW), BF16),
        ],
        compiler_params=_params("parallel"),
        name="mla_project",
    )(x, nw, w_in, g_q, g_kv, w_uq, w_ukt, cs, sn)


def _softmax_update(s, v, m_ref, l_ref, acc_ref, v_transposed=False):
    m_prev = m_ref[...]
    m_new = jnp.maximum(m_prev, jnp.max(s, axis=1, keepdims=True))
    alpha = jnp.exp(m_prev - m_new)
    p = jnp.exp(s - m_new)
    p16 = p.astype(BF16)
    pv = _dot_nt(p16, v) if v_transposed else _dot(p16, v)
    l_ref[...] = alpha * l_ref[...] + jnp.sum(p, axis=1, keepdims=True)
    acc_ref[...] = alpha * acc_ref[...] + pv
    m_ref[...] = m_new


def _softmax_init(m_ref, l_ref, acc_ref):
    m_ref[...] = jnp.full(m_ref.shape, -jnp.inf, F32)
    l_ref[...] = jnp.zeros(l_ref.shape, F32)
    acc_ref[...] = jnp.zeros(acc_ref.shape, F32)


def _attn_prompt_body(q_ref, kv_ref, kvt_ref, o_ref, m_ref, l_ref, acc_ref, sa_ref, sb_ref, *, tq):
    i = pl.program_id(1)
    heads = q_ref.shape[0]
    cols = heads * tq
    q = q_ref[...].reshape(cols, MLA_ROW)

    def scores(j, buf):
        k = kv_ref[pl.ds(pl.multiple_of(j * tq, tq), tq), :]
        buf[...] = _dot_nt(k, q)

    def absorb(j, buf, masked):
        st = buf[...]
        if masked:
            key = lax.broadcasted_iota(jnp.int32, (tq, cols), 0)
            tok = lax.broadcasted_iota(jnp.int32, (tq, cols), 1) % tq
            st = jnp.where(key <= tok, st, -jnp.inf)
        m_prev = m_ref[...]
        m_new = jnp.maximum(m_prev, jnp.max(st, axis=0, keepdims=True))
        alpha = jnp.exp(m_prev - m_new)
        p = jnp.exp(st - m_new)
        l_ref[...] = alpha * l_ref[...] + jnp.sum(p, axis=0, keepdims=True)
        acc_ref[...] = alpha * acc_ref[...] + _dot(kvt_ref[j], p.astype(BF16))
        m_ref[...] = m_new

    scores(0, sa_ref)
    _softmax_init(m_ref, l_ref, acc_ref)

    def body(p, carry):
        scores(2 * p + 1, sb_ref)
        absorb(2 * p, sa_ref, False)
        scores(2 * p + 2, sa_ref)
        absorb(2 * p + 1, sb_ref, False)
        return carry

    lax.fori_loop(0, i // 2, body, 0)

    @pl.when(i % 2 == 0)
    def _():
        absorb(i, sa_ref, True)

    @pl.when(i % 2 == 1)
    def _():
        scores(i, sb_ref)
        absorb(i - 1, sa_ref, False)
        absorb(i, sb_ref, True)

    o = (acc_ref[...] / l_ref[...]).T
    o_ref[...] = o.reshape(heads, tq, KV_LORA).astype(BF16)


def _attn_prompt(q, kv, batch, seq, tq):
    heads, m, _ = q.shape
    nq = seq // tq
    kvt = kv[:, :KV_LORA].reshape(batch * nq, tq, KV_LORA).transpose(0, 2, 1)
    return pl.pallas_call(
        functools.partial(_attn_prompt_body, tq=tq),
        grid=(batch, nq),
        in_specs=[
            pl.BlockSpec((heads, tq, MLA_ROW), lambda b, i: (0, b * nq + i, 0)),
            pl.BlockSpec((seq, MLA_ROW), lambda b, i: (b, 0)),
            pl.BlockSpec((nq, KV_LORA, tq), lambda b, i: (b, 0, 0)),
        ],
        out_specs=pl.BlockSpec((heads, tq, KV_LORA), lambda b, i: (0, b * nq + i, 0)),
        out_shape=jax.ShapeDtypeStruct((heads, m, KV_LORA), BF16),
        scratch_shapes=[
            pltpu.VMEM((1, heads * tq), F32),
            pltpu.VMEM((1, heads * tq), F32),
            pltpu.VMEM((KV_LORA, heads * tq), F32),
            pltpu.VMEM((tq, heads * tq), F32),
            pltpu.VMEM((tq, heads * tq), F32),
        ],
        compiler_params=_params("parallel", "parallel"),
        name="mla_attn_prompt",
    )(q, kv, kvt)


def _attn_decode_body(pt_ref, q_ref, kn_ref, cache_ref, o_ref, kbuf_ref, sem_ref,
                      m_ref, l_ref, acc_ref, *, pages, ts, layer_slot):
    b = pl.program_id(0)
    c = pl.program_id(1)
    n_chunks = pl.num_programs(1)
    step = b * n_chunks + c
    slot = step % 2

    def page_copy(page, p, sl):
        return pltpu.make_async_copy(cache_ref.at[layer_slot, page], kbuf_ref.at[sl, p],
                                     sem_ref.at[sl])

    def start_pages(bb, cc, sl):
        def body(p, carry):
            page_copy(pt_ref[bb, cc * pages + p], p, sl).start()
            return carry
        lax.fori_loop(0, pages, body, 0, unroll=8)

    @pl.when(step == 0)
    def _():
        start_pages(0, 0, 0)

    @pl.when(step + 1 < pl.num_programs(0) * n_chunks)
    def _():
        nxt = step + 1
        start_pages(nxt // n_chunks, nxt % n_chunks, 1 - slot)

    def wait_body(p, carry):
        page_copy(0, p, slot).wait()
        return carry

    lax.fori_loop(0, pages, wait_body, 0, unroll=8)

    @pl.when(c == 0)
    def _():
        _softmax_init(m_ref, l_ref, acc_ref)

    q = q_ref[0]
    group = DEC_PAGES_PER_GROUP
    kts, scores = [], []
    for g in range(pages // group):
        kt = jnp.concatenate([kbuf_ref[slot, p].astype(BF16)
                              for p in range(g * group, (g + 1) * group)],
                             axis=1)
        kts.append(kt)
        scores.append(_dot(q, kt))
    s = jnp.concatenate(scores, axis=1)
    m_prev = m_ref[...]
    m_new = jnp.maximum(m_prev, jnp.max(s, axis=1, keepdims=True))
    alpha = jnp.exp(m_prev - m_new)
    p = jnp.exp(s - m_new)
    l_ref[...] = alpha * l_ref[...] + jnp.sum(p, axis=1, keepdims=True)
    p16 = p.astype(BF16)
    width = group * PAGE_SIZE
    pv = sum(_dot_nt(p16[:, g * width:(g + 1) * width], kt[:KV_LORA, :])
             for g, kt in enumerate(kts))
    acc_ref[...] = alpha * acc_ref[...] + pv
    m_ref[...] = m_new

    @pl.when(c == pl.num_programs(1) - 1)
    def _():
        kn = kn_ref[0]
        s = _dot_nt(q, kn)
        tok = lax.broadcasted_iota(jnp.int32, s.shape, 0) % ts
        key = lax.broadcasted_iota(jnp.int32, s.shape, 1)
        s = jnp.where(key <= tok, s, -jnp.inf)
        _softmax_update(s, kn[:, :KV_LORA], m_ref, l_ref, acc_ref)
        o_ref[0] = (acc_ref[...] / l_ref[...]).astype(BF16)


def _attn_decode(q, k_new, cache_t, page_table, layer_slot, ts):
    bs, rows, _ = q.shape
    n_pages = page_table.shape[1]
    pages = DEC_PAGES_PER_STEP
    assert n_pages % pages == 0
    grid_spec = pltpu.PrefetchScalarGridSpec(
        num_scalar_prefetch=1,
        grid=(bs, n_pages // pages),
        in_specs=[
            pl.BlockSpec((1, rows, MLA_ROW), lambda b, c, pt: (b, 0, 0)),
            pl.BlockSpec((1, k_new.shape[1], MLA_ROW), lambda b, c, pt: (b, 0, 0)),
            pl.BlockSpec(memory_space=pl.ANY),
        ],
        out_specs=pl.BlockSpec((1, rows, KV_LORA), lambda b, c, pt: (b, 0, 0)),
        scratch_shapes=[
            pltpu.VMEM((2, pages, MLA_ROW, PAGE_SIZE), cache_t.dtype),
            pltpu.SemaphoreType.DMA((2,)),
            pltpu.VMEM((rows, 1), F32),
            pltpu.VMEM((rows, 1), F32),
            pltpu.VMEM((rows, KV_LORA), F32),
        ],
    )
    return pl.pallas_call(
        functools.partial(_attn_decode_body, pages=pages, ts=ts, layer_slot=layer_slot),
        grid_spec=grid_spec,
        out_shape=jax.ShapeDtypeStruct((bs, rows, KV_LORA), BF16),
        compiler_params=_params("arbitrary", "arbitrary"),
        name="mla_attn_decode",
    )(page_table, q, k_new, cache_t)


def _layer_tail_body(*refs, from_latent, d_ff, chunk):
    if from_latent:
        (a_ref, wuv_ref, wo_ref, h_ref, nw_mix_ref, nw_in_ref, win_ref, wout_ref, nw_out_ref,
         out_ref) = refs
    else:
        (a_ref, wo_ref, h_ref, nw_mix_ref, nw_in_ref, win_ref, wout_ref, nw_out_ref,
         out_ref) = refs
    tm = h_ref.shape[0]
    n_chains = TAIL_ROW_CHAINS if tm % (TAIL_ROW_CHAINS * 2 * SUBLANES) == 0 else 1
    rows_per = tm // n_chains
    chains = [dict(rows=slice(r * rows_per, (r + 1) * rows_per)) for r in range(n_chains)]
    for ch in chains:
        rows = ch["rows"]
        if from_latent:
            a16 = jnp.concatenate(
                [_dot(a_ref[h, rows, :], wuv_ref[h]).astype(BF16) for h in range(MLA_HEADS)],
                axis=1)
        else:
            a16 = a_ref[rows, :]
        ch["mix"] = _dot(a16, wo_ref[...])
    for ch in chains:
        x = h_ref[ch["rows"], :] + _rms(ch["mix"], nw_mix_ref[...])
        ch.update(x=x, u=_rms(x, nw_in_ref[...]).astype(BF16), acc=jnp.zeros(x.shape, F32))
    for c in range(d_ff // chunk):
        lo = c * chunk
        for ch in chains:
            gate = _dot(ch["u"], win_ref[0, :, lo:lo + chunk])
            up = _dot(ch["u"], win_ref[0, :, d_ff + lo:d_ff + lo + chunk])
            ch["act"] = (_silu(gate) * up).astype(BF16)
        for ch in chains:
            ch["acc"] = ch["acc"] + _dot(ch["act"], wout_ref[0, lo:lo + chunk, :])
    for ch in chains:
        out_ref[ch["rows"], :] = ch["x"] + _rms(ch["acc"], nw_out_ref[...])


def _layer_tail(a16, w_uv, w_o, h, nw_mix, ffn, tm):
    nw_in, w_in, w_out, layer, nw_out = ffn
    m = h.shape[0]
    d_ff = w_out.shape[1]
    assert d_ff % FFN_CHUNK == 0
    from_latent = w_uv is not None
    row = lambda i: (i, 0)
    if from_latent:
        in_specs = [pl.BlockSpec((MLA_HEADS, tm, KV_LORA), lambda i: (0, i, 0)),
                    _const_spec(w_uv.shape)]
        args = [a16, w_uv]
    else:
        in_specs = [pl.BlockSpec((tm, a16.shape[1]), row)]
        args = [a16]
    in_specs += [
        _const_spec(w_o.shape),
        pl.BlockSpec((tm, D_MODEL), row),
        _const_spec((1, D_MODEL)),
        _const_spec((1, D_MODEL)),
        _fixed_spec((1,) + w_in.shape[1:], (layer, 0, 0)),
        _fixed_spec((1,) + w_out.shape[1:], (layer, 0, 0)),
        _const_spec((1, D_MODEL)),
    ]
    args += [w_o, h, nw_mix, nw_in, w_in, w_out, nw_out]
    return pl.pallas_call(
        functools.partial(_layer_tail_body, from_latent=from_latent, d_ff=d_ff, chunk=FFN_CHUNK),
        grid=(m // tm,),
        in_specs=in_specs,
        out_specs=pl.BlockSpec((tm, D_MODEL), row),
        out_shape=jax.ShapeDtypeStruct((m, D_MODEL), F32),
        compiler_params=_params("parallel"),
        name="layer_tail",
    )(*args)


def _dn_proj_body(*refs, tm, seq_len, has_hist, tail_rows):
    if has_hist:
        (x_ref, nw_ref, wqkv_ref, wz_ref, wba_ref, cw_ref, alog_ref, dtb_ref, hist_ref,
         qkv_ref, z_ref, bg_ref, tail_ref, ext_ref) = refs
    else:
        (x_ref, nw_ref, wqkv_ref, wz_ref, wba_ref, cw_ref, alog_ref, dtb_ref,
         qkv_ref, z_ref, bg_ref, tail_ref, ext_ref) = refs
        hist_ref = None
    i = pl.program_id(0)
    carry = CONV_CARRY_ROWS

    @pl.when((i * tm) % seq_len == 0)
    def _():
        ext_ref[0:carry, :] = jnp.zeros((carry, DN_QKV), F32)

    u = _rms(x_ref[...], nw_ref[...]).astype(BF16)
    assert carry == SUBLANES and tm % (DN_PROJ_ROW_CHAINS * SUBLANES) == 0
    rows_per = tm // DN_PROJ_ROW_CHAINS
    assert rows_per % seq_len == 0 or not has_hist
    raws = []
    for r in range(DN_PROJ_ROW_CHAINS):
        raws.append(_dot(u[r * rows_per:(r + 1) * rows_per], wqkv_ref[...]))
        if r == DN_PROJ_GATE_SLOT:
            z_ref[...] = _dot(u, wz_ref[...])
            ba = _dot(u, wba_ref[...])
    sub = lax.broadcasted_iota(jnp.int32, (1, SUBLANES, 1), 1)
    cw = cw_ref[...]
    before = ext_ref[...]
    for r, raw in enumerate(raws):
        rows = slice(r * rows_per, (r + 1) * rows_per)
        cur = raw.reshape(rows_per // SUBLANES, SUBLANES, DN_QKV)
        prev = jnp.concatenate([before.reshape(1, SUBLANES, DN_QKV), cur[:-1]], axis=0)
        acc = raw * cw[CONV_W - 1:CONV_W, :]
        for k in range(1, CONV_W):
            mixed = jnp.where(sub >= SUBLANES - k, prev, cur)
            shifted = pltpu.roll(mixed, k, axis=1).reshape(rows_per, DN_QKV)
            if has_hist:
                tpos = lax.broadcasted_iota(jnp.int32, (rows_per, 1), 0) % seq_len
                shifted = jnp.where(tpos >= k, shifted, hist_ref[k - 1, rows, :])
            acc = acc + shifted * cw[CONV_W - 1 - k:CONV_W - k, :]
        before = raw[rows_per - carry:, :]
        act = _silu(acc)
        for h in range(DN_HEADS):
            qh = act[:, h * DN_DK:(h + 1) * DN_DK]
            kh = act[:, DN_HK + h * DN_DK:DN_HK + (h + 1) * DN_DK]
            qn = qh * lax.rsqrt(jnp.sum(qh * qh, axis=-1, keepdims=True) + L2_EPS) * (DN_DK ** -0.5)
            kn = kh * lax.rsqrt(jnp.sum(kh * kh, axis=-1, keepdims=True) + L2_EPS)
            qkv_ref[rows, h * DN_DK:(h + 1) * DN_DK] = qn
            qkv_ref[rows, DN_HK + h * DN_DK:DN_HK + (h + 1) * DN_DK] = kn
        qkv_ref[rows, 2 * DN_HK:] = act[:, 2 * DN_HK:]
    ext_ref[...] = before
    tail_ref[0] = jnp.concatenate(raws, axis=0)[tm - tail_rows:, :]
    beta = _sigmoid(ba[:, :DN_HEADS])
    g = -jnp.exp(alog_ref[...]) * _softplus(ba[:, DN_HEADS:2 * DN_HEADS] + dtb_ref[...])
    bg_ref[:, :DN_HEADS] = beta
    bg_ref[:, DN_HEADS:] = g


def _dn_project(x, nw, w_all, w_ba, conv_w, a_log, dt_bias, hist, tm, seq_len, tail_rows):
    assert DN_QKV % DN_Z == 0
    m = x.shape[0]
    n_tiles = m // tm
    has_hist = hist is not None
    assert (tm % seq_len == 0 and n_tiles == 1) if has_hist else seq_len % tm == 0
    tiles_per_tail = max(seq_len // tm, 1)
    row = lambda i: (i, 0)
    in_specs = [
        pl.BlockSpec((tm, D_MODEL), row),
        _const_spec((1, D_MODEL)),
        _fixed_spec((D_MODEL, DN_QKV), (0, 0)),
        _fixed_spec((D_MODEL, DN_Z), (0, DN_QKV // DN_Z)),
        _const_spec(w_ba.shape),
        _const_spec(conv_w.shape),
        _const_spec((1, DN_HEADS)),
        _const_spec((1, DN_HEADS)),
    ]
    args = [x, nw, w_all, w_all, w_ba, conv_w, a_log, dt_bias]
    if has_hist:
        in_specs.append(pl.BlockSpec((CONV_W - 1, tm, DN_QKV), lambda i: (0, i, 0)))
        args.append(hist)
    return pl.pallas_call(
        functools.partial(_dn_proj_body, tm=tm, seq_len=seq_len, has_hist=has_hist,
                          tail_rows=tail_rows),
        grid=(n_tiles,),
        in_specs=in_specs,
        out_specs=[
            pl.BlockSpec((tm, DN_QKV), row),
            pl.BlockSpec((tm, DN_Z), row),
            pl.BlockSpec((tm, 2 * DN_HEADS), row),
            pl.BlockSpec((1, tail_rows, DN_QKV), lambda i: (i // tiles_per_tail, 0, 0)),
        ],
        out_shape=[
            jax.ShapeDtypeStruct((m, DN_QKV), F32),
            jax.ShapeDtypeStruct((m, DN_Z), F32),
            jax.ShapeDtypeStruct((m, 2 * DN_HEADS), F32),
            jax.ShapeDtypeStruct((n_tiles // tiles_per_tail, tail_rows, DN_QKV), F32),
        ],
        scratch_shapes=[pltpu.VMEM((CONV_CARRY_ROWS, DN_QKV), F32)],
        compiler_params=_params("arbitrary"),
        name="dn_project",
    )(*args)


def _stack_heads(load, g):
    return jnp.concatenate(
        [load(slice((g * GDN_GROUP + hh) * DN_DK, (g * GDN_GROUP + hh + 1) * DN_DK))
         for hh in range(GDN_GROUP)], axis=0)


def _gdn_local_body(q_ref, k_ref, v_ref, bg_ref, bgt_ref,
                    u_ref, w_ref, qd_ref, kdt_ref, qk_ref, egl_ref, *, chunk, chunks_per_step):
    n_heads = DN_HEADS
    stack = GDN_GROUP * chunk
    ri = lax.broadcasted_iota(jnp.int32, (stack, stack), 0)
    ci = lax.broadcasted_iota(jnp.int32, (stack, stack), 1)
    same_head = (ri // chunk) == (ci // chunk)
    causal = same_head & (ri >= ci)
    strict = same_head & (ri > ci)
    r1 = lax.broadcasted_iota(jnp.int32, (chunk, chunk), 0)
    c1 = lax.broadcasted_iota(jnp.int32, (chunk, chunk), 1)
    tril = jnp.where(r1 >= c1, 1.0, 0.0).astype(BF16)
    triu = jnp.where(r1 <= c1, 1.0, 0.0).astype(BF16)
    n_double = int(math.log2(chunk)) - 2

    def store_heads(ref, rows, g, stacked):
        for hh in range(GDN_GROUP):
            h = g * GDN_GROUP + hh
            ref[rows, h * DN_DK:(h + 1) * DN_DK] = stacked[hh * chunk:(hh + 1) * chunk, :]

    chains = []
    for cp in range(chunks_per_step):
        rows = slice(cp * chunk, (cp + 1) * chunk)
        bg = bg_ref[rows, :]
        bgt = bgt_ref[cp]
        gc_col = sum(_dot(tril, part) for part in _split3(bg))
        gc_row = sum(_dot(part, triu) for part in _split3(bgt))
        egl_ref[cp] = jnp.broadcast_to(jnp.exp(gc_row[n_heads:, chunk - 1:chunk]),
                                       (n_heads, LANES))
        for g in range(n_heads // GDN_GROUP):
            heads = [g * GDN_GROUP + hh for hh in range(GDN_GROUP)]
            q = _stack_heads(lambda sl: q_ref[rows, sl], g)
            k = _stack_heads(lambda sl: k_ref[rows, sl], g)
            v = _stack_heads(lambda sl: v_ref[rows, sl], g)
            beta = jnp.concatenate([bg[:, h:h + 1] for h in heads], axis=0)
            gcc = jnp.concatenate([gc_col[:, n_heads + h:n_heads + h + 1] for h in heads], axis=0)
            gcr = jnp.concatenate([gc_row[n_heads + h:n_heads + h + 1, :] for h in heads], axis=1)
            g_last = jnp.concatenate(
                [jnp.broadcast_to(gc_col[chunk - 1:chunk, n_heads + h:n_heads + h + 1], (chunk, 1))
                 for h in heads], axis=0)
            eg = jnp.exp(gcc)
            kb = k * beta
            store_heads(qd_ref, rows, g, (q * eg).astype(BF16))
            kdt_ref[cp, :, g * stack:(g + 1) * stack] = (k * jnp.exp(g_last - gcc)).T.astype(BF16)
            chains.append(dict(
                rows=rows, g=g, q16=q.astype(BF16), k16=k.astype(BF16), kb16=kb.astype(BF16),
                decay=jnp.where(causal, jnp.exp(gcc - gcr), 0.0),
                rhs=jnp.concatenate([v * beta, kb * eg], axis=1)))

    for ch in chains:
        lower = jnp.where(strict, _dot_nt(ch["kb16"], ch["k16"]) * ch["decay"], 0.0)
        ch.update(lower=lower, power=lower, off=-lower)
    for ch in chains:
        qk = jnp.where(causal, _dot_nt(ch["q16"], ch["k16"]) * ch["decay"], 0.0)
        qk_packed = sum(qk[hh * chunk:(hh + 1) * chunk, :] for hh in range(GDN_GROUP))
        qk_ref[ch["rows"], ch["g"] * stack:(ch["g"] + 1) * stack] = qk_packed.astype(BF16)
    for _ in range(n_double):
        for ch in chains:
            p16 = ch["power"].astype(BF16)
            ch["power"] = _dot(p16, p16)
        for ch in chains:
            ch["off"] = ch["off"] + ch["power"] + _dot(ch["off"].astype(BF16),
                                                       ch["power"].astype(BF16))
    for ch in chains:
        ch["resid"] = (ch["lower"] + ch["off"]) + _dot_hl(ch["lower"], ch["off"])
    for ch in chains:
        ch["off"] = ch["off"] - ch["resid"] - _dot(ch["off"].astype(BF16),
                                                   ch["resid"].astype(BF16))
    for ch in chains:
        sol = ch["rhs"] + _dot(ch["off"].astype(BF16), ch["rhs"].astype(BF16))
        store_heads(u_ref, ch["rows"], ch["g"], sol[:, :DN_DV])
        store_heads(w_ref, ch["rows"], ch["g"], sol[:, DN_DV:].astype(BF16))


def _gdn_scan_body(u_ref, w_ref, qd_ref, kdt_ref, qk_ref, egl_ref, z_ref, gout_ref, s0_ref,
                   y_ref, s_ref, *, chunk, seqs):
    c = pl.program_id(1)

    @pl.when(c == 0)
    def _():
        s_ref[...] = s0_ref[...]

    stack = GDN_GROUP * chunk
    pair = 2 * DN_DK
    lhs_mask = ((lax.broadcasted_iota(jnp.int32, (4 * chunk, pair), 0) // chunk) % 2
                == lax.broadcasted_iota(jnp.int32, (4 * chunk, pair), 1) // DN_DK)
    qk_mask = (lax.broadcasted_iota(jnp.int32, (stack, stack), 0) // chunk
               == lax.broadcasted_iota(jnp.int32, (stack, stack), 1) // chunk)
    kd_mask = (lax.broadcasted_iota(jnp.int32, (GDN_GROUP * DN_DK, stack), 0) // DN_DK
               == lax.broadcasted_iota(jnp.int32, (GDN_GROUP * DN_DK, stack), 1) // chunk)
    zero16 = jnp.zeros((), BF16)
    chains = [dict(sq=sq, g=g) for sq in range(seqs) for g in range(DN_HEADS // GDN_GROUP)]
    for ch in chains:
        sq, g = ch["sq"], ch["g"]
        s = s_ref[sq, g * GDN_GROUP:(g + 1) * GDN_GROUP].reshape(GDN_GROUP * DN_DK, DN_DV)
        s16 = s.astype(BF16)
        ws, qs = [], []
        for p in range(GDN_GROUP // 2):
            cols = [slice((g * GDN_GROUP + 2 * p + e) * DN_DK, (g * GDN_GROUP + 2 * p + e + 1) * DN_DK)
                    for e in range(2)]
            lhs = jnp.concatenate([w_ref[sq, :, cols[0]], w_ref[sq, :, cols[1]],
                                   qd_ref[sq, :, cols[0]], qd_ref[sq, :, cols[1]]], axis=0)
            lhs_bd = jnp.where(lhs_mask, jnp.concatenate([lhs, lhs], axis=1), zero16)
            res = _dot(lhs_bd, s16[p * pair:(p + 1) * pair, :])
            ws.append(res[:2 * chunk])
            qs.append(res[2 * chunk:])
        ch.update(s=s, ws=jnp.concatenate(ws, axis=0), qs=jnp.concatenate(qs, axis=0))
    for ch in chains:
        sq, g = ch["sq"], ch["g"]
        v_new = _stack_heads(lambda sl: u_ref[sq, :, sl], g) - ch["ws"]
        ch["v16"] = v_new.astype(BF16)
    gout = gout_ref[...]
    for ch in chains:
        sq, g = ch["sq"], ch["g"]
        qk = qk_ref[sq, :, g * stack:(g + 1) * stack]
        qk_bd = jnp.where(qk_mask, jnp.concatenate([qk] * GDN_GROUP, axis=0), zero16)
        o = ch["qs"] + _dot(qk_bd, ch["v16"])
        y = _rms(o, gout) * _silu(_stack_heads(lambda sl: z_ref[sq, :, sl], g))
        y16 = y.astype(BF16)
        for hh in range(GDN_GROUP):
            h = g * GDN_GROUP + hh
            y_ref[sq, :, h * DN_DV:(h + 1) * DN_DV] = y16[hh * chunk:(hh + 1) * chunk, :]
    for ch in chains:
        sq, g = ch["sq"], ch["g"]
        kdt = kdt_ref[sq, 0, :, g * stack:(g + 1) * stack]
        kdt_bd = jnp.where(kd_mask, jnp.concatenate([kdt] * GDN_GROUP, axis=0), zero16)
        gate = jnp.concatenate(
            [jnp.broadcast_to(egl_ref[sq, 0, g * GDN_GROUP + hh:g * GDN_GROUP + hh + 1, :],
                              (DN_DK, DN_DV)) for hh in range(GDN_GROUP)], axis=0)
        s_new = ch["s"] * gate + _dot(kdt_bd, ch["v16"])
        s_ref[sq, g * GDN_GROUP:(g + 1) * GDN_GROUP] = s_new.reshape(GDN_GROUP, DN_DK, DN_DV)


def _gdn(qkv, bg, z, g_out, s0, n_seq, seq_len, chunk):
    m = qkv.shape[0]
    n_chunks = m // chunk
    nc = seq_len // chunk
    cps = GDN_CHUNKS_PER_STEP
    sps = GDN_SEQS_PER_STEP
    assert n_chunks % cps == 0 and n_seq % sps == 0
    bgt = bg.reshape(n_chunks, chunk, 2 * DN_HEADS).transpose(0, 2, 1)
    half = DN_HEADS * chunk
    row = lambda i: (i, 0)
    blk = lambda col: pl.BlockSpec((cps * chunk, DN_HK), lambda i, col=col: (i, col))
    u, w16, qd16, kdt16, qk16, egl = pl.pallas_call(
        functools.partial(_gdn_local_body, chunk=chunk, chunks_per_step=cps),
        grid=(n_chunks // cps,),
        in_specs=[
            blk(0), blk(1), blk(2),
            pl.BlockSpec((cps * chunk, 2 * DN_HEADS), row),
            pl.BlockSpec((cps, 2 * DN_HEADS, chunk), lambda i: (i, 0, 0)),
        ],
        out_specs=[
            pl.BlockSpec((cps * chunk, DN_Z), row),
            pl.BlockSpec((cps * chunk, DN_HK), row),
            pl.BlockSpec((cps * chunk, DN_HK), row),
            pl.BlockSpec((cps, DN_DK, half), lambda i: (i, 0, 0)),
            pl.BlockSpec((cps * chunk, half), row),
            pl.BlockSpec((cps, DN_HEADS, LANES), lambda i: (i, 0, 0)),
        ],
        out_shape=[
            jax.ShapeDtypeStruct((m, DN_Z), F32),
            jax.ShapeDtypeStruct((m, DN_HK), BF16),
            jax.ShapeDtypeStruct((m, DN_HK), BF16),
            jax.ShapeDtypeStruct((n_chunks, DN_DK, half), BF16),
            jax.ShapeDtypeStruct((m, half), BF16),
            jax.ShapeDtypeStruct((n_chunks, DN_HEADS, LANES), F32),
        ],
        compiler_params=_params("parallel"),
        name="gdn_local",
    )(qkv, qkv, qkv, bg, bgt)

    per_seq = lambda a: a.reshape((n_seq, nc if a.ndim == 3 else seq_len) + a.shape[1:])
    tok_spec = lambda width: pl.BlockSpec((sps, chunk, width), lambda b, c: (b, c, 0))
    lead_spec = lambda d1, d2: pl.BlockSpec((sps, 1, d1, d2), lambda b, c: (b, c, 0, 0))
    state_spec = pl.BlockSpec((sps, DN_HEADS, DN_DK, DN_DV), lambda b, c: (b, 0, 0, 0))
    y16, s = pl.pallas_call(
        functools.partial(_gdn_scan_body, chunk=chunk, seqs=sps),
        grid=(n_seq // sps, nc),
        in_specs=[
            tok_spec(DN_Z),
            tok_spec(DN_HK),
            tok_spec(DN_HK),
            lead_spec(DN_DK, half),
            tok_spec(half),
            lead_spec(DN_HEADS, LANES),
            tok_spec(DN_Z),
            pl.BlockSpec((1, DN_DV), lambda b, c: (0, 0)),
            state_spec,
        ],
        out_specs=[tok_spec(DN_Z), state_spec],
        out_shape=[
            jax.ShapeDtypeStruct((n_seq, seq_len, DN_Z), BF16),
            jax.ShapeDtypeStruct(s0.shape, F32),
        ],
        compiler_params=_params("parallel", "arbitrary"),
        name="gdn_scan",
    )(per_seq(u), per_seq(w16), per_seq(qd16), per_seq(kdt16), per_seq(qk16), per_seq(egl),
      per_seq(z), g_out, s0)
    return y16.reshape(m, DN_Z), s


def _mla_layer(hp, hs, cache_mla, page_table, slot, nw, w_in, g_q, g_kv, w_uq, w_uk, w_uv, w_o,
               ffn, bp, tp, bs, ts):
    past = page_table.shape[1] * PAGE_SIZE
    w_in16 = w_in.astype(BF16)
    w_uq_heads = w_uq.reshape(Q_LORA, MLA_HEADS, QK_NOPE + QK_ROPE)
    w_uq16 = jnp.concatenate(
        [w_uq_heads[:, :, :QK_NOPE].reshape(Q_LORA, MLA_HEADS * QK_NOPE),
         w_uq_heads[:, :, QK_NOPE:].reshape(Q_LORA, MLA_HEADS * QK_ROPE)], axis=1).astype(BF16)
    w_ukt16 = jnp.swapaxes(w_uk, 1, 2).astype(BF16)
    w_uv16 = w_uv.astype(BF16)
    w_o16 = w_o.astype(BF16)
    g_q = g_q.reshape(1, Q_LORA)
    g_kv = g_kv.reshape(1, KV_LORA)
    nw0 = nw[0].reshape(1, D_MODEL)
    nw1 = nw[1].reshape(1, D_MODEL)

    cs_p, sn_p = _rope_tables(jnp.arange(tp))
    cs_s, sn_s = _rope_tables(past + jnp.tile(jnp.arange(ts), bs))
    ms = bs * ts
    rows_p, kv_p, q_p = _mla_project(hp, nw0, w_in16, g_q, g_kv, w_uq16, w_ukt16, cs_p, sn_p,
                                     TM_TOKENS)
    rows_s, kv_s, q_s = _mla_project(hs, nw0, w_in16, g_q, g_kv, w_uq16, w_ukt16, cs_s, sn_s, ms)

    o_p = _attn_prompt(q_p, kv_p, bp, tp, TQ_ATTN)

    q_sb = q_s.reshape(MLA_HEADS, bs, ts, MLA_ROW).transpose(1, 0, 2, 3).reshape(
        bs, MLA_HEADS * ts, MLA_ROW)
    new_rows = 2 * SUBLANES
    k_new = jnp.pad(kv_s.reshape(bs, ts, MLA_ROW), ((0, 0), (0, new_rows - ts), (0, 0)))
    o_sb = _attn_decode(q_sb, k_new, jnp.swapaxes(cache_mla, 2, 3), page_table, slot, ts)
    o_s = o_sb.reshape(bs, MLA_HEADS, ts, KV_LORA).transpose(1, 0, 2, 3).reshape(
        MLA_HEADS, ms, KV_LORA)

    hp = _layer_tail(o_p, w_uv16, w_o16, hp, nw1, ffn, TM_TOKENS)
    hs = _layer_tail(o_s, w_uv16, w_o16, hs, nw1, ffn, ms)
    return hp, hs, rows_p.reshape(bp, tp, MLA_ROW), rows_s.reshape(bs, ts, MLA_ROW)


def _dn_layer(hp, hs, s0_s, conv0_s, nw, w_in, conv_w, a_log, dt_bias, g_out, w_o, ffn,
              bp, tp, bs, ts):
    w_in16 = w_in.astype(BF16)
    w_ba16 = jnp.pad(w_in[:, DN_QKV + DN_Z:], ((0, 0), (0, LANES - 2 * DN_HEADS))).astype(BF16)
    w_o16 = w_o.astype(BF16)
    a_log = a_log.reshape(1, DN_HEADS).astype(F32)
    dt_bias = dt_bias.reshape(1, DN_HEADS).astype(F32)
    g_out = g_out.reshape(1, DN_DV)
    nw0 = nw[0].reshape(1, D_MODEL)
    nw1 = nw[1].reshape(1, D_MODEL)
    hist_rows = CONV_W - 1
    ms = bs * ts

    qkv_p, z_p, bg_p, tail_p = _dn_project(hp, nw0, w_in16, w_ba16, conv_w, a_log, dt_bias,
                                           None, TM_DN_PROJ, tp, SUBLANES)
    conv_p = tail_p[:, SUBLANES - hist_rows:, :]
    s0_p = jnp.zeros((bp, DN_HEADS, DN_DK, DN_DV), F32)
    y_p, s_p = _gdn(qkv_p, bg_p, z_p, g_out, s0_p, bp, tp, DN_CHUNK)
    hp = _layer_tail(y_p, None, w_o16, hp, nw1, ffn, TM_TOKENS)

    tok = jnp.arange(ts)
    hist = jnp.stack([
        conv0_s[:, jnp.clip(hist_rows - k + tok, 0, hist_rows - 1), :].reshape(ms, DN_QKV)
        for k in range(1, CONV_W)
    ]).astype(F32)
    qkv_s, z_s, bg_s, tail_s = _dn_project(hs, nw0, w_in16, w_ba16, conv_w, a_log, dt_bias,
                                           hist, ms, ts, ms)
    raw_s = tail_s.reshape(bs, ts, DN_QKV)
    conv_s = jnp.concatenate([conv0_s.astype(F32), raw_s], axis=1)[:, ts:, :]
    cs = DN_CHUNK_SAMPLE
    assert ts <= cs
    pad = lambda a: jnp.pad(a.reshape(bs, ts, -1), ((0, 0), (0, cs - ts), (0, 0))).reshape(
        bs * cs, -1)
    y_s_pad, s_s = _gdn(pad(qkv_s), pad(bg_s), pad(z_s), g_out, s0_s.astype(F32), bs, cs, cs)
    y_s = y_s_pad.reshape(bs, cs, DN_Z)[:, :ts, :].reshape(ms, DN_Z)
    hs = _layer_tail(y_s, None, w_o16, hs, nw1, ffn, ms)
    return hp, hs, s_p, s_s, conv_p, conv_s


def kernel(x_prompt, x_sample, cache_mla, state_dn, state_dn_conv, page_table, norm_w, mla_w_in,
           mla_g_q, mla_g_kv, mla_w_uq, mla_w_uk, mla_w_uv, mla_w_o, dn_w_in, dn_conv_w, dn_a_log,
           dn_dt_bias, dn_g_out, dn_w_o, ffn_w_in, ffn_w_out):
    bp, tp, _ = x_prompt.shape
    bs, ts, _ = x_sample.shape
    depth = norm_w.shape[0]
    hp = x_prompt.reshape(bp * tp, D_MODEL)
    hs = x_sample.reshape(bs * ts, D_MODEL)
    rows_p_l, rows_s_l, sp_l, ss_l, cp_l, cs_l = [], [], [], [], [], []
    ffn_w_in16 = ffn_w_in.astype(BF16)
    ffn_w_out16 = ffn_w_out.astype(BF16)
    for layer in range(depth):
        j = layer // N_MIXERS
        nw = norm_w[layer]
        ffn = (nw[2].reshape(1, D_MODEL), ffn_w_in16, ffn_w_out16, layer, nw[3].reshape(1, D_MODEL))
        if layer % N_MIXERS == 0:
            hp, hs, rows_p, rows_s = _mla_layer(
                hp, hs, cache_mla, page_table, j, nw, mla_w_in[j], mla_g_q[j], mla_g_kv[j],
                mla_w_uq[j], mla_w_uk[j], mla_w_uv[j], mla_w_o[j], ffn, bp, tp, bs, ts)
            rows_p_l.append(rows_p)
            rows_s_l.append(rows_s)
        else:
            hp, hs, s_p, s_s, c_p, c_s = _dn_layer(
                hp, hs, state_dn[j], state_dn_conv[j], nw, dn_w_in[j], dn_conv_w[j], dn_a_log[j],
                dn_dt_bias[j], dn_g_out[j], dn_w_o[j], ffn, bp, tp, bs, ts)
            sp_l.append(s_p.astype(state_dn.dtype))
            ss_l.append(s_s.astype(state_dn.dtype))
            cp_l.append(c_p.astype(state_dn_conv.dtype))
            cs_l.append(c_s.astype(state_dn_conv.dtype))
    return (hp.reshape(bp, tp, D_MODEL), hs.reshape(bs, ts, D_MODEL),
            jnp.stack(rows_p_l), jnp.stack(rows_s_l), jnp.stack(sp_l), jnp.stack(ss_l),
            jnp.stack(cp_l), jnp.stack(cs_l))
```
